```python
import math
import jax
import jax.numpy as jnp
from jax import lax
import numpy as np

D_MODEL = 1024
BATCH = 4
SEQ = 4096
DEPTH = 4
DEC_BATCH = 128
DEC_SEQ = 1
PAST_LEN = 2048
PAGE_SIZE = 128

N_EVEN = (DEPTH + 1) // 2
N_ODD = DEPTH // 2
H_A = 4
D_HA = 64
DA = 2 * D_HA
W_A = H_A * DA
H_B = 4
DK_B = 128
DV_B = 128
W_B = H_B * DV_B
QKV_B = 2 * H_B * DK_B + H_B * DV_B
GDN_CONV = 4
GDN_CHUNK = 64
D_C = D_MODEL
SC_WIDTH = 3
NUM_BUCKETS = 32
MAX_EXACT = NUM_BUCKETS // 2
MAX_DISTANCE = 128
Q_BLOCK = 128
EPS = 1e-6
NEG = -1e30
EVEN_SIZES = (W_A, W_A, W_A, W_A, QKV_B, W_B, H_B, H_B)
P_EVEN = sum(EVEN_SIZES)
P_ODD = 4 * D_C

kernel_name = 'hybrid_diffattn_gdn_shortconv_step'


def split_cols(p, sizes):
    idx, acc = [], 0
    for s in sizes[:-1]:
        acc += s
        idx.append(acc)
    return jnp.split(p, idx, axis=-1)


def rmsnorm(x, w):
    xf = x.astype(jnp.float32)
    y = xf * lax.rsqrt(jnp.mean(xf * xf, axis=-1, keepdims=True) + EPS)
    return (y * w.astype(jnp.float32)).astype(x.dtype)


def l2norm(x):
    xf = x.astype(jnp.float32)
    return xf * lax.rsqrt(jnp.sum(xf * xf, axis=-1, keepdims=True) + EPS)


def causal_conv(x, buf, w):
    W = w.shape[0]
    T = x.shape[1]
    xp = jnp.concatenate([buf.astype(x.dtype), x], axis=1)
    y = xp[:, 0:T] * w[0]
    for j in range(1, W):
        y = y + xp[:, j:j + T] * w[j]
    return y, xp[:, T:]


def t5_bucket(rel):
    n = jnp.maximum(rel, 0)
    nf = jnp.maximum(n, 1).astype(jnp.float32)
    large = MAX_EXACT + (jnp.log(nf / MAX_EXACT) / math.log(MAX_DISTANCE / MAX_EXACT)
                         * (NUM_BUCKETS - MAX_EXACT)).astype(jnp.int32)
    large = jnp.minimum(large, NUM_BUCKETS - 1)
    return jnp.where(n < MAX_EXACT, n, large)


def diff_attention(q, k, v, q_pos, k_pos, lam, rel_table):
    B, T, H, _ = q.shape
    QB = math.gcd(Q_BLOCK, T)
    nb = T // QB
    qb = jnp.swapaxes(q.reshape(B, nb, QB, H, DA), 0, 1)
    pb = q_pos.reshape(nb, QB)
    k1, k2 = k[..., :D_HA], k[..., D_HA:]
    scale = D_HA ** -0.5

    def one_block(args):
        qi, pi = args
        bias = jnp.transpose(rel_table[t5_bucket(pi[:, None] - k_pos[None, :])].astype(jnp.float32), (2, 0, 1))
        mask = k_pos[None, :] <= pi[:, None]

        def probs(qa, ka):
            s = jnp.einsum('bqhd,bkhd->bhqk', qa, ka).astype(jnp.float32) * scale + bias
            return jax.nn.softmax(jnp.where(mask, s, NEG), axis=-1)

        a = probs(qi[..., :D_HA], k1) - lam * probs(qi[..., D_HA:], k2)
        return jnp.einsum('bhqk,bkhd->bqhd', a.astype(v.dtype), v)

    o = lax.map(one_block, (qb, pb))
    return jnp.swapaxes(o, 0, 1).reshape(B, T, H, DA)


def gated_delta_chunked(q, k, v, g, beta, S0):
    B, T, H, DK = q.shape
    DV = v.shape[-1]
    C = math.gcd(GDN_CHUNK, T)
    N = T // C

    def blk(a):
        return jnp.moveaxis(a.reshape((B, N, C, H) + a.shape[3:]), 3, 2)

    q, k, v, g, beta = blk(q), blk(k), blk(v), blk(g), blk(beta)
    g = jnp.cumsum(g, axis=-1)
    incl = jnp.tril(jnp.ones((C, C), dtype=bool))
    strict = jnp.tril(jnp.ones((C, C), dtype=bool), -1)
    decay = jnp.exp(jnp.where(incl, g[..., :, None] - g[..., None, :], NEG))
    kb = k * beta[..., None]
    M = jnp.where(strict, jnp.einsum('bnhid,bnhjd->bnhij', kb, k) * decay, 0.0)
    eye = jnp.eye(C, dtype=jnp.float32)
    Tm = lax.linalg.triangular_solve(eye + M, jnp.broadcast_to(eye, M.shape), left_side=True, lower=True)
    u = Tm @ (v * beta[..., None])
    w = Tm @ (kb * jnp.exp(g)[..., None])
    attn = jnp.einsum('bnhid,bnhjd->bnhij', q, k) * decay
    qg = q * jnp.exp(g)[..., None]
    kg = k * jnp.exp(g[..., -1:] - g)[..., None]
    gl = jnp.exp(g[..., -1])

    def step(S, inp):
        qg_c, kg_c, u_c, w_c, attn_c, gl_c = inp
        v_new = u_c - w_c @ S
        o = qg_c @ S + attn_c @ v_new
        S = S * gl_c[..., None, None] + jnp.swapaxes(kg_c, -1, -2) @ v_new
        return S, o

    xs = tuple(jnp.moveaxis(a, 1, 0) for a in (qg, kg, u, w, attn, gl))
    S, o = lax.scan(step, S0, xs)
    o = jnp.moveaxis(jnp.moveaxis(o, 0, 1), 2, 3).reshape(B, T, H, DV)
    return o, S


def even_layer(x, past_kv, S0, conv0, li, ei, lambda_init, P):
    B, T, _ = x.shape
    h = rmsnorm(x, P['norm_w'][li])
    p = h @ P['w_in_even'][ei]
    qa, ka, va, za, qkv_b, zb, a_b, b_b = split_cols(p, EVEN_SIZES)
    qa = rmsnorm(qa.reshape(B, T, H_A, 2, D_HA), P['qn_w'][ei]).reshape(B, T, H_A, DA)
    ka = rmsnorm(ka.reshape(B, T, H_A, 2, D_HA), P['kn_w'][ei]).reshape(B, T, H_A, DA)
    va = va.reshape(B, T, H_A, DA)
    lam = (jnp.exp(jnp.sum(P['lam_q1'][ei] * P['lam_k1'][ei]).astype(jnp.float32))
           - jnp.exp(jnp.sum(P['lam_q2'][ei] * P['lam_k2'][ei]).astype(jnp.float32)) + lambda_init)
    if past_kv is None:
        k_all, v_all = ka, va
    else:
        k_all = jnp.concatenate([past_kv[0].astype(ka.dtype), ka], axis=1)
        v_all = jnp.concatenate([past_kv[1].astype(va.dtype), va], axis=1)
    L = k_all.shape[1]
    k_pos = jnp.arange(L, dtype=jnp.int32)
    q_pos = (L - T) + jnp.arange(T, dtype=jnp.int32)
    oa = diff_attention(qa, k_all, v_all, q_pos, k_pos, lam, P['rel_table'])
    oa = rmsnorm(oa, P['subln_w'][ei]) * (1.0 - lambda_init)
    oa = oa.reshape(B, T, W_A) * jax.nn.silu(za)
    cb, conv_new = causal_conv(qkv_b, conv0, P['gdn_conv_w'][ei])
    cb = jax.nn.silu(cb)
    qb, kb, vb = jnp.split(cb, [H_B * DK_B, 2 * H_B * DK_B], axis=-1)
    qb = l2norm(qb.reshape(B, T, H_B, DK_B)) * (DK_B ** -0.5)
    kb = l2norm(kb.reshape(B, T, H_B, DK_B))
    vb = vb.reshape(B, T, H_B, DV_B).astype(jnp.float32)
    g = -jnp.exp(P['gdn_a_log'][ei].astype(jnp.float32)) * jax.nn.softplus(
        a_b.astype(jnp.float32) + P['gdn_dt_bias'][ei].astype(jnp.float32))
    beta = jax.nn.sigmoid(b_b.astype(jnp.float32))
    ob, S_new = gated_delta_chunked(qb, kb, vb, g, beta, S0.astype(jnp.float32))
    ob = rmsnorm(ob, P['gdn_norm_w'][ei]).astype(x.dtype).reshape(B, T, W_B) * jax.nn.silu(zb)
    y = jnp.concatenate([oa, ob], axis=-1) @ P['w_out_even'][ei]
    return x + y, ka, va, S_new.astype(S0.dtype), conv_new


def odd_layer(x, buf0, li, oi, P):
    h = rmsnorm(x, P['norm_w'][li])
    p = h @ P['w_in_odd'][oi]
    bg, cg, hh, z = jnp.split(p, 4, axis=-1)
    cv, buf = causal_conv(cg * hh, buf0, P['sc_conv_w'][oi])
    y = (bg * cv * jax.nn.silu(z)) @ P['w_out_odd'][oi]
    return x + y, buf


def trunk(x, paged, S0s, gconv0s, sconv0s, P):
    ks, vs, Ss, gcs, scs = [], [], [], [], []
    ei, oi = 0, 0
    for li in range(DEPTH):
        if li % 2 == 0:
            if paged is None:
                past = None
            else:
                cache_k, cache_v, page_table = paged
                nb = page_table.shape[0]
                past = (cache_k[ei][page_table].reshape(nb, -1, H_A, DA),
                        cache_v[ei][page_table].reshape(nb, -1, H_A, DA))
            lambda_init = 0.8 - 0.6 * math.exp(-0.3 * li)
            x, k_new, v_new, S_new, gc_new = even_layer(x, past, S0s[ei], gconv0s[ei], li, ei, lambda_init, P)
            ks.append(k_new)
            vs.append(v_new)
            Ss.append(S_new)
            gcs.append(gc_new)
            ei += 1
        else:
            x, sc_new = odd_layer(x, sconv0s[oi], li, oi, P)
            scs.append(sc_new)
            oi += 1
    return x, jnp.stack(ks), jnp.stack(vs), jnp.stack(Ss), jnp.stack(gcs), jnp.stack(scs)


def setup_inputs(seed: int = 0) -> dict:
    key = jax.random.key(seed)
    kk = jax.random.split(key, 32)
    f = jnp.float32
    n_pages = PAST_LEN // PAGE_SIZE
    n_used = DEC_BATCH * n_pages
    n_pool = n_used + n_used // 4

    def nrm(k, shape, scale):
        return jax.random.normal(k, shape, f) * scale

    page_table = jax.random.permutation(kk[0], n_pool)[:n_used].reshape(DEC_BATCH, n_pages).astype(jnp.int32)
    a_log = jnp.log(jax.random.uniform(kk[1], (N_EVEN, H_B), f, 1.0, 16.0))
    dt = jnp.exp(jax.random.uniform(kk[2], (N_EVEN, H_B), f, math.log(1e-3), math.log(1e-1)))
    dt_bias = dt + jnp.log(-jnp.expm1(-dt))
    out_scale = DEPTH ** -0.5
    return {
        'x_prompt': nrm(kk[3], (BATCH, SEQ, D_MODEL), 1.0),
        'x_sample': nrm(kk[4], (DEC_BATCH, DEC_SEQ, D_MODEL), 1.0),
        'cache_k': nrm(kk[5], (N_EVEN, n_pool, PAGE_SIZE, H_A, DA), 1.0),
        'cache_v': nrm(kk[6], (N_EVEN, n_pool, PAGE_SIZE, H_A, DA), 1.0),
        'page_table': page_table,
        'state_gdn': nrm(kk[7], (N_EVEN, DEC_BATCH, H_B, DK_B, DV_B), 0.1),
        'state_gdn_conv': nrm(kk[8], (N_EVEN, DEC_BATCH, GDN_CONV - 1, QKV_B), 1.0),
        'state_shortconv': nrm(kk[9], (N_ODD, DEC_BATCH, SC_WIDTH - 1, D_C), 1.0),
        'norm_w': 1.0 + nrm(kk[10], (DEPTH, D_MODEL), 0.1),
        'rel_table': nrm(kk[11], (NUM_BUCKETS, H_A), 0.5),
        'w_in_even': nrm(kk[12], (N_EVEN, D_MODEL, P_EVEN), D_MODEL ** -0.5),
        'w_out_even': nrm(kk[13], (N_EVEN, W_A + W_B, D_MODEL), (W_A + W_B) ** -0.5 * out_scale),
        'qn_w': 1.0 + nrm(kk[14], (N_EVEN, D_HA), 0.1),
        'kn_w': 1.0 + nrm(kk[15], (N_EVEN, D_HA), 0.1),
        'lam_q1': nrm(kk[16], (N_EVEN, D_HA), 0.1),
        'lam_k1': nrm(kk[17], (N_EVEN, D_HA), 0.1),
        'lam_q2': nrm(kk[18], (N_EVEN, D_HA), 0.1),
        'lam_k2': nrm(kk[19], (N_EVEN, D_HA), 0.1),
        'subln_w': 1.0 + nrm(kk[20], (N_EVEN, DA), 0.1),
        'gdn_conv_w': nrm(kk[21], (N_EVEN, GDN_CONV, QKV_B), GDN_CONV ** -0.5),
        'gdn_a_log': a_log,
        'gdn_dt_bias': dt_bias,
        'gdn_norm_w': 1.0 + nrm(kk[22], (N_EVEN, DV_B), 0.1),
        'w_in_odd': nrm(kk[23], (N_ODD, D_MODEL, P_ODD), D_MODEL ** -0.5),
        'sc_conv_w': nrm(kk[24], (N_ODD, SC_WIDTH, D_C), SC_WIDTH ** -0.5),
        'w_out_odd': nrm(kk[25], (N_ODD, D_C, D_MODEL), D_C ** -0.5 * out_scale),
    }


def reference(x_prompt, x_sample, cache_k, cache_v, page_table, state_gdn, state_gdn_conv, state_shortconv,
              norm_w, rel_table, w_in_even, w_out_even, qn_w, kn_w, lam_q1, lam_k1, lam_q2, lam_k2, subln_w,
              gdn_conv_w, gdn_a_log, gdn_dt_bias, gdn_norm_w, w_in_odd, sc_conv_w, w_out_odd):
    P = {'norm_w': norm_w, 'rel_table': rel_table, 'w_in_even': w_in_even, 'w_out_even': w_out_even,
         'qn_w': qn_w, 'kn_w': kn_w, 'lam_q1': lam_q1, 'lam_k1': lam_k1, 'lam_q2': lam_q2, 'lam_k2': lam_k2,
         'subln_w': subln_w, 'gdn_conv_w': gdn_conv_w, 'gdn_a_log': gdn_a_log, 'gdn_dt_bias': gdn_dt_bias,
         'gdn_norm_w': gdn_norm_w, 'w_in_odd': w_in_odd, 'sc_conv_w': sc_conv_w, 'w_out_odd': w_out_odd}
    bp = x_prompt.shape[0]
    zS = jnp.zeros((N_EVEN, bp, H_B, DK_B, DV_B), state_gdn.dtype)
    zgc = jnp.zeros((N_EVEN, bp, GDN_CONV - 1, QKV_B), x_prompt.dtype)
    zsc = jnp.zeros((N_ODD, bp, SC_WIDTH - 1, D_C), x_prompt.dtype)
    y_prompt, k_p, v_p, S_p, gc_p, sc_p = trunk(x_prompt, None, zS, zgc, zsc, P)
    y_sample, k_s, v_s, S_s, gc_s, sc_s = trunk(x_sample, (cache_k, cache_v, page_table), state_gdn,
                                                 state_gdn_conv, state_shortconv, P)
    return (y_prompt, y_sample, k_p, v_p, S_p, gc_p, sc_p, k_s, v_s, S_s, gc_s, sc_s)
```

```python
import functools
import math

import jax
import jax.numpy as jnp
from jax import lax
from jax.experimental import pallas as pl
from jax.experimental.pallas import tpu as pltpu

F32, BF16 = jnp.float32, jnp.bfloat16

DEPTH = 4
H_A, D_HA = 4, 64
DA = 2 * D_HA
W_A = H_A * DA
H_B, DK_B, DV_B = 4, 128, 128
W_B = H_B * DV_B
QKV_B = 2 * H_B * DK_B + H_B * DV_B
GDN_CONV, GDN_CHUNK, SC_WIDTH = 4, 64, 3
NUM_BUCKETS, MAX_EXACT, MAX_DISTANCE = 32, 16, 128
EPS, NEG = 1e-6, -1e30
LANE = 128
P_MAIN = 4 * W_A + QKV_B + W_B
P_EVEN_PAD = P_MAIN + LANE
AB_BLOCK = P_MAIN // LANE
VMEM_LIMIT = 48 * 1024 * 1024


def _cparams(*sem):
    return pltpu.CompilerParams(dimension_semantics=sem, vmem_limit_bytes=VMEM_LIMIT)


def _silu(z):
    return z / (1.0 + jnp.exp(-z))


def _sigmoid(z):
    return 1.0 / (1.0 + jnp.exp(-z))


def _dot(a, b):
    return jnp.dot(a.astype(BF16), b.astype(BF16), preferred_element_type=F32)


def _dot_nt(a, b):
    return lax.dot_general(a.astype(BF16), b.astype(BF16), (((1,), (1,)), ((), ())), preferred_element_type=F32)


def _split(a):
    hi = a.astype(BF16)
    return hi, (a - hi.astype(F32)).astype(BF16)


def _dot3(a, b):
    ah, al = _split(a)
    bh, bl = _split(b)
    d = functools.partial(jnp.dot, preferred_element_type=F32)
    return d(ah, bh) + d(ah, bl) + d(al, bh)


def _norm_proj_kernel(x_ref, nw_ref, w_ref, o_ref, h_ref):
    @pl.when(pl.program_id(1) == 0)
    def _():
        x = x_ref[...]
        ms = jnp.mean(x * x, axis=-1, keepdims=True)
        h_ref[...] = (x * lax.rsqrt(ms + EPS) * nw_ref[...]).astype(BF16)

    o_ref[...] = jnp.dot(h_ref[...], w_ref[...], preferred_element_type=F32)


def _norm_proj(x, nw, w, tm, tn):
    m, d = x.shape
    n = w.shape[1]
    return pl.pallas_call(
        _norm_proj_kernel,
        out_shape=jax.ShapeDtypeStruct((m, n), F32),
        grid=(m // tm, n // tn),
        in_specs=[pl.BlockSpec((tm, d), lambda i, j: (i, 0)),
                  pl.BlockSpec((1, d), lambda i, j: (0, 0)),
                  pl.BlockSpec((d, tn), lambda i, j: (0, j))],
        out_specs=pl.BlockSpec((tm, tn), lambda i, j: (i, j)),
        scratch_shapes=[pltpu.VMEM((tm, d), BF16)],
        compiler_params=_cparams("arbitrary", "arbitrary"),
        name="norm_proj",
    )(x, nw, w)


def _qkv_prep_kernel(q_ref, k_ref, v_ref, qw_ref, kw_ref, bd_ref, qo_ref, kf_ref, kb_ref, vf_ref, vb_ref):
    bd = bd_ref[...]

    def group_norm(x, w):
        hi, lo = _split(x * x)
        ms = jnp.dot(hi, bd, preferred_element_type=F32) + jnp.dot(lo, bd, preferred_element_type=F32)
        return x * lax.rsqrt(ms + EPS) * w

    qn = group_norm(q_ref[...], qw_ref[...])
    kn = group_norm(k_ref[...], kw_ref[...])
    qo_ref[...] = (qn * (D_HA ** -0.5)).astype(BF16)
    kf_ref[...] = kn
    kb_ref[...] = kn.astype(BF16)
    v = v_ref[...]
    vf_ref[...] = v
    vb_ref[...] = v.astype(BF16)


def _qkv_prep(p, qw, kw, bd, tm):
    m = p.shape[0]
    col = lambda c: pl.BlockSpec((tm, W_A), lambda i, c=c: (i, c))
    full = lambda a: pl.BlockSpec(a.shape, lambda i: (0,) * a.ndim)
    row = pl.BlockSpec((tm, W_A), lambda i: (i, 0))
    shp = lambda dt: jax.ShapeDtypeStruct((m, W_A), dt)
    return pl.pallas_call(
        _qkv_prep_kernel,
        out_shape=(shp(BF16), shp(F32), shp(BF16), shp(F32), shp(BF16)),
        grid=(m // tm,),
        in_specs=[col(0), col(1), col(2), full(qw), full(kw), full(bd)],
        out_specs=(row, row, row, row, row),
        compiler_params=_cparams("arbitrary"),
        name="qkv_prep",
    )(p, p, p, qw, kw, bd)


def _t5_bias(n, tab_ref, h):
    nf = jnp.maximum(n, 1).astype(F32)
    large = MAX_EXACT + (jnp.log(nf / MAX_EXACT) / math.log(MAX_DISTANCE / MAX_EXACT)
                         * (NUM_BUCKETS - MAX_EXACT)).astype(jnp.int32)
    large = jnp.minimum(large, NUM_BUCKETS - 1)
    bkt = jnp.where(n < MAX_EXACT, n, large)
    out = jnp.zeros(n.shape, F32)
    for b in range(NUM_BUCKETS):
        out = jnp.where(bkt == b, tab_ref[b, h], out)
    return out - tab_ref[NUM_BUCKETS - 1, h]


def _bias_tiles_kernel(tab_ref, o_ref, *, tq):
    h = pl.program_id(0)
    i = lax.broadcasted_iota(jnp.int32, (tq, tq), 0)
    j = lax.broadcasted_iota(jnp.int32, (tq, tq), 1)
    n0 = i - j
    o_ref[0, 0] = jnp.where(n0 >= 0, _t5_bias(jnp.maximum(n0, 0), tab_ref, h), NEG)
    o_ref[0, 1] = _t5_bias(n0 + tq, tab_ref, h)


def _bias_tiles(rel_table, tq):
    return pl.pallas_call(
        functools.partial(_bias_tiles_kernel, tq=tq),
        out_shape=jax.ShapeDtypeStruct((H_A, 2, tq, tq), F32),
        grid=(H_A,),
        in_specs=[pl.BlockSpec(memory_space=pltpu.SMEM)],
        out_specs=pl.BlockSpec((1, 2, tq, tq), lambda h: (h, 0, 0, 0)),
        compiler_params=_cparams("arbitrary"),
        name="bias_tiles",
    )(rel_table)


def _bias_decode_kernel(tab_ref, o_ref, *, page):
    h = pl.program_id(0)
    lane = lax.broadcasted_iota(jnp.int32, (8, 2 * page), 1)
    n = jnp.where(lane < page, page - lane, 0)
    o_ref[0] = _t5_bias(n, tab_ref, h)


def _bias_decode(rel_table, page):
    out = pl.pallas_call(
        functools.partial(_bias_decode_kernel, page=page),
        out_shape=jax.ShapeDtypeStruct((H_A, 8, 2 * page), F32),
        grid=(H_A,),
        in_specs=[pl.BlockSpec(memory_space=pltpu.SMEM)],
        out_specs=pl.BlockSpec((1, 8, 2 * page), lambda h: (h, 0, 0)),
        compiler_params=_cparams("arbitrary"),
        name="bias_decode",
    )(rel_table)
    return jnp.repeat(out[:, 0, :], 2, axis=0)


def _attn_finish(o, lam_unused, sw, z, out_scale):
    ms = jnp.mean(o * o, axis=-1, keepdims=True)
    return o * lax.rsqrt(ms + EPS) * sw * out_scale * _silu(z)


def _attn_prompt_kernel(lam_ref, q_ref, k_ref, v_ref, bias_ref, za_ref, sw_ref, o_ref,
                        m1, l1, a1, m2, l2, a2, *, tq, out_scale):
    qi = pl.program_id(2)
    q = q_ref[...].astype(F32)
    lane = lax.broadcasted_iota(jnp.int32, q.shape, 1)
    streams = ((jnp.where(lane < D_HA, q, 0.0).astype(BF16), m1, l1, a1),
               (jnp.where(lane >= D_HA, q, 0.0).astype(BF16), m2, l2, a2))
    for _, m, l, a in streams:
        m[...] = jnp.full(m.shape, NEG, F32)
        l[...] = jnp.zeros(l.shape, F32)
        a[...] = jnp.zeros(a.shape, F32)

    def step(j, bias):
        rows = pl.ds(pl.multiple_of(j * tq, tq), tq)
        kb = k_ref[rows, :]
        vb = v_ref[rows, :]
        for qq, m, l, a in streams:
            s = lax.dot_general(qq, kb, (((1,), (1,)), ((), ())), preferred_element_type=F32)
            if bias is not None:
                s = s + bias
            m_prev = m[...]
            m_new = jnp.maximum(m_prev, jnp.max(s, axis=1, keepdims=True))
            p = jnp.exp(s - m_new)
            alpha = jnp.exp(m_prev - m_new)
            l[...] = alpha * l[...] + jnp.sum(p, axis=1, keepdims=True)
            a[...] = alpha * a[...] + jnp.dot(p.astype(BF16), vb, preferred_element_type=F32)
            m[...] = m_new

    def far(j, c):
        step(j, None)
        return c

    lax.fori_loop(0, jnp.maximum(qi - 1, 0), far, 0)

    @pl.when(qi >= 1)
    def _():
        step(qi - 1, bias_ref[0, 1])

    step(qi, bias_ref[0, 0])
    o = a1[...] / l1[...] - lam_ref[0] * (a2[...] / l2[...])
    o_ref[...] = _attn_finish(o, None, sw_ref[...], za_ref[...], out_scale).astype(BF16)


def _attn_prompt(lam, q, k, v, bias, p, sw, nb, t, tq, out_scale):
    m = q.shape[0]
    nq = t // tq
    za0 = (3 * W_A) // DA
    return pl.pallas_call(
        functools.partial(_attn_prompt_kernel, tq=tq, out_scale=out_scale),
        out_shape=jax.ShapeDtypeStruct((m, W_A), BF16),
        grid=(nb, H_A, nq),
        in_specs=[pl.BlockSpec(memory_space=pltpu.SMEM),
                  pl.BlockSpec((tq, DA), lambda b, h, i: (b * nq + i, h)),
                  pl.BlockSpec((t, DA), lambda b, h, i: (b, h)),
                  pl.BlockSpec((t, DA), lambda b, h, i: (b, h)),
                  pl.BlockSpec((1, 2, tq, tq), lambda b, h, i: (h, 0, 0, 0)),
                  pl.BlockSpec((tq, DA), lambda b, h, i: (b * nq + i, za0 + h)),
                  pl.BlockSpec((1, DA), lambda b, h, i: (0, 0))],
        out_specs=pl.BlockSpec((tq, DA), lambda b, h, i: (b * nq + i, h)),
        scratch_shapes=[pltpu.VMEM((tq, 1), F32), pltpu.VMEM((tq, 1), F32), pltpu.VMEM((tq, DA), F32),
                        pltpu.VMEM((tq, 1), F32), pltpu.VMEM((tq, 1), F32), pltpu.VMEM((tq, DA), F32)],
        compiler_params=_cparams("arbitrary", "arbitrary", "arbitrary"),
        name="attn_prompt",
    )(lam, q, k, v, bias, p, sw)


def _attn_decode_kernel(pt_ref, lam_ref, q_ref, kn_ref, vn_ref, ck_ref, cv_ref, bias_ref, za_ref, sw_ref, o_ref,
                        m_ref, l_ref, a_ref, *, page, out_scale):
    pi = pl.program_id(1)
    last = pl.num_programs(1) - 1
    rows = 2 * H_A
    q = q_ref[...].astype(F32)
    row = lax.broadcasted_iota(jnp.int32, (rows, W_A), 0)
    lane = lax.broadcasted_iota(jnp.int32, (rows, W_A), 1)
    qmat32 = jnp.where((lane >> 6) == row, jnp.broadcast_to(q, (rows, W_A)), 0.0)
    qmat = qmat32.astype(BF16)

    @pl.when(pi == 0)
    def _():
        m_ref[...] = jnp.full(m_ref.shape, NEG, F32)
        l_ref[...] = jnp.zeros(l_ref.shape, F32)
        a_ref[...] = jnp.zeros(a_ref.shape, F32)

    kp = ck_ref[...].astype(BF16)
    vp = cv_ref[...].astype(BF16)
    s = lax.dot_general(qmat, kp, (((1,), (1,)), ((), ())), preferred_element_type=F32)
    s = jnp.where(pi == last, s + bias_ref[:, :page], s)
    m_prev = m_ref[...]
    m_new = jnp.maximum(m_prev, jnp.max(s, axis=1, keepdims=True))
    p = jnp.exp(s - m_new)
    alpha = jnp.exp(m_prev - m_new)
    l_ref[...] = alpha * l_ref[...] + jnp.sum(p, axis=1, keepdims=True)
    a_ref[...] = alpha * a_ref[...] + jnp.dot(p.astype(BF16), vp, preferred_element_type=F32)
    m_ref[...] = m_new

    @pl.when(pi == last)
    def _():
        kn = kn_ref[...].astype(F32)
        vn = vn_ref[...].astype(F32)
        s_new = jnp.sum(qmat32 * kn, axis=1, keepdims=True) + bias_ref[:, page:page + 1]
        m_prev = m_ref[...]
        m_new = jnp.maximum(m_prev, s_new)
        p_new = jnp.exp(s_new - m_new)
        alpha = jnp.exp(m_prev - m_new)
        l = alpha * l_ref[...] + p_new
        o = (alpha * a_ref[...] + p_new * vn) / l
        sw = sw_ref[...]
        z = za_ref[...]
        outs = []
        for h in range(H_A):
            cols = slice(h * DA, (h + 1) * DA)
            oh = o[2 * h:2 * h + 1, cols] - lam_ref[0] * o[2 * h + 1:2 * h + 2, cols]
            outs.append(_attn_finish(oh, None, sw, z[:, cols], out_scale))
        o_ref[...] = jnp.concatenate(outs, axis=1).astype(BF16)


def _attn_decode(page_table, lam, q, kn, vn, cache_k, cache_v, ei, bias, p, sw, out_scale):
    nb, npg = page_table.shape
    page = cache_k.shape[2]
    ck = cache_k.reshape(cache_k.shape[0], cache_k.shape[1], page, W_A)
    cv = cache_v.reshape(cache_v.shape[0], cache_v.shape[1], page, W_A)
    r3 = lambda a: a.reshape(nb, 1, a.shape[-1])
    vec = lambda c: pl.BlockSpec((None, 1, W_A), lambda b, i, pt, c=c: (b, 0, c))
    cache = pl.BlockSpec((None, None, page, W_A), lambda b, i, pt: (ei, pt[b * npg + i], 0, 0))
    out = pl.pallas_call(
        functools.partial(_attn_decode_kernel, page=page, out_scale=out_scale),
        out_shape=jax.ShapeDtypeStruct((nb, 1, W_A), BF16),
        grid_spec=pltpu.PrefetchScalarGridSpec(
            num_scalar_prefetch=1,
            grid=(nb, npg),
            in_specs=[pl.BlockSpec(memory_space=pltpu.SMEM),
                      vec(0), vec(0), vec(0), cache, cache,
                      pl.BlockSpec(bias.shape, lambda b, i, pt: (0, 0)),
                      vec(3),
                      pl.BlockSpec((1, DA), lambda b, i, pt: (0, 0))],
            out_specs=pl.BlockSpec((None, 1, W_A), lambda b, i, pt: (b, 0, 0)),
            scratch_shapes=[pltpu.VMEM((2 * H_A, 1), F32), pltpu.VMEM((2 * H_A, 1), F32),
                            pltpu.VMEM((2 * H_A, W_A), F32)]),
        compiler_params=_cparams("arbitrary", "arbitrary"),
        name="attn_decode",
    )(page_table.reshape(-1), lam, r3(q), r3(kn), r3(vn), ck, cv, bias, r3(p), sw)
    return out.reshape(nb, W_A)


def _shifted_conv(x, prev, w):
    taps = w.shape[0]
    r8 = lax.broadcasted_iota(jnp.int32, (8, x.shape[1]), 0)
    y = x * w[taps - 1:taps, :]
    xt = x[:8]
    yt = xt * w[taps - 1:taps, :]
    for s in range(1, taps):
        ws = w[taps - 1 - s:taps - s, :]
        y = y + pltpu.roll(x, s, 0) * ws
        yt = yt + jnp.where(r8 < s, pltpu.roll(prev, s, 0), pltpu.roll(xt, s, 0)) * ws
    return jnp.concatenate([yt, y[8:]], axis=0)


def _gates(ab, alog, dtb):
    x = ab + dtb
    sp = jnp.maximum(x, 0.0) + jnp.log(1.0 + jnp.exp(-jnp.abs(x)))
    return -jnp.exp(alog) * sp, _sigmoid(ab)


def _l2norm(x):
    return x * lax.rsqrt(jnp.sum(x * x, axis=-1, keepdims=True) + EPS)


def _rms(x, w):
    return x * lax.rsqrt(jnp.mean(x * x, axis=-1, keepdims=True) + EPS) * w


def _inv_unit_lower(mm, masks):
    c = mm.shape[0]
    eye = (lax.broadcasted_iota(jnp.int32, (c, c), 0) == lax.broadcasted_iota(jnp.int32, (c, c), 1)).astype(F32)
    p = eye - mm * masks[0]
    for mk in masks[1:]:
        p = p - _dot3(_dot3(p, mm * mk), p)
    return p


def _level_masks(c):
    i = lax.broadcasted_iota(jnp.int32, (c, c), 0)
    j = lax.broadcasted_iota(jnp.int32, (c, c), 1)
    masks = []
    s = 1
    while s < c:
        sh = s.bit_length() - 1
        masks.append((((i >> (sh + 1)) == (j >> (sh + 1))) & ((i >> sh) != (j >> sh)) & (i > j)).astype(F32))
        s *= 2
    return masks


def _gdn_prompt_kernel(pq_ref, pk_ref, pv_ref, zb_ref, ab_ref, cw_ref, alog_ref, dtb_ref, gnw_ref,
                       ob_ref, s_ref, carry_ref, cb_ref, g_ref, beta_ref, *, rows, chunk):
    t = pl.program_id(1)

    @pl.when(t == 0)
    def _():
        carry_ref[...] = jnp.zeros(carry_ref.shape, F32)
        s_ref[...] = jnp.zeros(s_ref.shape, F32)

    cw = cw_ref[...]
    for seg, ref in enumerate((pq_ref, pk_ref, pv_ref)):
        cols = slice(seg * W_B, (seg + 1) * W_B)
        x = ref[...]
        y = _shifted_conv(x, carry_ref[:, cols], cw[:, cols])
        carry_ref[:, cols] = x[rows - 8:rows]
        cb_ref[:, cols] = _silu(y)
    g, beta = _gates(ab_ref[...], alog_ref[...], dtb_ref[...])
    g_ref[...] = g
    beta_ref[...] = beta

    c = chunk
    ii = lax.broadcasted_iota(jnp.int32, (c, c), 0)
    jj = lax.broadcasted_iota(jnp.int32, (c, c), 1)
    incl = ii >= jj
    ltri = incl.astype(BF16)
    masks = _level_masks(c)
    gnw = gnw_ref[...]

    def body(ci, carry):
        rs = pl.ds(pl.multiple_of(ci * c, c), c)
        gch = g_ref[rs, :]
        g1 = gch.astype(BF16)
        r1 = gch - g1.astype(F32)
        g2 = r1.astype(BF16)
        g3 = (r1 - g2.astype(F32)).astype(BF16)
        d = functools.partial(jnp.dot, preferred_element_type=F32)
        gc = d(ltri, g1) + d(ltri, g2) + d(ltri, g3)
        gct = jnp.concatenate([gc, gc], axis=0).T
        ge = jnp.exp(gc)
        kdec = jnp.exp(gc[c - 1:c, :] - gc)
        bch = beta_ref[rs, :]
        for h in range(H_B):
            hc = slice(h * DK_B, (h + 1) * DK_B)
            qh = _l2norm(cb_ref[rs, hc]) * (DK_B ** -0.5)
            kh = _l2norm(cb_ref[rs, slice(W_B + h * DK_B, W_B + (h + 1) * DK_B)])
            vh = cb_ref[rs, slice(2 * W_B + h * DV_B, 2 * W_B + (h + 1) * DV_B)]
            gcol = gc[:, h:h + 1]
            grow = gct[h:h + 1, :c]
            bcol = bch[:, H_B + h:H_B + h + 1]
            gecol = ge[:, h:h + 1]
            decay = jnp.exp(jnp.where(incl, gcol - grow, NEG))
            kbeta = kh * bcol
            mm = jnp.where(ii > jj, _dot_nt(kbeta, kh) * decay, 0.0)
            tm = _inv_unit_lower(mm, masks)
            uw = _dot(tm, jnp.concatenate([vh * bcol, kbeta * gecol], axis=1))
            u, w = uw[:, :DV_B], uw[:, DV_B:]
            attn = _dot_nt(qh, kh) * decay
            s_old = s_ref[h]
            v_new = u - _dot(w, s_old)
            o = _dot(qh * gecol, s_old) + _dot(attn, v_new)
            kg = kh * kdec[:, h:h + 1]
            s_ref[h] = s_old * ge[c - 1:c, h:h + 1] + lax.dot_general(
                kg.astype(BF16), v_new.astype(BF16), (((0,), (0,)), ((), ())), preferred_element_type=F32)
            ob_ref[rs, hc] = (_rms(o, gnw) * _silu(zb_ref[rs, hc])).astype(BF16)
        return carry

    lax.fori_loop(0, rows // c, body, 0)


def _gdn_prompt(p, cw, alog, dtb, gnw, nb, t, rows):
    m = p.shape[0]
    nt = t // rows
    col = lambda cidx: pl.BlockSpec((rows, W_B), lambda b, i, cidx=cidx: (b * nt + i, cidx))
    full = lambda a: pl.BlockSpec(a.shape, lambda b, i: (0,) * a.ndim)
    c0 = (4 * W_A) // W_B
    return pl.pallas_call(
        functools.partial(_gdn_prompt_kernel, rows=rows, chunk=math.gcd(GDN_CHUNK, t)),
        out_shape=(jax.ShapeDtypeStruct((m, W_B), BF16), jax.ShapeDtypeStruct((nb, H_B, DK_B, DV_B), F32)),
        grid=(nb, nt),
        in_specs=[col(c0), col(c0 + 1), col(c0 + 2), col(c0 + 3),
                  pl.BlockSpec((rows, LANE), lambda b, i: (b * nt + i, AB_BLOCK)),
                  full(cw), full(alog), full(dtb), full(gnw)],
        out_specs=(pl.BlockSpec((rows, W_B), lambda b, i: (b * nt + i, 0)),
                   pl.BlockSpec((None, H_B, DK_B, DV_B), lambda b, i: (b, 0, 0, 0))),
        scratch_shapes=[pltpu.VMEM((8, QKV_B), F32), pltpu.VMEM((rows, QKV_B), F32),
                        pltpu.VMEM((rows, LANE), F32), pltpu.VMEM((rows, LANE), F32)],
        compiler_params=_cparams("arbitrary", "arbitrary"),
        name="gdn_prompt",
    )(p, p, p, p, p, cw, alog, dtb, gnw)


def _gdn_decode_prep_kernel(pq_ref, pk_ref, pv_ref, ab_ref, c0_ref, cw_ref, alog_ref, dtb_ref,
                            q_ref, k_ref, v_ref, eg_ref, beta_ref, qk_ref, cn_ref):
    cw = cw_ref[...]
    taps = cw.shape[0]
    outs = (q_ref, k_ref, v_ref)
    for seg, ref in enumerate((pq_ref, pk_ref, pv_ref)):
        cols = slice(seg * W_B, (seg + 1) * W_B)
        x = ref[...]
        y = x * cw[taps - 1:taps, cols]
        for j in range(taps - 1):
            y = y + c0_ref[j, :, cols] * cw[j:j + 1, cols]
            if j >= 1:
                cn_ref[j - 1, :, cols] = c0_ref[j, :, cols]
        cn_ref[taps - 2, :, cols] = x
        outs[seg][...] = _silu(y)
    lane = lax.broadcasted_iota(jnp.int32, eg_ref.shape, 1)
    qk = jnp.zeros(eg_ref.shape, F32)
    for h in range(H_B):
        hc = slice(h * DK_B, (h + 1) * DK_B)
        qh = _l2norm(q_ref[:, hc]) * (DK_B ** -0.5)
        kh = _l2norm(k_ref[:, hc])
        q_ref[:, hc] = qh
        k_ref[:, hc] = kh
        qk = jnp.where(lane == h, jnp.sum(qh * kh, axis=-1, keepdims=True), qk)
    g, beta = _gates(ab_ref[...], alog_ref[...], dtb_ref[...])
    eg_ref[...] = jnp.exp(g)
    beta_ref[...] = beta
    qk_ref[...] = qk


def _gdn_decode_prep(p, conv0_t, cw, alog, dtb):
    nb = p.shape[0]
    c0 = (4 * W_A) // W_B
    col = lambda cidx: pl.BlockSpec((nb, W_B), lambda i, cidx=cidx: (0, cidx))
    full = lambda a: pl.BlockSpec(a.shape, lambda i: (0,) * a.ndim)
    wide = jax.ShapeDtypeStruct((nb, W_B), F32)
    narrow = jax.ShapeDtypeStruct((nb, LANE), F32)
    ospec = lambda s: pl.BlockSpec(s.shape, lambda i: (0,) * len(s.shape))
    outs = (wide, wide, wide, narrow, narrow, narrow, jax.ShapeDtypeStruct(conv0_t.shape, F32))
    return pl.pallas_call(
        _gdn_decode_prep_kernel,
        out_shape=outs,
        grid=(1,),
        in_specs=[col(c0), col(c0 + 1), col(c0 + 2),
                  pl.BlockSpec((nb, LANE), lambda i: (0, AB_BLOCK)),
                  full(conv0_t), full(cw), full(alog), full(dtb)],
        out_specs=tuple(ospec(s) for s in outs),
        compiler_params=_cparams("arbitrary"),
        name="gdn_decode_prep",
    )(p, p, p, p, conv0_t, cw, alog, dtb)


def _gdn_decode_kernel(s_ref, qt_ref, kt_ref, v_ref, eg_ref, beta_ref, qk_ref, zb_ref, gnw_ref,
                       so_ref, ob_ref, o_scr, *, bb):
    v = v_ref[...]
    eg = eg_ref[...]
    beta = beta_ref[...]
    qk = qk_ref[...]
    for h in range(H_B):
        hc = slice(h * DV_B, (h + 1) * DV_B)
        qt = qt_ref[h]
        kt = kt_ref[h]
        for i in range(bb):
            s_old = s_ref[i, h]
            kc = kt[:, i:i + 1]
            qc = qt[:, i:i + 1]
            egs = eg[i:i + 1, h:h + 1]
            ks = jnp.sum(s_old * kc, axis=0, keepdims=True)
            qs = jnp.sum(s_old * qc, axis=0, keepdims=True)
            v_new = beta[i:i + 1, H_B + h:H_B + h + 1] * (v[i:i + 1, hc] - egs * ks)
            o_scr[i:i + 1, hc] = egs * qs + qk[i:i + 1, h:h + 1] * v_new
            so_ref[i, h] = s_old * egs + kc * v_new
    gnw = gnw_ref[...]
    z = zb_ref[...]
    outs = [_rms(o_scr[:, h * DV_B:(h + 1) * DV_B], gnw) * _silu(z[:, h * DV_B:(h + 1) * DV_B]) for h in range(H_B)]
    ob_ref[...] = jnp.concatenate(outs, axis=1).astype(BF16)


def _gdn_decode(state, ei, qt, kt, v, eg, beta, qk, p, gnw, bb):
    nb = v.shape[0]
    ns = nb // bb
    narrow = pl.BlockSpec((bb, LANE), lambda i: (i, 0))
    tr = pl.BlockSpec((None, H_B, DK_B, bb), lambda i: (i, 0, 0, 0))
    zb0 = (4 * W_A + QKV_B) // W_B
    return pl.pallas_call(
        functools.partial(_gdn_decode_kernel, bb=bb),
        out_shape=(jax.ShapeDtypeStruct((nb, H_B, DK_B, DV_B), F32), jax.ShapeDtypeStruct((nb, W_B), BF16)),
        grid=(ns,),
        in_specs=[pl.BlockSpec((None, bb, H_B, DK_B, DV_B), lambda i: (ei, i, 0, 0, 0)),
                  tr, tr,
                  pl.BlockSpec((bb, W_B), lambda i: (i, 0)),
                  narrow, narrow, narrow,
                  pl.BlockSpec((bb, W_B), lambda i: (i, zb0)),
                  pl.BlockSpec((1, DV_B), lambda i: (0, 0))],
        out_specs=(pl.BlockSpec((bb, H_B, DK_B, DV_B), lambda i: (i, 0, 0, 0)),
                   pl.BlockSpec((bb, W_B), lambda i: (i, 0))),
        scratch_shapes=[pltpu.VMEM((bb, W_B), F32)],
        compiler_params=_cparams("arbitrary"),
        name="gdn_decode",
    )(state, qt, kt, v, eg, beta, qk, p, gnw)


def _out_proj_kernel(x_ref, oa_ref, ob_ref, w_ref, y_ref):
    y_ref[...] = (x_ref[...]
                  + jnp.dot(oa_ref[...], w_ref[0:W_A, :], preferred_element_type=F32)
                  + jnp.dot(ob_ref[...], w_ref[W_A:W_A + W_B, :], preferred_element_type=F32))


def _out_proj(x, oa, ob, w, tm):
    m, d = x.shape
    return pl.pallas_call(
        _out_proj_kernel,
        out_shape=jax.ShapeDtypeStruct((m, d), F32),
        grid=(m // tm,),
        in_specs=[pl.BlockSpec((tm, d), lambda i: (i, 0)),
                  pl.BlockSpec((tm, W_A), lambda i: (i, 0)),
                  pl.BlockSpec((tm, W_B), lambda i: (i, 0)),
                  pl.BlockSpec(w.shape, lambda i: (0, 0))],
        out_specs=pl.BlockSpec((tm, d), lambda i: (i, 0)),
        compiler_params=_cparams("arbitrary"),
        name="out_proj",
    )(x, oa, ob, w)


def _odd_prompt_kernel(x_ref, bg_ref, cg_ref, hh_ref, z_ref, cw_ref, w_ref, y_ref, sc_ref, carry_ref, *, rows):
    t = pl.program_id(1)

    @pl.when(t == 0)
    def _():
        carry_ref[...] = jnp.zeros(carry_ref.shape, F32)

    u = cg_ref[...] * hh_ref[...]
    cv = _shifted_conv(u, carry_ref[...], cw_ref[...])
    carry_ref[...] = u[rows - 8:rows]
    g = bg_ref[...] * cv * _silu(z_ref[...])
    y_ref[...] = x_ref[...] + jnp.dot(g.astype(BF16), w_ref[...], preferred_element_type=F32)

    @pl.when(t == pl.num_programs(1) - 1)
    def _():
        sc_ref[...] = u[rows - 8:rows]


def _odd_prompt(x, p, cw, w, nb, t, rows):
    m, d = x.shape
    nt = t // rows
    col = lambda c: pl.BlockSpec((rows, d), lambda b, i, c=c: (b * nt + i, c))
    full = lambda a: pl.BlockSpec(a.shape, lambda b, i: (0,) * a.ndim)
    return pl.pallas_call(
        functools.partial(_odd_prompt_kernel, rows=rows),
        out_shape=(jax.ShapeDtypeStruct((m, d), F32), jax.ShapeDtypeStruct((nb, 8, d), F32)),
        grid=(nb, nt),
        in_specs=[pl.BlockSpec((rows, d), lambda b, i: (b * nt + i, 0)),
                  col(0), col(1), col(2), col(3), full(cw), full(w)],
        out_specs=(pl.BlockSpec((rows, d), lambda b, i: (b * nt + i, 0)),
                   pl.BlockSpec((None, 8, d), lambda b, i: (b, 0, 0))),
        scratch_shapes=[pltpu.VMEM((8, d), F32)],
        compiler_params=_cparams("arbitrary", "arbitrary"),
        name="odd_prompt",
    )(x, p, p, p, p, cw, w)


def _odd_decode_kernel(x_ref, bg_ref, cg_ref, hh_ref, z_ref, b0_ref, b1_ref, cw_ref, w_ref, y_ref, u_ref):
    cw = cw_ref[...]
    u = cg_ref[...] * hh_ref[...]
    cv = b0_ref[...] * cw[0:1, :] + b1_ref[...] * cw[1:2, :] + u * cw[2:3, :]
    g = bg_ref[...] * cv * _silu(z_ref[...])
    y_ref[...] = x_ref[...] + jnp.dot(g.astype(BF16), w_ref[...], preferred_element_type=F32)
    u_ref[...] = u


def _odd_decode(x, p, b0, b1, cw, w):
    m, d = x.shape
    col = lambda c: pl.BlockSpec((m, d), lambda i, c=c: (0, c))
    full = lambda a: pl.BlockSpec(a.shape, lambda i: (0,) * a.ndim)
    return pl.pallas_call(
        _odd_decode_kernel,
        out_shape=(jax.ShapeDtypeStruct((m, d), F32), jax.ShapeDtypeStruct((m, d), F32)),
        grid=(1,),
        in_specs=[full(x), col(0), col(1), col(2), col(3), full(b0), full(b1), full(cw), full(w)],
        out_specs=(pl.BlockSpec((m, d), lambda i: (0, 0)), pl.BlockSpec((m, d), lambda i: (0, 0))),
        compiler_params=_cparams("arbitrary"),
        name="odd_decode",
    )(x, p, p, p, p, b0, b1, cw, w)


def _tile(n, want):
    t = math.gcd(n, want)
    assert t == n or t % 8 == 0, (n, want)
    return t


def _pad_lanes(v):
    return jnp.pad(v.astype(F32), (0, LANE - v.shape[0])).reshape(1, LANE)


def kernel(x_prompt, x_sample, cache_k, cache_v, page_table, state_gdn, state_gdn_conv, state_shortconv, norm_w, rel_table, w_in_even, w_out_even, qn_w, kn_w, lam_q1, lam_k1, lam_q2, lam_k2, subln_w, gdn_conv_w, gdn_a_log, gdn_dt_bias, gdn_norm_w, w_in_odd, sc_conv_w, w_out_odd):
    nbp, t, d = x_prompt.shape
    nbs = x_sample.shape[0]
    page = cache_k.shape[2]
    assert x_sample.shape[1] == 1 and page >= MAX_DISTANCE and t % 8 == 0
    n_even = (DEPTH + 1) // 2
    tq = _tile(t, 256)
    assert tq >= MAX_DISTANCE

    xp = x_prompt.reshape(nbp * t, d)
    xs = x_sample.reshape(nbs, d)
    tm_p = _tile(nbp * t, 1024)

    g64 = jnp.arange(W_A) // D_HA
    bd = jnp.where(g64[:, None] == g64[None, :], 1.0 / D_HA, 0.0).astype(BF16)
    bias_p = _bias_tiles(rel_table.astype(F32), tq)
    bias_s = _bias_decode(rel_table.astype(F32), page)

    kp, vp, sp, gcp, scp = [], [], [], [], []
    ks, vs, ss, gcs, scs = [], [], [], [], []
    ei = oi = 0
    for li in range(DEPTH):
        nw = norm_w[li].reshape(1, d)
        if li % 2 == 0:
            lambda_init = 0.8 - 0.6 * math.exp(-0.3 * li)
            w_in = jnp.pad(w_in_even[ei], ((0, 0), (0, P_EVEN_PAD - w_in_even.shape[2]))).astype(BF16)
            w_out = w_out_even[ei].astype(BF16)
            qw = jnp.tile(qn_w[ei], W_A // D_HA).reshape(1, W_A)
            kw = jnp.tile(kn_w[ei], W_A // D_HA).reshape(1, W_A)
            sw = subln_w[ei].reshape(1, DA)
            lam = (jnp.exp(jnp.sum(lam_q1[ei] * lam_k1[ei]).astype(F32))
                   - jnp.exp(jnp.sum(lam_q2[ei] * lam_k2[ei]).astype(F32)) + lambda_init).reshape(1)
            cw = gdn_conv_w[ei]
            alog = _pad_lanes(gdn_a_log[ei])
            dtb = _pad_lanes(gdn_dt_bias[ei])
            gnw = gdn_norm_w[ei].reshape(1, DV_B)
            tn = P_EVEN_PAD // 3

            p = _norm_proj(xp, nw, w_in, tm_p, tn)
            qb, kf, kb, vf, vb = _qkv_prep(p, qw, kw, bd, tm_p)
            oa = _attn_prompt(lam, qb, kb, vb, bias_p, p, sw, nbp, t, tq, 1.0 - lambda_init)
            ob, s_new = _gdn_prompt(p, cw, alog, dtb, gnw, nbp, t, _tile(t, 512))
            xp = _out_proj(xp, oa, ob, w_out, tm_p)
            kp.append(kf.reshape(nbp, t, H_A, DA))
            vp.append(vf.reshape(nbp, t, H_A, DA))
            sp.append(s_new)
            c0 = 4 * W_A
            gcp.append(p.reshape(nbp, t, -1)[:, t - (GDN_CONV - 1):, c0:c0 + QKV_B])

            p = _norm_proj(xs, nw, w_in, nbs, tn)
            qb, kf, kb, vf, vb = _qkv_prep(p, qw, kw, bd, nbs)
            oa = _attn_decode(page_table, lam, qb, kb, vb, cache_k, cache_v, ei, bias_s, p, sw, 1.0 - lambda_init)
            conv0_t = jnp.swapaxes(state_gdn_conv[ei], 0, 1)
            qn, kn, vv, eg, beta, qk, conv_new = _gdn_decode_prep(p, conv0_t, cw, alog, dtb)
            bb = 8
            to_cols = lambda a: a.reshape(nbs // bb, bb, H_B, DK_B).transpose(0, 2, 3, 1)
            s_new, ob = _gdn_decode(state_gdn, ei, to_cols(qn), to_cols(kn), vv, eg, beta, qk, p, gnw, bb)
            xs = _out_proj(xs, oa, ob, w_out, nbs)
            ks.append(kf.reshape(nbs, 1, H_A, DA))
            vs.append(vf.reshape(nbs, 1, H_A, DA))
            ss.append(s_new)
            gcs.append(jnp.swapaxes(conv_new, 0, 1))
            ei += 1
        else:
            w_in = w_in_odd[oi].astype(BF16)
            w_out = w_out_odd[oi].astype(BF16)
            cw = sc_conv_w[oi]
            p = _norm_proj(xp, nw, w_in, tm_p, 1024)
            xp, tail = _odd_prompt(xp, p, cw, w_out, nbp, t, _tile(t, 512))
            scp.append(tail[:, 8 - (SC_WIDTH - 1):, :])

            p = _norm_proj(xs, nw, w_in, nbs, 1024)
            buf0 = state_shortconv[oi]
            xs, u = _odd_decode(xs, p, buf0[:, 0, :], buf0[:, 1, :], cw, w_out)
            scs.append(jnp.stack([buf0[:, 1, :], u], axis=1))
            oi += 1

    return (xp.reshape(nbp, t, d), xs.reshape(nbs, 1, d),
            jnp.stack(kp), jnp.stack(vp), jnp.stack(sp), jnp.stack(gcp), jnp.stack(scp),
            jnp.stack(ks), jnp.stack(vs), jnp.stack(ss), jnp.stack(gcs), jnp.stack(scs))
```

```python
import functools
import math

import jax
import jax.numpy as jnp
from jax import lax
from jax.experimental import pallas as pl
from jax.experimental.pallas import tpu as pltpu

F32, BF16 = jnp.float32, jnp.bfloat16

DEPTH = 4
H_A, D_HA = 4, 64
DA = 2 * D_HA
W_A = H_A * DA
H_B, DK_B, DV_B = 4, 128, 128
W_B = H_B * DV_B
QKV_B = 2 * H_B * DK_B + H_B * DV_B
GDN_CONV, GDN_CHUNK, SC_WIDTH = 4, 64, 3
NUM_BUCKETS, MAX_EXACT, MAX_DISTANCE = 32, 16, 128
EPS, NEG = 1e-6, -1e30
LOG2E = math.log2(math.e)
LANE = 128
P_MAIN = 4 * W_A + QKV_B + W_B
P_EVEN_PAD = P_MAIN + LANE
AB_BLOCK = P_MAIN // LANE
VMEM_LIMIT = 48 * 1024 * 1024


def _cparams(*sem):
    return pltpu.CompilerParams(dimension_semantics=sem, vmem_limit_bytes=VMEM_LIMIT)


def _silu(z):
    return z / (1.0 + jnp.exp(-z))


def _sigmoid(z):
    return 1.0 / (1.0 + jnp.exp(-z))


def _dot(a, b):
    return jnp.dot(a.astype(BF16), b.astype(BF16), preferred_element_type=F32)


def _dot_nt(a, b):
    return lax.dot_general(a.astype(BF16), b.astype(BF16), (((1,), (1,)), ((), ())), preferred_element_type=F32)


def _split(a):
    hi = a.astype(BF16)
    return hi, (a - hi.astype(F32)).astype(BF16)


def _norm_proj_kernel(x_ref, nw_ref, w_ref, o_ref, h_ref):
    @pl.when(pl.program_id(1) == 0)
    def _():
        x = x_ref[...]
        ms = jnp.mean(x * x, axis=-1, keepdims=True)
        h_ref[...] = (x * lax.rsqrt(ms + EPS) * nw_ref[...]).astype(BF16)

    o_ref[...] = jnp.dot(h_ref[...], w_ref[...], preferred_element_type=F32)


def _norm_proj(x, nw, w, tm, tn):
    m, d = x.shape
    n = w.shape[1]
    return pl.pallas_call(
        _norm_proj_kernel,
        out_shape=jax.ShapeDtypeStruct((m, n), F32),
        grid=(m // tm, n // tn),
        in_specs=[pl.BlockSpec((tm, d), lambda i, j: (i, 0)),
                  pl.BlockSpec((1, d), lambda i, j: (0, 0)),
                  pl.BlockSpec((d, tn), lambda i, j: (0, j))],
        out_specs=pl.BlockSpec((tm, tn), lambda i, j: (i, j)),
        scratch_shapes=[pltpu.VMEM((tm, d), BF16)],
        compiler_params=_cparams("arbitrary", "arbitrary"),
        name="norm_proj",
    )(x, nw, w)


def _qkv_prep_kernel(q_ref, k_ref, v_ref, qw_ref, kw_ref, bd_ref, qo_ref, kf_ref, kb_ref, vf_ref, vb_ref):
    bd = bd_ref[...]

    def group_norm(x, w):
        hi, lo = _split(x * x)
        ms = jnp.dot(hi, bd, preferred_element_type=F32) + jnp.dot(lo, bd, preferred_element_type=F32)
        return x * lax.rsqrt(ms + EPS) * w

    qn = group_norm(q_ref[...], qw_ref[...])
    kn = group_norm(k_ref[...], kw_ref[...])
    qo_ref[...] = (qn * (D_HA ** -0.5 * LOG2E)).astype(BF16)
    kf_ref[...] = kn
    kb_ref[...] = kn.astype(BF16)
    v = v_ref[...]
    vf_ref[...] = v
    vb_ref[...] = v.astype(BF16)


def _qkv_prep(p, qw, kw, bd, tm):
    m = p.shape[0]
    col = lambda c: pl.BlockSpec((tm, W_A), lambda i, c=c: (i, c))
    full = lambda a: pl.BlockSpec(a.shape, lambda i: (0,) * a.ndim)
    row = pl.BlockSpec((tm, W_A), lambda i: (i, 0))
    shp = lambda dt: jax.ShapeDtypeStruct((m, W_A), dt)
    return pl.pallas_call(
        _qkv_prep_kernel,
        out_shape=(shp(BF16), shp(F32), shp(BF16), shp(F32), shp(BF16)),
        grid=(m // tm,),
        in_specs=[col(0), col(1), col(2), full(qw), full(kw), full(bd)],
        out_specs=(row, row, row, row, row),
        compiler_params=_cparams("arbitrary"),
        name="qkv_prep",
    )(p, p, p, qw, kw, bd)


def _t5_bias(n, tab_ref, h):
    nf = jnp.maximum(n, 1).astype(F32)
    large = MAX_EXACT + (jnp.log(nf / MAX_EXACT) / math.log(MAX_DISTANCE / MAX_EXACT)
                         * (NUM_BUCKETS - MAX_EXACT)).astype(jnp.int32)
    large = jnp.minimum(large, NUM_BUCKETS - 1)
    bkt = jnp.where(n < MAX_EXACT, n, large)
    out = jnp.zeros(n.shape, F32)
    for b in range(NUM_BUCKETS):
        out = jnp.where(bkt == b, tab_ref[b, h], out)
    return (out - tab_ref[NUM_BUCKETS - 1, h]) * LOG2E


def _bias_tiles_kernel(tab_ref, o_ref, *, tq):
    h = pl.program_id(0)
    i = lax.broadcasted_iota(jnp.int32, (tq, tq), 0)
    j = lax.broadcasted_iota(jnp.int32, (tq, tq), 1)
    n0 = i - j
    o_ref[0, 0] = jnp.where(n0 >= 0, _t5_bias(jnp.maximum(n0, 0), tab_ref, h), NEG)
    o_ref[0, 1] = _t5_bias(n0 + tq, tab_ref, h)


def _bias_tiles(rel_table, tq):
    return pl.pallas_call(
        functools.partial(_bias_tiles_kernel, tq=tq),
        out_shape=jax.ShapeDtypeStruct((H_A, 2, tq, tq), F32),
        grid=(H_A,),
        in_specs=[pl.BlockSpec(memory_space=pltpu.SMEM)],
        out_specs=pl.BlockSpec((1, 2, tq, tq), lambda h: (h, 0, 0, 0)),
        compiler_params=_cparams("arbitrary"),
        name="bias_tiles",
    )(rel_table)


def _bias_decode_kernel(tab_ref, o_ref, *, page):
    w = page * H_A
    row = lax.broadcasted_iota(jnp.int32, (2 * H_A, w), 0)
    lane = lax.broadcasted_iota(jnp.int32, (2 * H_A, w), 1)
    valid = (lane & (H_A - 1)) == (row >> 1)
    n = page - (lane >> 2)
    near = jnp.zeros((2 * H_A, w), F32)
    new = jnp.zeros((2 * H_A, LANE), F32)
    for h in range(H_A):
        near = jnp.where((row >> 1) == h, _t5_bias(n, tab_ref, h), near)
        new = jnp.where((row[:, :LANE] >> 1) == h, _t5_bias(jnp.zeros((2 * H_A, LANE), jnp.int32), tab_ref, h), new)
    o_ref[:, 0:w] = jnp.where(valid, 0.0, NEG)
    o_ref[:, w:2 * w] = jnp.where(valid, near, NEG)
    o_ref[:, 2 * w:2 * w + LANE] = new


def _bias_decode(rel_table, page):
    assert H_A == 4
    return pl.pallas_call(
        functools.partial(_bias_decode_kernel, page=page),
        out_shape=jax.ShapeDtypeStruct((2 * H_A, 2 * page * H_A + LANE), F32),
        in_specs=[pl.BlockSpec(memory_space=pltpu.SMEM)],
        out_specs=pl.BlockSpec(memory_space=pltpu.VMEM),
        name="bias_decode",
    )(rel_table)


def _attn_finish(o, sw, z, out_scale):
    ms = jnp.mean(o * o, axis=-1, keepdims=True)
    return o * lax.rsqrt(ms + EPS) * sw * out_scale * _silu(z)


def _attn_prompt_kernel(lam_ref, q_ref, k_ref, v_ref, bias_ref, za_ref, sw_ref, o_ref, m, l, a, *, tq, out_scale):
    qi = pl.program_id(2)
    q = q_ref[...].astype(F32)
    lane = lax.broadcasted_iota(jnp.int32, q.shape, 1)
    q2 = jnp.concatenate([jnp.where(lane < D_HA, q, 0.0), jnp.where(lane >= D_HA, q, 0.0)], axis=0).astype(BF16)
    m[...] = jnp.full(m.shape, NEG, F32)
    l[...] = jnp.zeros(l.shape, F32)
    a[...] = jnp.zeros(a.shape, F32)
    reps = tq // LANE

    def step(j, bias):
        rows = pl.ds(pl.multiple_of(j * tq, tq), tq)
        s = lax.dot_general(q2, k_ref[rows, :], (((1,), (1,)), ((), ())), preferred_element_type=F32)
        if bias is not None:
            s = s + jnp.concatenate([bias, bias], axis=0)
        m_prev = m[...]
        m_new = jnp.maximum(m_prev, jnp.max(s, axis=1, keepdims=True))
        p = jnp.exp2(s - pltpu.repeat(m_new, reps, axis=1))
        alpha = jnp.exp2(m_prev - m_new)
        l[...] = alpha * l[...] + jnp.sum(p, axis=1, keepdims=True)
        a[...] = alpha * a[...] + jnp.dot(p.astype(BF16), v_ref[rows, :], preferred_element_type=F32)
        m[...] = m_new

    def far(j, c):
        step(j, None)
        return c

    lax.fori_loop(0, jnp.maximum(qi - 1, 0), far, 0)

    @pl.when(qi >= 1)
    def _():
        step(qi - 1, bias_ref[0, 1])

    step(qi, bias_ref[0, 0])
    o = a[...] / l[...]
    o = o[:tq] - lam_ref[0] * o[tq:]
    o_ref[...] = _attn_finish(o, sw_ref[...], za_ref[...], out_scale).astype(BF16)


def _attn_prompt(lam, q, k, v, bias, p, sw, nb, t, tq, out_scale):
    m = q.shape[0]
    nq = t // tq
    za0 = (3 * W_A) // DA
    return pl.pallas_call(
        functools.partial(_attn_prompt_kernel, tq=tq, out_scale=out_scale),
        out_shape=jax.ShapeDtypeStruct((m, W_A), BF16),
        grid=(nb, H_A, nq),
        in_specs=[pl.BlockSpec(memory_space=pltpu.SMEM),
                  pl.BlockSpec((tq, DA), lambda b, h, i: (b * nq + i, h)),
                  pl.BlockSpec((t, DA), lambda b, h, i: (b, h)),
                  pl.BlockSpec((t, DA), lambda b, h, i: (b, h)),
                  pl.BlockSpec((1, 2, tq, tq), lambda b, h, i: (h, 0, 0, 0)),
                  pl.BlockSpec((tq, DA), lambda b, h, i: (b * nq + i, za0 + h)),
                  pl.BlockSpec((1, DA), lambda b, h, i: (0, 0))],
        out_specs=pl.BlockSpec((tq, DA), lambda b, h, i: (b * nq + i, h)),
        scratch_shapes=[pltpu.VMEM((2 * tq, LANE), F32), pltpu.VMEM((2 * tq, LANE), F32),
                        pltpu.VMEM((2 * tq, DA), F32)],
        compiler_params=_cparams("arbitrary", "arbitrary", "arbitrary"),
        name="attn_prompt",
    )(lam, q, k, v, bias, p, sw)


def _attn_decode_kernel(pt_ref, lam_ref, q_ref, kn_ref, vn_ref, *refs, page, npg, out_scale):
    ck_refs, cv_refs = refs[:npg], refs[npg:2 * npg]
    bias_ref, za_ref, sw_ref, o_ref = refs[2 * npg:]
    rows, w = 2 * H_A, page * H_A
    row = lax.broadcasted_iota(jnp.int32, (rows, DA), 0)
    lane = lax.broadcasted_iota(jnp.int32, (rows, DA), 1)

    def head_rows(x):
        out = jnp.zeros((rows, DA), F32)
        for h in range(H_A):
            out = jnp.where((row >> 1) == h, jnp.broadcast_to(x[:, h * DA:(h + 1) * DA], (rows, DA)), out)
        return out

    q8 = jnp.where((lane >> 6) == (row & 1), head_rows(q_ref[...].astype(F32)), 0.0)
    q8b = q8.astype(BF16)
    s_all = []
    for j in range(npg):
        s = lax.dot_general(q8b, ck_refs[j][...].astype(BF16), (((1,), (1,)), ((), ())), preferred_element_type=F32)
        s_all.append(s + (bias_ref[:, w:2 * w] if j == npg - 1 else bias_ref[:, 0:w]))
    s_new = (jnp.sum(q8 * head_rows(kn_ref[...].astype(F32)), axis=1, keepdims=True)
             + bias_ref[:, 2 * w:2 * w + 1])
    m = s_all[0]
    for s in s_all[1:]:
        m = jnp.maximum(m, s)
    m = jnp.maximum(jnp.max(m, axis=1, keepdims=True), s_new)
    p_new = jnp.exp2(s_new - m)
    l = p_new
    acc = p_new * head_rows(vn_ref[...].astype(F32))
    for j in range(npg):
        p = jnp.exp2(s_all[j] - m)
        l = l + jnp.sum(p, axis=1, keepdims=True)
        acc = acc + jnp.dot(p.astype(BF16), cv_refs[j][...].astype(BF16), preferred_element_type=F32)
    o = acc / l
    sw = sw_ref[...]
    z = za_ref[...]
    outs = []
    for h in range(H_A):
        oh = o[2 * h:2 * h + 1, :] - lam_ref[0] * o[2 * h + 1:2 * h + 2, :]
        outs.append(_attn_finish(oh, sw, z[:, h * DA:(h + 1) * DA], out_scale))
    o_ref[...] = jnp.concatenate(outs, axis=1).astype(BF16)


def _attn_decode(page_table, lam, q, kn, vn, cache_k, cache_v, ei, bias, p, sw, out_scale):
    nb, npg = page_table.shape
    page = cache_k.shape[2]
    ck = cache_k.reshape(cache_k.shape[0], cache_k.shape[1], page * H_A, DA)
    cv = cache_v.reshape(cache_v.shape[0], cache_v.shape[1], page * H_A, DA)
    r3 = lambda a: a.reshape(nb, 1, a.shape[-1])
    vec = lambda c: pl.BlockSpec((None, 1, W_A), lambda b, pt, c=c: (b, 0, c))
    cache = [pl.BlockSpec((None, None, page * H_A, DA), lambda b, pt, j=j: (ei, pt[b * npg + j], 0, 0))
             for j in range(npg)]
    out = pl.pallas_call(
        functools.partial(_attn_decode_kernel, page=page, npg=npg, out_scale=out_scale),
        out_shape=jax.ShapeDtypeStruct((nb, 1, W_A), BF16),
        grid_spec=pltpu.PrefetchScalarGridSpec(
            num_scalar_prefetch=1,
            grid=(nb,),
            in_specs=[pl.BlockSpec(memory_space=pltpu.SMEM), vec(0), vec(0), vec(0)] + cache + cache
                     + [pl.BlockSpec(bias.shape, lambda b, pt: (0, 0)),
                        vec(3),
                        pl.BlockSpec((1, DA), lambda b, pt: (0, 0))],
            out_specs=pl.BlockSpec((None, 1, W_A), lambda b, pt: (b, 0, 0))),
        compiler_params=_cparams("arbitrary"),
        name="attn_decode",
    )(page_table.reshape(-1), lam, r3(q), r3(kn), r3(vn), *([ck] * npg), *([cv] * npg), bias, r3(p), sw)
    return out.reshape(nb, W_A)


def _shifted_conv(x, prev, w):
    taps = w.shape[0]
    r8 = lax.broadcasted_iota(jnp.int32, (8, x.shape[1]), 0)
    y = x * w[taps - 1:taps, :]
    xt = x[:8]
    yt = xt * w[taps - 1:taps, :]
    for s in range(1, taps):
        ws = w[taps - 1 - s:taps - s, :]
        y = y + pltpu.roll(x, s, 0) * ws
        yt = yt + jnp.where(r8 < s, pltpu.roll(prev, s, 0), pltpu.roll(xt, s, 0)) * ws
    return jnp.concatenate([yt, y[8:]], axis=0)


def _gates(ab, alog, dtb):
    x = ab + dtb
    sp = jnp.maximum(x, 0.0) + jnp.log(1.0 + jnp.exp(-jnp.abs(x)))
    return -jnp.exp(alog) * sp, _sigmoid(ab)


def _l2norm(x):
    return x * lax.rsqrt(jnp.sum(x * x, axis=-1, keepdims=True) + EPS)


def _rms(x, w):
    return x * lax.rsqrt(jnp.mean(x * x, axis=-1, keepdims=True) + EPS) * w


def _level_masks(c):
    i = lax.broadcasted_iota(jnp.int32, (c, c), 0)
    j = lax.broadcasted_iota(jnp.int32, (c, c), 1)
    masks = []
    s = 1
    while s < c:
        sh = s.bit_length() - 1
        masks.append((((i >> (sh + 1)) == (j >> (sh + 1))) & ((i >> sh) != (j >> sh)) & (i > j)).astype(F32))
        s *= 2
    return masks


def _conv_from_ref(ref, prev, w, rows):
    taps = w.shape[0]
    r8 = lax.broadcasted_iota(jnp.int32, (8, w.shape[1]), 0)
    xt = ref[0:8, :]
    yt = xt * w[taps - 1:taps, :]
    y = ref[8:rows, :] * w[taps - 1:taps, :]
    for s in range(1, taps):
        ws = w[taps - 1 - s:taps - s, :]
        y = y + ref[8 - s:rows - s, :] * ws
        yt = yt + jnp.where(r8 < s, pltpu.roll(prev, s, 0), pltpu.roll(xt, s, 0)) * ws
    return jnp.concatenate([yt, y], axis=0)


def _gdn_prompt_kernel(pq_ref, pk_ref, pv_ref, zb_ref, ab_ref, cw_ref, alog_ref, dtb_ref, gnw_ref,
                       ob_ref, s_ref, carry_ref, cb_ref, g_ref, beta_ref, u_ref, wq_ref, ak_ref, gl_ref,
                       *, rows, chunk):
    t = pl.program_id(1)
    c = chunk

    @pl.when(t == 0)
    def _():
        carry_ref[...] = jnp.zeros(carry_ref.shape, F32)
        s_ref[...] = jnp.zeros(s_ref.shape, F32)

    cw = cw_ref[...]
    for seg, ref in enumerate((pq_ref, pk_ref, pv_ref)):
        cols = slice(seg * W_B, (seg + 1) * W_B)
        cb_ref[:, cols] = _silu(_conv_from_ref(ref, carry_ref[:, cols], cw[:, cols], rows))
        carry_ref[:, cols] = ref[rows - 8:rows, :]
    g, beta = _gates(ab_ref[...], alog_ref[...], dtb_ref[...])
    g_ref[...] = g
    beta_ref[...] = beta

    ii = lax.broadcasted_iota(jnp.int32, (c, c), 0)
    jj = lax.broadcasted_iota(jnp.int32, (c, c), 1)
    incl = ii >= jj
    ltri = incl.astype(BF16)
    masks = _level_masks(c)

    def prep(ci):
        rs = pl.ds(pl.multiple_of(ci * c, c), c)
        gch = g_ref[rs, :]
        g1 = gch.astype(BF16)
        r1 = gch - g1.astype(F32)
        g2 = r1.astype(BF16)
        g3 = (r1 - g2.astype(F32)).astype(BF16)
        d = functools.partial(jnp.dot, preferred_element_type=F32)
        gc = d(ltri, g1) + d(ltri, g2) + d(ltri, g3)
        gct = jnp.concatenate([gc, gc], axis=0).T
        ge = jnp.exp(gc)
        kdec = jnp.exp(gc[c - 1:c, :] - gc)
        gl_ref[ci] = ge[c - 1:c, :]
        return rs, gc, gct, ge, kdec, beta_ref[rs, :]

    def local(i2, carry):
        items = []
        for ci in (2 * i2, 2 * i2 + 1):
            rs, gc, gct, ge, kdec, bch = prep(ci)
            for h in range(H_B):
                hc = slice(h * DK_B, (h + 1) * DK_B)
                qh = _l2norm(cb_ref[rs, hc]) * (DK_B ** -0.5)
                kh = _l2norm(cb_ref[rs, slice(W_B + h * DK_B, W_B + (h + 1) * DK_B)])
                vh = cb_ref[rs, slice(2 * W_B + h * DV_B, 2 * W_B + (h + 1) * DV_B)]
                bcol = bch[:, H_B + h:H_B + h + 1]
                gecol = ge[:, h:h + 1]
                decay = jnp.exp(jnp.where(incl, gc[:, h:h + 1] - gct[h:h + 1, :c], NEG))
                kbeta = kh * bcol
                wq_ref[ci, h, c:2 * c] = (qh * gecol).astype(BF16)
                ak_ref[ci, h, c:c + DK_B] = (kh * kdec[:, h:h + 1]).T.astype(BF16)
                items.append(dict(ci=ci, h=h, rs=rs, hc=hc, decay=decay, qk=(kbeta, qh, kh),
                                  rhs=jnp.concatenate([vh * bcol, kbeta * gecol], axis=1)))
        for it in items:
            kbeta, qh, kh = it.pop("qk")
            both = _dot_nt(jnp.concatenate([kbeta, qh], axis=0), kh)
            it["mm"] = jnp.where(ii > jj, both[:c] * it["decay"], 0.0)
            ak_ref[it["ci"], it["h"], 0:c] = (both[c:] * it["decay"]).astype(BF16)
        for it in items:
            it["pp"] = -(it["mm"] * masks[0])
        for mk in masks[1:]:
            for it in items:
                e = it["mm"] * mk
                it["x"] = e + _dot(it["pp"], e)
            for it in items:
                it["pp"] = it["pp"] - (it["x"] + _dot(it["x"], it["pp"]))
        for it in items:
            uw = it["rhs"] + _dot(it["pp"], it["rhs"])
            u_ref[it["rs"], it["hc"]] = uw[:, :DV_B]
            wq_ref[it["ci"], it["h"], 0:c] = uw[:, DV_B:].astype(BF16)
        return carry

    lax.fori_loop(0, rows // (2 * c), local, 0)
    gnw = gnw_ref[...]

    def scan(ci, carry):
        rs = pl.ds(pl.multiple_of(ci * c, c), c)
        gl = gl_ref[ci]
        for h in range(H_B):
            hc = slice(h * DK_B, (h + 1) * DK_B)
            s_old = s_ref[h]
            r = jnp.dot(wq_ref[ci, h], s_old.astype(BF16), preferred_element_type=F32)
            v_new = u_ref[rs, hc] - r[:c]
            r2 = jnp.dot(ak_ref[ci, h], v_new.astype(BF16), preferred_element_type=F32)
            s_ref[h] = s_old * gl[:, h:h + 1] + r2[c:]
            ob_ref[rs, hc] = (_rms(r[c:] + r2[:c], gnw) * _silu(zb_ref[rs, hc])).astype(BF16)
        return carry

    lax.fori_loop(0, rows // c, scan, 0)


def _gdn_prompt(p, cw, alog, dtb, gnw, nb, t, rows):
    m = p.shape[0]
    nt = t // rows
    c = math.gcd(GDN_CHUNK, t)
    col = lambda cidx: pl.BlockSpec((rows, W_B), lambda b, i, cidx=cidx: (b * nt + i, cidx))
    full = lambda a: pl.BlockSpec(a.shape, lambda b, i: (0,) * a.ndim)
    c0 = (4 * W_A) // W_B
    return pl.pallas_call(
        functools.partial(_gdn_prompt_kernel, rows=rows, chunk=c),
        out_shape=(jax.ShapeDtypeStruct((m, W_B), BF16), jax.ShapeDtypeStruct((nb, H_B, DK_B, DV_B), F32)),
        grid=(nb, nt),
        in_specs=[col(c0), col(c0 + 1), col(c0 + 2), col(c0 + 3),
                  pl.BlockSpec((rows, LANE), lambda b, i: (b * nt + i, AB_BLOCK)),
                  full(cw), full(alog), full(dtb), full(gnw)],
        out_specs=(pl.BlockSpec((rows, W_B), lambda b, i: (b * nt + i, 0)),
                   pl.BlockSpec((None, H_B, DK_B, DV_B), lambda b, i: (b, 0, 0, 0))),
        scratch_shapes=[pltpu.VMEM((8, QKV_B), F32), pltpu.VMEM((rows, QKV_B), F32),
                        pltpu.VMEM((rows, LANE), F32), pltpu.VMEM((rows, LANE), F32),
                        pltpu.VMEM((rows, W_B), F32),
                        pltpu.VMEM((rows // c, H_B, 2 * c, DK_B), BF16),
                        pltpu.VMEM((rows // c, H_B, c + DK_B, c), BF16),
                        pltpu.VMEM((rows // c, 1, LANE), F32)],
        compiler_params=_cparams("arbitrary", "arbitrary"),
        name="gdn_prompt",
    )(p, p, p, p, p, cw, alog, dtb, gnw)


def _gdn_decode_prep_kernel(pq_ref, pk_ref, pv_ref, ab_ref, c0_ref, cw_ref, alog_ref, dtb_ref,
                            q_ref, k_ref, v_ref, eg_ref, beta_ref, qk_ref, cn_ref):
    cw = cw_ref[...]
    taps = cw.shape[0]
    outs = (q_ref, k_ref, v_ref)
    for seg, ref in enumerate((pq_ref, pk_ref, pv_ref)):
        cols = slice(seg * W_B, (seg + 1) * W_B)
        x = ref[...]
        y = x * cw[taps - 1:taps, cols]
        for j in range(taps - 1):
            y = y + c0_ref[j, :, cols] * cw[j:j + 1, cols]
            if j >= 1:
                cn_ref[j - 1, :, cols] = c0_ref[j, :, cols]
        cn_ref[taps - 2, :, cols] = x
        outs[seg][...] = _silu(y)
    lane = lax.broadcasted_iota(jnp.int32, eg_ref.shape, 1)
    qk = jnp.zeros(eg_ref.shape, F32)
    for h in range(H_B):
        hc = slice(h * DK_B, (h + 1) * DK_B)
        qh = _l2norm(q_ref[:, hc]) * (DK_B ** -0.5)
        kh = _l2norm(k_ref[:, hc])
        q_ref[:, hc] = qh
        k_ref[:, hc] = kh
        qk = jnp.where(lane == h, jnp.sum(qh * kh, axis=-1, keepdims=True), qk)
    g, beta = _gates(ab_ref[...], alog_ref[...], dtb_ref[...])
    eg_ref[...] = jnp.exp(g)
    beta_ref[...] = beta
    qk_ref[...] = qk


def _gdn_decode_prep(p, conv0_t, cw, alog, dtb):
    nb = p.shape[0]
    c0 = (4 * W_A) // W_B
    col = lambda cidx: pl.BlockSpec((nb, W_B), lambda i, cidx=cidx: (0, cidx))
    full = lambda a: pl.BlockSpec(a.shape, lambda i: (0,) * a.ndim)
    wide = jax.ShapeDtypeStruct((nb, W_B), F32)
    narrow = jax.ShapeDtypeStruct((nb, LANE), F32)
    ospec = lambda s: pl.BlockSpec(s.shape, lambda i: (0,) * len(s.shape))
    outs = (wide, wide, wide, narrow, narrow, narrow, jax.ShapeDtypeStruct(conv0_t.shape, F32))
    return pl.pallas_call(
        _gdn_decode_prep_kernel,
        out_shape=outs,
        grid=(1,),
        in_specs=[col(c0), col(c0 + 1), col(c0 + 2),
                  pl.BlockSpec((nb, LANE), lambda i: (0, AB_BLOCK)),
                  full(conv0_t), full(cw), full(alog), full(dtb)],
        out_specs=tuple(ospec(s) for s in outs),
        compiler_params=_cparams("arbitrary"),
        name="gdn_decode_prep",
    )(p, p, p, p, conv0_t, cw, alog, dtb)


def _gdn_decode_kernel(s_ref, qt_ref, kt_ref, v_ref, eg_ref, beta_ref, qk_ref, zb_ref, gnw_ref,
                       so_ref, ob_ref, o_scr, *, bb):
    v = v_ref[...]
    eg = eg_ref[...]
    beta = beta_ref[...]
    qk = qk_ref[...]
    for h in range(H_B):
        hc = slice(h * DV_B, (h + 1) * DV_B)
        qt = qt_ref[h]
        kt = kt_ref[h]
        for i in range(bb):
            s_old = s_ref[i, h]
            kc = kt[:, i:i + 1]
            qc = qt[:, i:i + 1]
            egs = eg[i:i + 1, h:h + 1]
            ks = jnp.sum(s_old * kc, axis=0, keepdims=True)
            qs = jnp.sum(s_old * qc, axis=0, keepdims=True)
            v_new = beta[i:i + 1, H_B + h:H_B + h + 1] * (v[i:i + 1, hc] - egs * ks)
            o_scr[i:i + 1, hc] = egs * qs + qk[i:i + 1, h:h + 1] * v_new
            so_ref[i, h] = s_old * egs + kc * v_new
    gnw = gnw_ref[...]
    z = zb_ref[...]
    outs = [_rms(o_scr[:, h * DV_B:(h + 1) * DV_B], gnw) * _silu(z[:, h * DV_B:(h + 1) * DV_B]) for h in range(H_B)]
    ob_ref[...] = jnp.concatenate(outs, axis=1).astype(BF16)


def _gdn_decode(state, ei, qt, kt, v, eg, beta, qk, p, gnw, bb):
    nb = v.shape[0]
    ns = nb // bb
    narrow = pl.BlockSpec((bb, LANE), lambda i: (i, 0))
    tr = pl.BlockSpec((None, H_B, DK_B, bb), lambda i: (i, 0, 0, 0))
    zb0 = (4 * W_A + QKV_B) // W_B
    return pl.pallas_call(
        functools.partial(_gdn_decode_kernel, bb=bb),
        out_shape=(jax.ShapeDtypeStruct((nb, H_B, DK_B, DV_B), F32), jax.ShapeDtypeStruct((nb, W_B), BF16)),
        grid=(ns,),
        in_specs=[pl.BlockSpec((None, bb, H_B, DK_B, DV_B), lambda i: (ei, i, 0, 0, 0)),
                  tr, tr,
                  pl.BlockSpec((bb, W_B), lambda i: (i, 0)),
                  narrow, narrow, narrow,
                  pl.BlockSpec((bb, W_B), lambda i: (i, zb0)),
                  pl.BlockSpec((1, DV_B), lambda i: (0, 0))],
        out_specs=(pl.BlockSpec((bb, H_B, DK_B, DV_B), lambda i: (i, 0, 0, 0)),
                   pl.BlockSpec((bb, W_B), lambda i: (i, 0))),
        scratch_shapes=[pltpu.VMEM((bb, W_B), F32)],
        compiler_params=_cparams("arbitrary"),
        name="gdn_decode",
    )(state, qt, kt, v, eg, beta, qk, p, gnw)


def _out_proj_kernel(x_ref, oa_ref, ob_ref, w_ref, y_ref):
    y_ref[...] = (x_ref[...]
                  + jnp.dot(oa_ref[...], w_ref[0:W_A, :], preferred_element_type=F32)
                  + jnp.dot(ob_ref[...], w_ref[W_A:W_A + W_B, :], preferred_element_type=F32))


def _out_proj(x, oa, ob, w, tm):
    m, d = x.shape
    return pl.pallas_call(
        _out_proj_kernel,
        out_shape=jax.ShapeDtypeStruct((m, d), F32),
        grid=(m // tm,),
        in_specs=[pl.BlockSpec((tm, d), lambda i: (i, 0)),
                  pl.BlockSpec((tm, W_A), lambda i: (i, 0)),
                  pl.BlockSpec((tm, W_B), lambda i: (i, 0)),
                  pl.BlockSpec(w.shape, lambda i: (0, 0))],
        out_specs=pl.BlockSpec((tm, d), lambda i: (i, 0)),
        compiler_params=_cparams("arbitrary"),
        name="out_proj",
    )(x, oa, ob, w)


def _odd_prompt_kernel(x_ref, bg_ref, cg_ref, hh_ref, z_ref, cw_ref, w_ref, y_ref, sc_ref, carry_ref, *, rows):
    t = pl.program_id(1)

    @pl.when(t == 0)
    def _():
        carry_ref[...] = jnp.zeros(carry_ref.shape, F32)

    u = cg_ref[...] * hh_ref[...]
    cv = _shifted_conv(u, carry_ref[...], cw_ref[...])
    carry_ref[...] = u[rows - 8:rows]
    g = bg_ref[...] * cv * _silu(z_ref[...])
    y_ref[...] = x_ref[...] + jnp.dot(g.astype(BF16), w_ref[...], preferred_element_type=F32)

    @pl.when(t == pl.num_programs(1) - 1)
    def _():
        sc_ref[...] = u[rows - 8:rows]


def _odd_prompt(x, p, cw, w, nb, t, rows):
    m, d = x.shape
    nt = t // rows
    col = lambda c: pl.BlockSpec((rows, d), lambda b, i, c=c: (b * nt + i, c))
    full = lambda a: pl.BlockSpec(a.shape, lambda b, i: (0,) * a.ndim)
    return pl.pallas_call(
        functools.partial(_odd_prompt_kernel, rows=rows),
        out_shape=(jax.ShapeDtypeStruct((m, d), F32), jax.ShapeDtypeStruct((nb, 8, d), F32)),
        grid=(nb, nt),
        in_specs=[pl.BlockSpec((rows, d), lambda b, i: (b * nt + i, 0)),
                  col(0), col(1), col(2), col(3), full(cw), full(w)],
        out_specs=(pl.BlockSpec((rows, d), lambda b, i: (b * nt + i, 0)),
                   pl.BlockSpec((None, 8, d), lambda b, i: (b, 0, 0))),
        scratch_shapes=[pltpu.VMEM((8, d), F32)],
        compiler_params=_cparams("arbitrary", "arbitrary"),
        name="odd_prompt",
    )(x, p, p, p, p, cw, w)


def _odd_decode_kernel(x_ref, bg_ref, cg_ref, hh_ref, z_ref, b0_ref, b1_ref, cw_ref, w_ref, y_ref, u_ref):
    cw = cw_ref[...]
    u = cg_ref[...] * hh_ref[...]
    cv = b0_ref[...] * cw[0:1, :] + b1_ref[...] * cw[1:2, :] + u * cw[2:3, :]
    g = bg_ref[...] * cv * _silu(z_ref[...])
    y_ref[...] = x_ref[...] + jnp.dot(g.astype(BF16), w_ref[...], preferred_element_type=F32)
    u_ref[...] = u


def _odd_decode(x, p, b0, b1, cw, w):
    m, d = x.shape
    col = lambda c: pl.BlockSpec((m, d), lambda i, c=c: (0, c))
    full = lambda a: pl.BlockSpec(a.shape, lambda i: (0,) * a.ndim)
    return pl.pallas_call(
        _odd_decode_kernel,
        out_shape=(jax.ShapeDtypeStruct((m, d), F32), jax.ShapeDtypeStruct((m, d), F32)),
        grid=(1,),
        in_specs=[full(x), col(0), col(1), col(2), col(3), full(b0), full(b1), full(cw), full(w)],
        out_specs=(pl.BlockSpec((m, d), lambda i: (0, 0)), pl.BlockSpec((m, d), lambda i: (0, 0))),
        compiler_params=_cparams("arbitrary"),
        name="odd_decode",
    )(x, p, p, p, p, b0, b1, cw, w)


def _tile(n, want):
    t = math.gcd(n, want)
    assert t == n or t % 8 == 0, (n, want)
    return t


def _pad_lanes(v):
    return jnp.pad(v.astype(F32), (0, LANE - v.shape[0])).reshape(1, LANE)


def kernel(x_prompt, x_sample, cache_k, cache_v, page_table, state_gdn, state_gdn_conv, state_shortconv, norm_w, rel_table, w_in_even, w_out_even, qn_w, kn_w, lam_q1, lam_k1, lam_q2, lam_k2, subln_w, gdn_conv_w, gdn_a_log, gdn_dt_bias, gdn_norm_w, w_in_odd, sc_conv_w, w_out_odd):
    nbp, t, d = x_prompt.shape
    nbs = x_sample.shape[0]
    page = cache_k.shape[2]
    assert x_sample.shape[1] == 1 and page >= MAX_DISTANCE and t % 8 == 0
    tq = _tile(t, 512)
    assert tq >= MAX_DISTANCE

    xp = x_prompt.reshape(nbp * t, d)
    xs = x_sample.reshape(nbs, d)
    tm_p = _tile(nbp * t, 1024)

    g64 = jnp.arange(W_A) // D_HA
    bd = jnp.where(g64[:, None] == g64[None, :], 1.0 / D_HA, 0.0).astype(BF16)
    bias_p = _bias_tiles(rel_table.astype(F32), tq)
    bias_s = _bias_decode(rel_table.astype(F32), page)

    kp, vp, sp, gcp, scp = [], [], [], [], []
    ks, vs, ss, gcs, scs = [], [], [], [], []
    ei = oi = 0
    for li in range(DEPTH):
        nw = norm_w[li].reshape(1, d)
        if li % 2 == 0:
            lambda_init = 0.8 - 0.6 * math.exp(-0.3 * li)
            w_in = jnp.pad(w_in_even[ei], ((0, 0), (0, P_EVEN_PAD - w_in_even.shape[2]))).astype(BF16)
            w_out = w_out_even[ei].astype(BF16)
            qw = jnp.tile(qn_w[ei], W_A // D_HA).reshape(1, W_A)
            kw = jnp.tile(kn_w[ei], W_A // D_HA).reshape(1, W_A)
            sw = subln_w[ei].reshape(1, DA)
            lam = (jnp.exp(jnp.sum(lam_q1[ei] * lam_k1[ei]).astype(F32))
                   - jnp.exp(jnp.sum(lam_q2[ei] * lam_k2[ei]).astype(F32)) + lambda_init).reshape(1)
            cw = gdn_conv_w[ei]
            alog = _pad_lanes(gdn_a_log[ei])
            dtb = _pad_lanes(gdn_dt_bias[ei])
            gnw = gdn_norm_w[ei].reshape(1, DV_B)
            tn = P_EVEN_PAD // 3

            p = _norm_proj(xp, nw, w_in, tm_p, tn)
            qb, kf, kb, vf, vb = _qkv_prep(p, qw, kw, bd, tm_p)
            oa = _attn_prompt(lam, qb, kb, vb, bias_p, p, sw, nbp, t, tq, 1.0 - lambda_init)
            ob, s_new = _gdn_prompt(p, cw, alog, dtb, gnw, nbp, t, _tile(t, 512))
            xp = _out_proj(xp, oa, ob, w_out, tm_p)
            kp.append(kf.reshape(nbp, t, H_A, DA))
            vp.append(vf.reshape(nbp, t, H_A, DA))
            sp.append(s_new)
            c0 = 4 * W_A
            gcp.append(p.reshape(nbp, t, -1)[:, t - (GDN_CONV - 1):, c0:c0 + QKV_B])

            p = _norm_proj(xs, nw, w_in, nbs, tn)
            qb, kf, kb, vf, vb = _qkv_prep(p, qw, kw, bd, nbs)
            oa = _attn_decode(page_table, lam, qb, kb, vb, cache_k, cache_v, ei, bias_s, p, sw, 1.0 - lambda_init)
            conv0_t = jnp.swapaxes(state_gdn_conv[ei], 0, 1)
            qn, kn, vv, eg, beta, qk, conv_new = _gdn_decode_prep(p, conv0_t, cw, alog, dtb)
            bb = 8
            to_cols = lambda a: a.reshape(nbs // bb, bb, H_B, DK_B).transpose(0, 2, 3, 1)
            s_new, ob = _gdn_decode(state_gdn, ei, to_cols(qn), to_cols(kn), vv, eg, beta, qk, p, gnw, bb)
            xs = _out_proj(xs, oa, ob, w_out, nbs)
            ks.append(kf.reshape(nbs, 1, H_A, DA))
            vs.append(vf.reshape(nbs, 1, H_A, DA))
            ss.append(s_new)
            gcs.append(jnp.swapaxes(conv_new, 0, 1))
            ei += 1
        else:
            w_in = w_in_odd[oi].astype(BF16)
            w_out = w_out_odd[oi].astype(BF16)
            cw = sc_conv_w[oi]
            p = _norm_proj(xp, nw, w_in, tm_p, 1024)
            xp, tail = _odd_prompt(xp, p, cw, w_out, nbp, t, _tile(t, 512))
            scp.append(tail[:, 8 - (SC_WIDTH - 1):, :])

            p = _norm_proj(xs, nw, w_in, nbs, 1024)
            buf0 = state_shortconv[oi]
            xs, u = _odd_decode(xs, p, buf0[:, 0, :], buf0[:, 1, :], cw, w_out)
            scs.append(jnp.stack([buf0[:, 1, :], u], axis=1))
            oi += 1

    return (xp.reshape(nbp, t, d), xs.reshape(nbs, 1, d),
            jnp.stack(kp), jnp.stack(vp), jnp.stack(sp), jnp.stack(gcp), jnp.stack(scp),
            jnp.stack(ks), jnp.stack(vs), jnp.stack(ss), jnp.stack(gcs), jnp.stack(scs))
```

```python
import functools
import math

import jax
import jax.numpy as jnp
from jax import lax
from jax.experimental import pallas as pl
from jax.experimental.pallas import tpu as pltpu

F32, BF16 = jnp.float32, jnp.bfloat16

DEPTH = 4
H_A, D_HA = 4, 64
DA = 2 * D_HA
W_A = H_A * DA
H_B, DK_B, DV_B = 4, 128, 128
W_B = H_B * DV_B
QKV_B = 2 * H_B * DK_B + H_B * DV_B
GDN_CONV, GDN_CHUNK, SC_WIDTH = 4, 64, 3
NUM_BUCKETS, MAX_EXACT, MAX_DISTANCE = 32, 16, 128
EPS, NEG = 1e-6, -1e30
LOG2E = math.log2(math.e)
LANE = 128
P_MAIN = 4 * W_A + QKV_B + W_B
VMEM_LIMIT = 48 * 1024 * 1024


def _cparams(*sem):
    return pltpu.CompilerParams(dimension_semantics=sem, vmem_limit_bytes=VMEM_LIMIT)


def _silu(z):
    return z / (1.0 + jnp.exp(-z))


def _sigmoid(z):
    return 1.0 / (1.0 + jnp.exp(-z))


def _dot(a, b):
    return jnp.dot(a.astype(BF16), b.astype(BF16), preferred_element_type=F32)


def _dot_nt(a, b):
    return lax.dot_general(a.astype(BF16), b.astype(BF16), (((1,), (1,)), ((), ())), preferred_element_type=F32)


def _split(a):
    hi = a.astype(BF16)
    return hi, (a - hi.astype(F32)).astype(BF16)


def _rms_rows(x, w):
    return x * lax.rsqrt(jnp.mean(x * x, axis=-1, keepdims=True) + EPS) * w


def _norm_proj_kernel(x_ref, nw_ref, w_ref, *refs, with_gates):
    if with_gates:
        wab_ref, o_ref, ab_ref, h_ref = refs
    else:
        o_ref, h_ref = refs

    @pl.when(pl.program_id(1) == 0)
    def _():
        h = _rms_rows(x_ref[...], nw_ref[...]).astype(BF16)
        h_ref[...] = h
        if with_gates:
            ab_ref[...] = jnp.dot(h, wab_ref[...], preferred_element_type=F32)

    o_ref[...] = jnp.dot(h_ref[...], w_ref[...], preferred_element_type=F32).astype(BF16)


def _norm_proj(x, nw, w, w_ab, tm, tn):
    m, d = x.shape
    n = w.shape[1]
    with_gates = w_ab is not None
    in_specs = [pl.BlockSpec((tm, d), lambda i, j: (i, 0)),
                pl.BlockSpec((1, d), lambda i, j: (0, 0)),
                pl.BlockSpec((d, tn), lambda i, j: (0, j))]
    out_shape = [jax.ShapeDtypeStruct((m, n), BF16)]
    out_specs = [pl.BlockSpec((tm, tn), lambda i, j: (i, j))]
    args = [x, nw, w]
    if with_gates:
        in_specs.append(pl.BlockSpec((d, LANE), lambda i, j: (0, 0)))
        out_shape.append(jax.ShapeDtypeStruct((m, LANE), F32))
        out_specs.append(pl.BlockSpec((tm, LANE), lambda i, j: (i, 0)))
        args.append(w_ab)
    out = pl.pallas_call(
        functools.partial(_norm_proj_kernel, with_gates=with_gates),
        out_shape=tuple(out_shape),
        grid=(m // tm, n // tn),
        in_specs=in_specs,
        out_specs=tuple(out_specs),
        scratch_shapes=[pltpu.VMEM((tm, d), BF16)],
        compiler_params=_cparams("arbitrary", "arbitrary"),
        name="norm_proj",
    )(*args)
    return out if with_gates else out[0]


def _qkv_prep_kernel(q_ref, k_ref, v_ref, qw_ref, kw_ref, bd_ref, *refs, n_alias):
    qo_ref, kb_ref, kf_ref, vf_ref = refs[n_alias:]
    bd = bd_ref[...]

    def group_norm(x, w):
        hi, lo = _split(x * x)
        ms = jnp.dot(hi, bd, preferred_element_type=F32) + jnp.dot(lo, bd, preferred_element_type=F32)
        return x * lax.rsqrt(ms + EPS) * w

    qn = group_norm(q_ref[...].astype(F32), qw_ref[...])
    kn = group_norm(k_ref[...].astype(F32), kw_ref[...])
    qo_ref[...] = (qn * (D_HA ** -0.5 * LOG2E)).astype(BF16)
    kb_ref[...] = kn.astype(BF16)
    v = v_ref[...].astype(F32)
    tm = kn.shape[0]
    for h in range(H_A):
        kf_ref[pl.ds(h, tm, stride=H_A), :] = kn[:, h * DA:(h + 1) * DA]
        vf_ref[pl.ds(h, tm, stride=H_A), :] = v[:, h * DA:(h + 1) * DA]


def _qkv_prep(p, qw, kw, bd, tm, ei, n_layers, kv_prev):
    m = p.shape[0]
    col = lambda c: pl.BlockSpec((tm, W_A), lambda i, c=c: (i, c))
    full = lambda a: pl.BlockSpec(a.shape, lambda i: (0,) * a.ndim)
    row = pl.BlockSpec((tm, W_A), lambda i: (i, 0))
    leaf = pl.BlockSpec((None, tm * H_A, DA), lambda i: (ei, i, 0))
    leaf_shape = jax.ShapeDtypeStruct((n_layers, m * H_A, DA), F32)
    n_alias = 0 if kv_prev is None else 2
    return pl.pallas_call(
        functools.partial(_qkv_prep_kernel, n_alias=n_alias),
        out_shape=(jax.ShapeDtypeStruct((m, W_A), BF16), jax.ShapeDtypeStruct((m, W_A), BF16), leaf_shape, leaf_shape),
        grid=(m // tm,),
        in_specs=[col(0), col(1), col(2), full(qw), full(kw), full(bd)] + [pl.BlockSpec(memory_space=pl.ANY)] * n_alias,
        out_specs=(row, row, leaf, leaf),
        input_output_aliases={} if kv_prev is None else {6: 2, 7: 3},
        compiler_params=_cparams("arbitrary"),
        name="qkv_prep",
    )(p, p, p, qw, kw, bd, *(kv_prev or ()))


def _t5_bias(n, tab_ref, h):
    nf = jnp.maximum(n, 1).astype(F32)
    large = MAX_EXACT + (jnp.log(nf / MAX_EXACT) / math.log(MAX_DISTANCE / MAX_EXACT)
                         * (NUM_BUCKETS - MAX_EXACT)).astype(jnp.int32)
    large = jnp.minimum(large, NUM_BUCKETS - 1)
    bkt = jnp.where(n < MAX_EXACT, n, large)
    out = jnp.zeros(n.shape, F32)
    for b in range(NUM_BUCKETS):
        out = jnp.where(bkt == b, tab_ref[b, h], out)
    return (out - tab_ref[NUM_BUCKETS - 1, h]) * LOG2E


def _bias_tiles_kernel(tab_ref, o_ref, *, tq):
    h = pl.program_id(0)
    i = lax.broadcasted_iota(jnp.int32, (tq, tq), 0)
    j = lax.broadcasted_iota(jnp.int32, (tq, tq), 1)
    n0 = i - j
    o_ref[0, 0] = jnp.where(n0 >= 0, _t5_bias(jnp.maximum(n0, 0), tab_ref, h), NEG)
    o_ref[0, 1] = _t5_bias(n0 + tq, tab_ref, h)


def _bias_tiles(rel_table, tq):
    return pl.pallas_call(
        functools.partial(_bias_tiles_kernel, tq=tq),
        out_shape=jax.ShapeDtypeStruct((H_A, 2, tq, tq), F32),
        grid=(H_A,),
        in_specs=[pl.BlockSpec(memory_space=pltpu.SMEM)],
        out_specs=pl.BlockSpec((1, 2, tq, tq), lambda h: (h, 0, 0, 0)),
        compiler_params=_cparams("arbitrary"),
        name="bias_tiles",
    )(rel_table)


def _bias_decode_kernel(tab_ref, o_ref, *, page):
    w = page * H_A
    row = lax.broadcasted_iota(jnp.int32, (2 * H_A, w), 0)
    lane = lax.broadcasted_iota(jnp.int32, (2 * H_A, w), 1)
    valid = (lane & (H_A - 1)) == (row >> 1)
    n = page - (lane >> 2)
    near = jnp.zeros((2 * H_A, w), F32)
    new = jnp.zeros((2 * H_A, LANE), F32)
    for h in range(H_A):
        near = jnp.where((row >> 1) == h, _t5_bias(n, tab_ref, h), near)
        new = jnp.where((row[:, :LANE] >> 1) == h, _t5_bias(jnp.zeros((2 * H_A, LANE), jnp.int32), tab_ref, h), new)
    o_ref[:, 0:w] = jnp.where(valid, 0.0, NEG)
    o_ref[:, w:2 * w] = jnp.where(valid, near, NEG)
    o_ref[:, 2 * w:2 * w + LANE] = new


def _bias_decode(rel_table, page):
    assert H_A == 4
    return pl.pallas_call(
        functools.partial(_bias_decode_kernel, page=page),
        out_shape=jax.ShapeDtypeStruct((2 * H_A, 2 * page * H_A + LANE), F32),
        in_specs=[pl.BlockSpec(memory_space=pltpu.SMEM)],
        out_specs=pl.BlockSpec(memory_space=pltpu.VMEM),
        name="bias_decode",
    )(rel_table)


def _attn_finish(o, sw, z, out_scale):
    ms = jnp.mean(o * o, axis=-1, keepdims=True)
    return o * lax.rsqrt(ms + EPS) * sw * out_scale * _silu(z.astype(F32))


def _attn_prompt_kernel(lam_ref, q_ref, k_ref, v_ref, bias_ref, za_ref, sw_ref, o_ref, m, l, a, *, tq, out_scale):
    qi = pl.program_id(2)
    q = q_ref[...].astype(F32)
    lane = lax.broadcasted_iota(jnp.int32, q.shape, 1)
    q2 = jnp.concatenate([jnp.where(lane < D_HA, q, 0.0), jnp.where(lane >= D_HA, q, 0.0)], axis=0).astype(BF16)
    m[...] = jnp.full(m.shape, NEG, F32)
    l[...] = jnp.zeros(l.shape, F32)
    a[...] = jnp.zeros(a.shape, F32)
    reps = tq // LANE

    def step(j, bias):
        rows = pl.ds(pl.multiple_of(j * tq, tq), tq)
        s = lax.dot_general(q2, k_ref[rows, :], (((1,), (1,)), ((), ())), preferred_element_type=F32)
        if bias is not None:
            s = s + jnp.concatenate([bias, bias], axis=0)
        m_prev = m[...]
        m_new = jnp.maximum(m_prev, jnp.max(s, axis=1, keepdims=True))
        p = jnp.exp2(s - jnp.concatenate([m_new] * reps, axis=1))
        alpha = jnp.exp2(m_prev - m_new)
        l[...] = alpha * l[...] + jnp.sum(p, axis=1, keepdims=True)
        a[...] = alpha * a[...] + jnp.dot(p.astype(BF16), v_ref[rows, :], preferred_element_type=F32)
        m[...] = m_new

    def far(j, c):
        step(j, None)
        return c

    lax.fori_loop(0, jnp.maximum(qi - 1, 0), far, 0)

    @pl.when(qi >= 1)
    def _():
        step(qi - 1, bias_ref[0, 1])

    step(qi, bias_ref[0, 0])
    o = a[...] / l[...]
    o = o[:tq] - lam_ref[0] * o[tq:]
    o_ref[...] = _attn_finish(o, sw_ref[...], za_ref[...], out_scale).astype(BF16)


def _attn_prompt(lam, q, k, bias, p, sw, nb, t, tq, out_scale):
    m = q.shape[0]
    nq = t // tq
    za0 = (3 * W_A) // DA
    v0 = (2 * W_A) // DA
    return pl.pallas_call(
        functools.partial(_attn_prompt_kernel, tq=tq, out_scale=out_scale),
        out_shape=jax.ShapeDtypeStruct((m, W_A), BF16),
        grid=(nb, H_A, nq),
        in_specs=[pl.BlockSpec(memory_space=pltpu.SMEM),
                  pl.BlockSpec((tq, DA), lambda b, h, i: (b * nq + i, h)),
                  pl.BlockSpec((t, DA), lambda b, h, i: (b, h)),
                  pl.BlockSpec((t, DA), lambda b, h, i: (b, v0 + h)),
                  pl.BlockSpec((1, 2, tq, tq), lambda b, h, i: (h, 0, 0, 0)),
                  pl.BlockSpec((tq, DA), lambda b, h, i: (b * nq + i, za0 + h)),
                  pl.BlockSpec((1, DA), lambda b, h, i: (0, 0))],
        out_specs=pl.BlockSpec((tq, DA), lambda b, h, i: (b * nq + i, h)),
        scratch_shapes=[pltpu.VMEM((2 * tq, LANE), F32), pltpu.VMEM((2 * tq, LANE), F32),
                        pltpu.VMEM((2 * tq, DA), F32)],
        compiler_params=_cparams("arbitrary", "arbitrary", "arbitrary"),
        name="attn_prompt",
    )(lam, q, k, p, bias, p, sw)


def _attn_decode_kernel(pt_ref, lam_ref, q_ref, kn_ref, vn_ref, *refs, page, npg, out_scale):
    ck_refs, cv_refs = refs[:npg], refs[npg:2 * npg]
    bias_ref, za_ref, sw_ref, o_ref = refs[2 * npg:]
    rows, w = 2 * H_A, page * H_A
    row = lax.broadcasted_iota(jnp.int32, (rows, DA), 0)
    lane = lax.broadcasted_iota(jnp.int32, (rows, DA), 1)

    def head_rows(x):
        out = jnp.zeros((rows, DA), F32)
        for h in range(H_A):
            out = jnp.where((row >> 1) == h, jnp.broadcast_to(x[:, h * DA:(h + 1) * DA], (rows, DA)), out)
        return out

    q8 = jnp.where((lane >> 6) == (row & 1), head_rows(q_ref[...].astype(F32)), 0.0)
    q8b = q8.astype(BF16)
    s_all = []
    for j in range(npg):
        s = lax.dot_general(q8b, ck_refs[j][...].astype(BF16), (((1,), (1,)), ((), ())), preferred_element_type=F32)
        s_all.append(s + (bias_ref[:, w:2 * w] if j == npg - 1 else bias_ref[:, 0:w]))
    s_new = (jnp.sum(q8 * head_rows(kn_ref[...].astype(F32)), axis=1, keepdims=True)
             + bias_ref[:, 2 * w:2 * w + 1])
    m = s_all[0]
    for s in s_all[1:]:
        m = jnp.maximum(m, s)
    m = jnp.maximum(jnp.max(m, axis=1, keepdims=True), s_new)
    p_new = jnp.exp2(s_new - m)
    l = p_new
    acc = p_new * head_rows(vn_ref[...].astype(F32))
    for j in range(npg):
        p = jnp.exp2(s_all[j] - m)
        l = l + jnp.sum(p, axis=1, keepdims=True)
        acc = acc + jnp.dot(p.astype(BF16), cv_refs[j][...].astype(BF16), preferred_element_type=F32)
    o = acc / l
    sw = sw_ref[...]
    z = za_ref[...]
    outs = []
    for h in range(H_A):
        oh = o[2 * h:2 * h + 1, :] - lam_ref[0] * o[2 * h + 1:2 * h + 2, :]
        outs.append(_attn_finish(oh, sw, z[:, h * DA:(h + 1) * DA], out_scale))
    o_ref[...] = jnp.concatenate(outs, axis=1).astype(BF16)


def _attn_decode(page_table, lam, q, kn, cache_k, cache_v, ei, bias, p, sw, out_scale):
    nb, npg = page_table.shape
    page = cache_k.shape[2]
    ck = cache_k.reshape(cache_k.shape[0], cache_k.shape[1], page * H_A, DA)
    cv = cache_v.reshape(cache_v.shape[0], cache_v.shape[1], page * H_A, DA)
    r3 = lambda a: a.reshape(nb, 1, a.shape[-1])
    vec = lambda c: pl.BlockSpec((None, 1, W_A), lambda b, pt, c=c: (b, 0, c))
    cache = [pl.BlockSpec((None, None, page * H_A, DA), lambda b, pt, j=j: (ei, pt[b * npg + j], 0, 0))
             for j in range(npg)]
    out = pl.pallas_call(
        functools.partial(_attn_decode_kernel, page=page, npg=npg, out_scale=out_scale),
        out_shape=jax.ShapeDtypeStruct((nb, 1, W_A), BF16),
        grid_spec=pltpu.PrefetchScalarGridSpec(
            num_scalar_prefetch=1,
            grid=(nb,),
            in_specs=[pl.BlockSpec(memory_space=pltpu.SMEM), vec(0), vec(0), vec(2)] + cache + cache
                     + [pl.BlockSpec(bias.shape, lambda b, pt: (0, 0)),
                        vec(3),
                        pl.BlockSpec((1, DA), lambda b, pt: (0, 0))],
            out_specs=pl.BlockSpec((None, 1, W_A), lambda b, pt: (b, 0, 0))),
        compiler_params=_cparams("arbitrary"),
        name="attn_decode",
    )(page_table.reshape(-1), lam, r3(q), r3(kn), r3(p), *([ck] * npg), *([cv] * npg), bias, r3(p), sw)
    return out.reshape(nb, W_A)


def _shifted_conv(x, prev, w):
    taps = w.shape[0]
    r8 = lax.broadcasted_iota(jnp.int32, (8, x.shape[1]), 0)
    y = x * w[taps - 1:taps, :]
    xt = x[:8]
    yt = xt * w[taps - 1:taps, :]
    for s in range(1, taps):
        ws = w[taps - 1 - s:taps - s, :]
        y = y + pltpu.roll(x, s, 0) * ws
        yt = yt + jnp.where(r8 < s, pltpu.roll(prev, s, 0), pltpu.roll(xt, s, 0)) * ws
    return jnp.concatenate([yt, y[8:]], axis=0)


def _gates(ab, alog, dtb):
    x = ab + dtb
    sp = jnp.maximum(x, 0.0) + jnp.log(1.0 + jnp.exp(-jnp.abs(x)))
    return -jnp.exp(alog) * sp, _sigmoid(ab)


def _l2norm(x):
    return x * lax.rsqrt(jnp.sum(x * x, axis=-1, keepdims=True) + EPS)


def _rms(x, w):
    return x * lax.rsqrt(jnp.mean(x * x, axis=-1, keepdims=True) + EPS) * w


def _level_masks(c):
    i = lax.broadcasted_iota(jnp.int32, (c, c), 0)
    j = lax.broadcasted_iota(jnp.int32, (c, c), 1)
    masks = []
    s = 1
    while s < c:
        sh = s.bit_length() - 1
        masks.append((((i >> (sh + 1)) == (j >> (sh + 1))) & ((i >> sh) != (j >> sh)) & (i > j)).astype(F32))
        s *= 2
    return masks


def _gdn_prompt_kernel(pq_ref, pk_ref, pv_ref, zb_ref, ab_ref, cw_ref, alog_ref, dtb_ref, gnw_ref,
                       ob_ref, s_ref, carry_ref, cb_ref, g_ref, beta_ref, u_ref, wq_ref, ak_ref, gl_ref,
                       *, rows, chunk):
    t = pl.program_id(1)
    c = chunk

    @pl.when(t == 0)
    def _():
        carry_ref[...] = jnp.zeros(carry_ref.shape, F32)
        s_ref[...] = jnp.zeros(s_ref.shape, F32)

    cw = cw_ref[...]
    for seg, ref in enumerate((pq_ref, pk_ref, pv_ref)):
        cols = slice(seg * W_B, (seg + 1) * W_B)
        x = ref[...].astype(F32)
        cb_ref[:, cols] = _silu(_shifted_conv(x, carry_ref[:, cols], cw[:, cols]))
        carry_ref[:, cols] = x[rows - 8:rows]
    g, beta = _gates(ab_ref[...], alog_ref[...], dtb_ref[...])
    g_ref[...] = g
    beta_ref[...] = beta

    ii = lax.broadcasted_iota(jnp.int32, (c, c), 0)
    jj = lax.broadcasted_iota(jnp.int32, (c, c), 1)
    incl = ii >= jj
    ltri = incl.astype(BF16)
    masks = _level_masks(c)

    def prep(ci):
        rs = pl.ds(pl.multiple_of(ci * c, c), c)
        gch = g_ref[rs, :]
        g1 = gch.astype(BF16)
        r1 = gch - g1.astype(F32)
        g2 = r1.astype(BF16)
        g3 = (r1 - g2.astype(F32)).astype(BF16)
        d = functools.partial(jnp.dot, preferred_element_type=F32)
        gc = d(ltri, g1) + d(ltri, g2) + d(ltri, g3)
        gct = jnp.concatenate([gc, gc], axis=0).T
        ge = jnp.exp(gc)
        kdec = jnp.exp(gc[c - 1:c, :] - gc)
        gl_ref[ci] = ge[c - 1:c, :]
        return rs, gc, gct, ge, kdec, beta_ref[rs, :]

    def local(i2, carry):
        items = []
        for ci in (2 * i2, 2 * i2 + 1):
            rs, gc, gct, ge, kdec, bch = prep(ci)
            for h in range(H_B):
                hc = slice(h * DK_B, (h + 1) * DK_B)
                qh = _l2norm(cb_ref[rs, hc]) * (DK_B ** -0.5)
                kh = _l2norm(cb_ref[rs, slice(W_B + h * DK_B, W_B + (h + 1) * DK_B)])
                vh = cb_ref[rs, slice(2 * W_B + h * DV_B, 2 * W_B + (h + 1) * DV_B)]
                bcol = bch[:, H_B + h:H_B + h + 1]
                gecol = ge[:, h:h + 1]
                decay = jnp.exp(jnp.where(incl, gc[:, h:h + 1] - gct[h:h + 1, :c], NEG))
                kbeta = kh * bcol
                wq_ref[ci, h, c:2 * c] = (qh * gecol).astype(BF16)
                ak_ref[ci, h, c:c + DK_B] = (kh * kdec[:, h:h + 1]).T.astype(BF16)
                items.append(dict(ci=ci, h=h, rs=rs, hc=hc, decay=decay, qk=(kbeta, qh, kh),
                                  rhs=jnp.concatenate([vh * bcol, kbeta * gecol], axis=1)))
        for it in items:
            kbeta, qh, kh = it.pop("qk")
            both = _dot_nt(jnp.concatenate([kbeta, qh], axis=0), kh)
            it["mm"] = jnp.where(ii > jj, both[:c] * it["decay"], 0.0)
            ak_ref[it["ci"], it["h"], 0:c] = (both[c:] * it["decay"]).astype(BF16)
        for it in items:
            it["pp"] = -(it["mm"] * masks[0])
        for mk in masks[1:]:
            for it in items:
                e = it["mm"] * mk
                it["x"] = e + _dot(it["pp"], e)
            for it in items:
                it["pp"] = it["pp"] - (it["x"] + _dot(it["x"], it["pp"]))
        for it in items:
            uw = it["rhs"] + _dot(it["pp"], it["rhs"])
            u_ref[it["rs"], it["hc"]] = uw[:, :DV_B]
            wq_ref[it["ci"], it["h"], 0:c] = uw[:, DV_B:].astype(BF16)
        return carry

    lax.fori_loop(0, rows // (2 * c), local, 0)
    gnw = gnw_ref[...]

    def scan(ci, carry):
        rs = pl.ds(pl.multiple_of(ci * c, c), c)
        gl = gl_ref[ci]
        heads = range(H_B)
        hcs = [slice(h * DK_B, (h + 1) * DK_B) for h in heads]
        s_old = [s_ref[h] for h in heads]
        r = [jnp.dot(wq_ref[ci, h], s_old[h].astype(BF16), preferred_element_type=F32) for h in heads]
        v_new = [u_ref[rs, hcs[h]] - r[h][:c] for h in heads]
        r2 = [jnp.dot(ak_ref[ci, h], v_new[h].astype(BF16), preferred_element_type=F32) for h in heads]
        for h in heads:
            s_ref[h] = s_old[h] * gl[:, h:h + 1] + r2[h][c:]
            ob_ref[rs, hcs[h]] = (_rms(r[h][c:] + r2[h][:c], gnw)
                                  * _silu(zb_ref[rs, hcs[h]].astype(F32))).astype(BF16)
        return carry

    lax.fori_loop(0, rows // c, scan, 0)


def _gdn_prompt(p, ab, cw, alog, dtb, gnw, nb, t, rows):
    m = p.shape[0]
    nt = t // rows
    c = math.gcd(GDN_CHUNK, t)
    col = lambda cidx: pl.BlockSpec((rows, W_B), lambda b, i, cidx=cidx: (b * nt + i, cidx))
    full = lambda a: pl.BlockSpec(a.shape, lambda b, i: (0,) * a.ndim)
    c0 = (4 * W_A) // W_B
    return pl.pallas_call(
        functools.partial(_gdn_prompt_kernel, rows=rows, chunk=c),
        out_shape=(jax.ShapeDtypeStruct((m, W_B), BF16), jax.ShapeDtypeStruct((nb, H_B, DK_B, DV_B), F32)),
        grid=(nb, nt),
        in_specs=[col(c0), col(c0 + 1), col(c0 + 2), col(c0 + 3),
                  pl.BlockSpec((rows, LANE), lambda b, i: (b * nt + i, 0)),
                  full(cw), full(alog), full(dtb), full(gnw)],
        out_specs=(pl.BlockSpec((rows, W_B), lambda b, i: (b * nt + i, 0)),
                   pl.BlockSpec((None, H_B, DK_B, DV_B), lambda b, i: (b, 0, 0, 0))),
        scratch_shapes=[pltpu.VMEM((8, QKV_B), F32), pltpu.VMEM((rows, QKV_B), F32),
                        pltpu.VMEM((rows, LANE), F32), pltpu.VMEM((rows, LANE), F32),
                        pltpu.VMEM((rows, W_B), F32),
                        pltpu.VMEM((rows // c, H_B, 2 * c, DK_B), BF16),
                        pltpu.VMEM((rows // c, H_B, c + DK_B, c), BF16),
                        pltpu.VMEM((rows // c, 1, LANE), F32)],
        compiler_params=_cparams("arbitrary", "arbitrary"),
        name="gdn_prompt",
    )(p, p, p, p, ab, cw, alog, dtb, gnw)


def _gdn_decode_prep_kernel(pq_ref, pk_ref, pv_ref, ab_ref, c0_ref, cw_ref, alog_ref, dtb_ref,
                            q_ref, k_ref, v_ref, eg_ref, beta_ref, qk_ref, cn_ref):
    cw = cw_ref[...]
    taps = cw.shape[0]
    outs = (q_ref, k_ref, v_ref)
    for seg, ref in enumerate((pq_ref, pk_ref, pv_ref)):
        cols = slice(seg * W_B, (seg + 1) * W_B)
        x = ref[...].astype(F32)
        y = x * cw[taps - 1:taps, cols]
        for j in range(taps - 1):
            y = y + c0_ref[j, :, cols] * cw[j:j + 1, cols]
            if j >= 1:
                cn_ref[j - 1, :, cols] = c0_ref[j, :, cols]
        cn_ref[taps - 2, :, cols] = x
        outs[seg][...] = _silu(y)
    lane = lax.broadcasted_iota(jnp.int32, eg_ref.shape, 1)
    qk = jnp.zeros(eg_ref.shape, F32)
    for h in range(H_B):
        hc = slice(h * DK_B, (h + 1) * DK_B)
        qh = _l2norm(q_ref[:, hc]) * (DK_B ** -0.5)
        kh = _l2norm(k_ref[:, hc])
        q_ref[:, hc] = qh
        k_ref[:, hc] = kh
        qk = jnp.where(lane == h, jnp.sum(qh * kh, axis=-1, keepdims=True), qk)
    g, beta = _gates(ab_ref[...], alog_ref[...], dtb_ref[...])
    eg_ref[...] = jnp.exp(g)
    beta_ref[...] = beta
    qk_ref[...] = qk


def _gdn_decode_prep(p, ab, conv0_t, cw, alog, dtb):
    nb = p.shape[0]
    c0 = (4 * W_A) // W_B
    col = lambda cidx: pl.BlockSpec((nb, W_B), lambda i, cidx=cidx: (0, cidx))
    full = lambda a: pl.BlockSpec(a.shape, lambda i: (0,) * a.ndim)
    wide = jax.ShapeDtypeStruct((nb, W_B), F32)
    narrow = jax.ShapeDtypeStruct((nb, LANE), F32)
    ospec = lambda s: pl.BlockSpec(s.shape, lambda i: (0,) * len(s.shape))
    outs = (wide, wide, wide, narrow, narrow, narrow, jax.ShapeDtypeStruct(conv0_t.shape, F32))
    return pl.pallas_call(
        _gdn_decode_prep_kernel,
        out_shape=outs,
        grid=(1,),
        in_specs=[col(c0), col(c0 + 1), col(c0 + 2),
                  full(ab), full(conv0_t), full(cw), full(alog), full(dtb)],
        out_specs=tuple(ospec(s) for s in outs),
        compiler_params=_cparams("arbitrary"),
        name="gdn_decode_prep",
    )(p, p, p, ab, conv0_t, cw, alog, dtb)


def _gdn_decode_kernel(s_ref, qt_ref, kt_ref, v_ref, eg_ref, beta_ref, qk_ref, zb_ref, gnw_ref, *refs, bb, n_alias):
    so_ref, ob_ref, o_scr = refs[n_alias:]
    v = v_ref[...]
    eg = eg_ref[...]
    beta = beta_ref[...]
    qk = qk_ref[...]
    for h in range(H_B):
        hc = slice(h * DV_B, (h + 1) * DV_B)
        qt = qt_ref[h]
        kt = kt_ref[h]
        for i in range(bb):
            s_old = s_ref[i, h]
            kc = kt[:, i:i + 1]
            qc = qt[:, i:i + 1]
            egs = eg[i:i + 1, h:h + 1]
            ks = jnp.sum(s_old * kc, axis=0, keepdims=True)
            qs = jnp.sum(s_old * qc, axis=0, keepdims=True)
            v_new = beta[i:i + 1, H_B + h:H_B + h + 1] * (v[i:i + 1, hc] - egs * ks)
            o_scr[i:i + 1, hc] = egs * qs + qk[i:i + 1, h:h + 1] * v_new
            so_ref[i, h] = s_old * egs + kc * v_new
    gnw = gnw_ref[...]
    z = zb_ref[...]
    outs = [_rms(o_scr[:, h * DV_B:(h + 1) * DV_B], gnw) * _silu(z[:, h * DV_B:(h + 1) * DV_B].astype(F32))
            for h in range(H_B)]
    ob_ref[...] = jnp.concatenate(outs, axis=1).astype(BF16)


def _gdn_decode(state, ei, qt, kt, v, eg, beta, qk, p, gnw, bb, s_prev):
    nb = v.shape[0]
    ns = nb // bb
    narrow = pl.BlockSpec((bb, LANE), lambda i: (i, 0))
    tr = pl.BlockSpec((None, H_B, DK_B, bb), lambda i: (i, 0, 0, 0))
    zb0 = (4 * W_A + QKV_B) // W_B
    n_alias = 0 if s_prev is None else 1
    return pl.pallas_call(
        functools.partial(_gdn_decode_kernel, bb=bb, n_alias=n_alias),
        out_shape=(jax.ShapeDtypeStruct(state.shape, F32), jax.ShapeDtypeStruct((nb, W_B), BF16)),
        grid=(ns,),
        in_specs=[pl.BlockSpec((None, bb, H_B, DK_B, DV_B), lambda i: (ei, i, 0, 0, 0)),
                  tr, tr,
                  pl.BlockSpec((bb, W_B), lambda i: (i, 0)),
                  narrow, narrow, narrow,
                  pl.BlockSpec((bb, W_B), lambda i: (i, zb0)),
                  pl.BlockSpec((1, DV_B), lambda i: (0, 0))] + [pl.BlockSpec(memory_space=pl.ANY)] * n_alias,
        out_specs=(pl.BlockSpec((None, bb, H_B, DK_B, DV_B), lambda i: (ei, i, 0, 0, 0)),
                   pl.BlockSpec((bb, W_B), lambda i: (i, 0))),
        input_output_aliases={9: 0} if n_alias else {},
        scratch_shapes=[pltpu.VMEM((bb, W_B), F32)],
        compiler_params=_cparams("arbitrary"),
        name="gdn_decode",
    )(state, qt, kt, v, eg, beta, qk, p, gnw, *(() if s_prev is None else (s_prev,)))


def _out_proj_kernel(x_ref, oa_ref, ob_ref, w_ref, y_ref):
    y_ref[...] = (x_ref[...]
                  + jnp.dot(oa_ref[...], w_ref[0:W_A, :], preferred_element_type=F32)
                  + jnp.dot(ob_ref[...], w_ref[W_A:W_A + W_B, :], preferred_element_type=F32))


def _out_proj(x, oa, ob, w, tm):
    m, d = x.shape
    return pl.pallas_call(
        _out_proj_kernel,
        out_shape=jax.ShapeDtypeStruct((m, d), F32),
        grid=(m // tm,),
        in_specs=[pl.BlockSpec((tm, d), lambda i: (i, 0)),
                  pl.BlockSpec((tm, W_A), lambda i: (i, 0)),
                  pl.BlockSpec((tm, W_B), lambda i: (i, 0)),
                  pl.BlockSpec(w.shape, lambda i: (0, 0))],
        out_specs=pl.BlockSpec((tm, d), lambda i: (i, 0)),
        compiler_params=_cparams("arbitrary"),
        name="out_proj",
    )(x, oa, ob, w)


def _odd_prompt_kernel(x_ref, nw_ref, wi_ref, cw_ref, wo_ref, y_ref, sc_ref, carry_ref, *, rows):
    t = pl.program_id(1)
    d = x_ref.shape[1]

    @pl.when(t == 0)
    def _():
        carry_ref[...] = jnp.zeros(carry_ref.shape, F32)

    x = x_ref[...]
    h = _rms_rows(x, nw_ref[...]).astype(BF16)
    proj = lambda c: jnp.dot(h, wi_ref[:, c * d:(c + 1) * d], preferred_element_type=F32)
    u = proj(1) * proj(2)
    cv = _shifted_conv(u, carry_ref[...], cw_ref[...])
    carry_ref[...] = u[rows - 8:rows]
    g = proj(0) * cv * _silu(proj(3))
    y_ref[...] = x + jnp.dot(g.astype(BF16), wo_ref[...], preferred_element_type=F32)

    @pl.when(t == pl.num_programs(1) - 1)
    def _():
        sc_ref[...] = u[rows - 8:rows]


def _odd_prompt(x, nw, w_in, cw, w_out, nb, t, rows):
    m, d = x.shape
    nt = t // rows
    full = lambda a: pl.BlockSpec(a.shape, lambda b, i: (0,) * a.ndim)
    return pl.pallas_call(
        functools.partial(_odd_prompt_kernel, rows=rows),
        out_shape=(jax.ShapeDtypeStruct((m, d), F32), jax.ShapeDtypeStruct((nb, 8, d), F32)),
        grid=(nb, nt),
        in_specs=[pl.BlockSpec((rows, d), lambda b, i: (b * nt + i, 0)),
                  full(nw), full(w_in), full(cw), full(w_out)],
        out_specs=(pl.BlockSpec((rows, d), lambda b, i: (b * nt + i, 0)),
                   pl.BlockSpec((None, 8, d), lambda b, i: (b, 0, 0))),
        scratch_shapes=[pltpu.VMEM((8, d), F32)],
        compiler_params=_cparams("arbitrary", "arbitrary"),
        name="odd_prompt",
    )(x, nw, w_in, cw, w_out)


def _odd_decode_kernel(x_ref, bg_ref, cg_ref, hh_ref, z_ref, b0_ref, b1_ref, cw_ref, w_ref, y_ref, u_ref):
    cw = cw_ref[...]
    u = cg_ref[...].astype(F32) * hh_ref[...].astype(F32)
    cv = b0_ref[...] * cw[0:1, :] + b1_ref[...] * cw[1:2, :] + u * cw[2:3, :]
    g = bg_ref[...].astype(F32) * cv * _silu(z_ref[...].astype(F32))
    y_ref[...] = x_ref[...] + jnp.dot(g.astype(BF16), w_ref[...], preferred_element_type=F32)
    u_ref[...] = u


def _odd_decode(x, p, b0, b1, cw, w):
    m, d = x.shape
    col = lambda c: pl.BlockSpec((m, d), lambda i, c=c: (0, c))
    full = lambda a: pl.BlockSpec(a.shape, lambda i: (0,) * a.ndim)
    return pl.pallas_call(
        _odd_decode_kernel,
        out_shape=(jax.ShapeDtypeStruct((m, d), F32), jax.ShapeDtypeStruct((m, d), F32)),
        grid=(1,),
        in_specs=[full(x), col(0), col(1), col(2), col(3), full(b0), full(b1), full(cw), full(w)],
        out_specs=(pl.BlockSpec((m, d), lambda i: (0, 0)), pl.BlockSpec((m, d), lambda i: (0, 0))),
        compiler_params=_cparams("arbitrary"),
        name="odd_decode",
    )(x, p, p, p, p, b0, b1, cw, w)


def _tile(n, want):
    t = math.gcd(n, want)
    assert t == n or t % 8 == 0, (n, want)
    return t


def _pad_lanes(v):
    return jnp.pad(v.astype(F32), (0, LANE - v.shape[0])).reshape(1, LANE)


def kernel(x_prompt, x_sample, cache_k, cache_v, page_table, state_gdn, state_gdn_conv, state_shortconv, norm_w, rel_table, w_in_even, w_out_even, qn_w, kn_w, lam_q1, lam_k1, lam_q2, lam_k2, subln_w, gdn_conv_w, gdn_a_log, gdn_dt_bias, gdn_norm_w, w_in_odd, sc_conv_w, w_out_odd):
    nbp, t, d = x_prompt.shape
    nbs = x_sample.shape[0]
    page = cache_k.shape[2]
    assert x_sample.shape[1] == 1 and page >= MAX_DISTANCE and t % 8 == 0
    tq = _tile(t, 512)
    assert tq >= MAX_DISTANCE

    xp = x_prompt.reshape(nbp * t, d)
    xs = x_sample.reshape(nbs, d)
    tm_p = _tile(nbp * t, 1024)

    g64 = jnp.arange(W_A) // D_HA
    bd = jnp.where(g64[:, None] == g64[None, :], 1.0 / D_HA, 0.0).astype(BF16)
    bias_p = _bias_tiles(rel_table.astype(F32), tq)
    bias_s = _bias_decode(rel_table.astype(F32), page)

    n_even = (DEPTH + 1) // 2
    kv_p = kv_s = s_s = None
    sp, gcp, scp = [], [], []
    gcs, scs = [], []
    ei = oi = 0
    for li in range(DEPTH):
        nw = norm_w[li].reshape(1, d)
        if li % 2 == 0:
            lambda_init = 0.8 - 0.6 * math.exp(-0.3 * li)
            w_in = w_in_even[ei][:, :P_MAIN].astype(BF16)
            w_ab = jnp.pad(w_in_even[ei][:, P_MAIN:], ((0, 0), (0, LANE - 2 * H_B))).astype(BF16)
            w_out = w_out_even[ei].astype(BF16)
            qw = jnp.tile(qn_w[ei], W_A // D_HA).reshape(1, W_A)
            kw = jnp.tile(kn_w[ei], W_A // D_HA).reshape(1, W_A)
            sw = subln_w[ei].reshape(1, DA)
            lam = (jnp.exp(jnp.sum(lam_q1[ei] * lam_k1[ei]).astype(F32))
                   - jnp.exp(jnp.sum(lam_q2[ei] * lam_k2[ei]).astype(F32)) + lambda_init).reshape(1)
            cw = gdn_conv_w[ei]
            alog = _pad_lanes(gdn_a_log[ei])
            dtb = _pad_lanes(gdn_dt_bias[ei])
            gnw = gdn_norm_w[ei].reshape(1, DV_B)
            tn = 1024

            p, ab = _norm_proj(xp, nw, w_in, w_ab, tm_p, tn)
            qb, kb, *kv_p = _qkv_prep(p, qw, kw, bd, tm_p, ei, n_even, kv_p)
            oa = _attn_prompt(lam, qb, kb, bias_p, p, sw, nbp, t, tq, 1.0 - lambda_init)
            ob, s_new = _gdn_prompt(p, ab, cw, alog, dtb, gnw, nbp, t, _tile(t, 512))
            xp = _out_proj(xp, oa, ob, w_out, tm_p)
            sp.append(s_new)
            c0 = 4 * W_A
            gcp.append(p.reshape(nbp, t, -1)[:, t - (GDN_CONV - 1):, c0:c0 + QKV_B].astype(F32))

            p, ab = _norm_proj(xs, nw, w_in, w_ab, nbs, tn)
            qb, kb, *kv_s = _qkv_prep(p, qw, kw, bd, nbs, ei, n_even, kv_s)
            oa = _attn_decode(page_table, lam, qb, kb, cache_k, cache_v, ei, bias_s, p, sw, 1.0 - lambda_init)
            conv0_t = jnp.swapaxes(state_gdn_conv[ei], 0, 1)
            qn, kn, vv, eg, beta, qk, conv_new = _gdn_decode_prep(p, ab, conv0_t, cw, alog, dtb)
            bb = 8
            to_cols = lambda a: a.reshape(nbs // bb, bb, H_B, DK_B).transpose(0, 2, 3, 1)
            s_s, ob = _gdn_decode(state_gdn, ei, to_cols(qn), to_cols(kn), vv, eg, beta, qk, p, gnw, bb, s_s)
            xs = _out_proj(xs, oa, ob, w_out, nbs)
            gcs.append(jnp.swapaxes(conv_new, 0, 1))
            ei += 1
        else:
            w_in = w_in_odd[oi].astype(BF16)
            w_out = w_out_odd[oi].astype(BF16)
            cw = sc_conv_w[oi]
            xp, tail = _odd_prompt(xp, nw, w_in, cw, w_out, nbp, t, _tile(t, 512))
            scp.append(tail[:, 8 - (SC_WIDTH - 1):, :])

            p = _norm_proj(xs, nw, w_in, None, nbs, 1024)
            buf0 = state_shortconv[oi]
            xs, u = _odd_decode(xs, p, buf0[:, 0, :], buf0[:, 1, :], cw, w_out)
            scs.append(jnp.stack([buf0[:, 1, :], u], axis=1))
            oi += 1

    leaf_p = lambda a: a.reshape(n_even, nbp, t, H_A, DA)
    leaf_s = lambda a: a.reshape(n_even, nbs, 1, H_A, DA)
    return (xp.reshape(nbp, t, d), xs.reshape(nbs, 1, d),
            leaf_p(kv_p[0]), leaf_p(kv_p[1]), jnp.stack(sp), jnp.stack(gcp), jnp.stack(scp),
            leaf_s(kv_s[0]), leaf_s(kv_s[1]), s_s, jnp.stack(gcs), jnp.stack(scs))
```

```python
import functools
import math

import jax
import jax.numpy as jnp
from jax import lax
from jax.experimental import pallas as pl
from jax.experimental.pallas import tpu as pltpu

F32, BF16 = jnp.float32, jnp.bfloat16

DEPTH = 4
H_A, D_HA = 4, 64
DA = 2 * D_HA
W_A = H_A * DA
H_B, DK_B, DV_B = 4, 128, 128
W_B = H_B * DV_B
QKV_B = 2 * H_B * DK_B + H_B * DV_B
GDN_CONV, GDN_CHUNK, SC_WIDTH = 4, 64, 3
NUM_BUCKETS, MAX_EXACT, MAX_DISTANCE = 32, 16, 128
EPS, NEG = 1e-6, -1e30
LOG2E = math.log2(math.e)
LANE = 128
P_MAIN = 4 * W_A + QKV_B + W_B
VMEM_LIMIT = 48 * 1024 * 1024
LOCAL_GROUP = 4


def _cparams(*sem):
    return pltpu.CompilerParams(dimension_semantics=sem, vmem_limit_bytes=VMEM_LIMIT)


def _silu(z):
    h = 0.5 * z
    return h + h * jnp.tanh(h)


def _sigmoid(z):
    return 1.0 / (1.0 + jnp.exp(-z))


def _dot(a, b):
    return jnp.dot(a.astype(BF16), b.astype(BF16), preferred_element_type=F32)


def _dot_nt(a, b):
    return lax.dot_general(a.astype(BF16), b.astype(BF16), (((1,), (1,)), ((), ())), preferred_element_type=F32)


def _rms_rows(x, w):
    return x * lax.rsqrt(jnp.mean(x * x, axis=-1, keepdims=True) + EPS) * w


def _norm_proj_kernel(x_ref, nw_ref, w_ref, o_ref, h_ref):
    @pl.when(pl.program_id(1) == 0)
    def _():
        h_ref[...] = _rms_rows(x_ref[...], nw_ref[...]).astype(BF16)

    o_ref[...] = jnp.dot(h_ref[...], w_ref[...], preferred_element_type=F32).astype(BF16)


def _norm_proj(x, nw, w, tm, tn):
    m, d = x.shape
    n = w.shape[1]
    return pl.pallas_call(
        _norm_proj_kernel,
        out_shape=jax.ShapeDtypeStruct((m, n), BF16),
        grid=(m // tm, n // tn),
        in_specs=[pl.BlockSpec((tm, d), lambda i, j: (i, 0)),
                  pl.BlockSpec((1, d), lambda i, j: (0, 0)),
                  pl.BlockSpec((d, tn), lambda i, j: (0, j))],
        out_specs=pl.BlockSpec((tm, tn), lambda i, j: (i, j)),
        scratch_shapes=[pltpu.VMEM((tm, d), BF16)],
        compiler_params=_cparams("arbitrary", "arbitrary"),
        name="norm_proj",
    )(x, nw, w)


def _even_in_kernel(x_ref, nw_ref, w_ref, wab_ref, qw_ref, kw_ref, bd_ref, *refs, n_alias):
    qo_ref, kb_ref, kf_ref, vf_ref, p_ref, ab_ref = refs[n_alias:]
    tm = x_ref.shape[0]
    h = _rms_rows(x_ref[...], nw_ref[...]).astype(BF16)
    proj = lambda c: jnp.dot(h, w_ref[:, c * W_A:(c + 1) * W_A], preferred_element_type=F32)
    ab_ref[...] = jnp.dot(h, wab_ref[...], preferred_element_type=F32)
    bd = bd_ref[...]

    def group_norm(x, w):
        ms = jnp.dot((x * x).astype(BF16), bd, preferred_element_type=F32)
        return x * lax.rsqrt(ms + EPS) * w

    def leaf(ref, val):
        for hd in range(H_A):
            ref[pl.ds(hd, tm, stride=H_A), :] = val[:, hd * DA:(hd + 1) * DA]

    qn = group_norm(proj(0), qw_ref[...])
    qo_ref[...] = (qn * (D_HA ** -0.5 * LOG2E)).astype(BF16)
    kn = group_norm(proj(1), kw_ref[...])
    kb_ref[...] = kn.astype(BF16)
    leaf(kf_ref, kn)
    v = proj(2)
    leaf(vf_ref, v)
    p_ref[:, 0:W_A] = v.astype(BF16)
    for c in range(3, P_MAIN // W_A):
        p_ref[:, (c - 2) * W_A:(c - 1) * W_A] = proj(c).astype(BF16)


def _even_in(x, nw, w, w_ab, qw, kw, bd, tm, ei, n_layers, kv_prev):
    m, d = x.shape
    n_rest = P_MAIN - 2 * W_A
    full = lambda a: pl.BlockSpec(a.shape, lambda i: (0,) * a.ndim)
    row = lambda n: pl.BlockSpec((tm, n), lambda i: (i, 0))
    leaf = pl.BlockSpec((None, tm * H_A, DA), lambda i: (ei, i, 0))
    leaf_shape = jax.ShapeDtypeStruct((n_layers, m * H_A, DA), F32)
    n_alias = 0 if kv_prev is None else 2
    return pl.pallas_call(
        functools.partial(_even_in_kernel, n_alias=n_alias),
        out_shape=(jax.ShapeDtypeStruct((m, W_A), BF16), jax.ShapeDtypeStruct((m, W_A), BF16), leaf_shape, leaf_shape,
                   jax.ShapeDtypeStruct((m, n_rest), BF16), jax.ShapeDtypeStruct((m, LANE), F32)),
        grid=(m // tm,),
        in_specs=[row(d), full(nw), full(w), full(w_ab), full(qw), full(kw), full(bd)]
                 + [pl.BlockSpec(memory_space=pl.ANY)] * n_alias,
        out_specs=(row(W_A), row(W_A), leaf, leaf, row(n_rest), row(LANE)),
        input_output_aliases={} if kv_prev is None else {7: 2, 8: 3},
        compiler_params=_cparams("arbitrary"),
        name="even_in",
    )(x, nw, w, w_ab, qw, kw, bd, *(kv_prev or ()))


def _t5_bias(n, tab_ref, h):
    nf = jnp.maximum(n, 1).astype(F32)
    large = MAX_EXACT + (jnp.log(nf / MAX_EXACT) / math.log(MAX_DISTANCE / MAX_EXACT)
                         * (NUM_BUCKETS - MAX_EXACT)).astype(jnp.int32)
    large = jnp.minimum(large, NUM_BUCKETS - 1)
    bkt = jnp.where(n < MAX_EXACT, n, large)
    out = jnp.zeros(n.shape, F32)
    for b in range(NUM_BUCKETS):
        out = jnp.where(bkt == b, tab_ref[b, h], out)
    return (out - tab_ref[NUM_BUCKETS - 1, h]) * LOG2E


def _bias_tiles_kernel(tab_ref, o_ref, *, tq):
    h = pl.program_id(0)
    i = lax.broadcasted_iota(jnp.int32, (tq, tq), 0)
    j = lax.broadcasted_iota(jnp.int32, (tq, tq), 1)
    n0 = i - j
    o_ref[0, 0] = jnp.where(n0 >= 0, _t5_bias(jnp.maximum(n0, 0), tab_ref, h), NEG)
    o_ref[0, 1] = _t5_bias(n0 + tq, tab_ref, h)


def _bias_tiles(rel_table, tq):
    return pl.pallas_call(
        functools.partial(_bias_tiles_kernel, tq=tq),
        out_shape=jax.ShapeDtypeStruct((H_A, 2, tq, tq), F32),
        grid=(H_A,),
        in_specs=[pl.BlockSpec(memory_space=pltpu.SMEM)],
        out_specs=pl.BlockSpec((1, 2, tq, tq), lambda h: (h, 0, 0, 0)),
        compiler_params=_cparams("arbitrary"),
        name="bias_tiles",
    )(rel_table)


def _bias_decode_kernel(tab_ref, o_ref, *, page):
    w = page * H_A
    row = lax.broadcasted_iota(jnp.int32, (2 * H_A, w), 0)
    lane = lax.broadcasted_iota(jnp.int32, (2 * H_A, w), 1)
    valid = (lane & (H_A - 1)) == (row >> 1)
    n = page - (lane >> 2)
    near = jnp.zeros((2 * H_A, w), F32)
    new = jnp.zeros((2 * H_A, LANE), F32)
    for h in range(H_A):
        near = jnp.where((row >> 1) == h, _t5_bias(n, tab_ref, h), near)
        new = jnp.where((row[:, :LANE] >> 1) == h, _t5_bias(jnp.zeros((2 * H_A, LANE), jnp.int32), tab_ref, h), new)
    o_ref[:, 0:w] = jnp.where(valid, 0.0, NEG)
    o_ref[:, w:2 * w] = jnp.where(valid, near, NEG)
    o_ref[:, 2 * w:2 * w + LANE] = new


def _bias_decode(rel_table, page):
    assert H_A == 4
    return pl.pallas_call(
        functools.partial(_bias_decode_kernel, page=page),
        out_shape=jax.ShapeDtypeStruct((2 * H_A, 2 * page * H_A + LANE), F32),
        in_specs=[pl.BlockSpec(memory_space=pltpu.SMEM)],
        out_specs=pl.BlockSpec(memory_space=pltpu.VMEM),
        name="bias_decode",
    )(rel_table)


def _attn_finish(o, sw, z, out_scale):
    ms = jnp.mean(o * o, axis=-1, keepdims=True)
    return o * lax.rsqrt(ms + EPS) * sw * out_scale * _silu(z.astype(F32))


def _attn_prompt_kernel(lam_ref, q_ref, k_ref, v_ref, bias_ref, za_ref, sw_ref, o_ref, m, l, a, *, tq, out_scale):
    qi = pl.program_id(2)
    q = q_ref[...].astype(F32)
    lane = lax.broadcasted_iota(jnp.int32, q.shape, 1)
    q2 = jnp.concatenate([jnp.where(lane < D_HA, q, 0.0), jnp.where(lane >= D_HA, q, 0.0)], axis=0).astype(BF16)
    m[...] = jnp.full(m.shape, NEG, F32)
    l[...] = jnp.zeros(l.shape, F32)
    a[...] = jnp.zeros(a.shape, F32)
    reps = tq // LANE

    def step(j, bias):
        rows = pl.ds(pl.multiple_of(j * tq, tq), tq)
        s = lax.dot_general(q2, k_ref[rows, :], (((1,), (1,)), ((), ())), preferred_element_type=F32)
        if bias is not None:
            s = s + jnp.concatenate([bias, bias], axis=0)
        m_prev = m[...]
        m_new = jnp.maximum(m_prev, jnp.max(s, axis=1, keepdims=True))
        p = jnp.exp2(s - jnp.concatenate([m_new] * reps, axis=1))
        alpha = jnp.exp2(m_prev - m_new)
        l[...] = alpha * l[...] + jnp.sum(p, axis=1, keepdims=True)
        a[...] = alpha * a[...] + jnp.dot(p.astype(BF16), v_ref[rows, :], preferred_element_type=F32)
        m[...] = m_new

    def far(j, c):
        step(j, None)
        return c

    lax.fori_loop(0, jnp.maximum(qi - 1, 0), far, 0)

    @pl.when(qi >= 1)
    def _():
        step(qi - 1, bias_ref[0, 1])

    step(qi, bias_ref[0, 0])
    o = a[...] / l[...]
    o = o[:tq] - lam_ref[0] * o[tq:]
    o_ref[...] = _attn_finish(o, sw_ref[...], za_ref[...], out_scale).astype(BF16)


def _attn_prompt(lam, q, k, bias, p, sw, nb, t, tq, out_scale):
    m = q.shape[0]
    nq = t // tq
    za0 = W_A // DA
    v0 = 0
    return pl.pallas_call(
        functools.partial(_attn_prompt_kernel, tq=tq, out_scale=out_scale),
        out_shape=jax.ShapeDtypeStruct((m, W_A), BF16),
        grid=(nb, H_A, nq),
        in_specs=[pl.BlockSpec(memory_space=pltpu.SMEM),
                  pl.BlockSpec((tq, DA), lambda b, h, i: (b * nq + i, h)),
                  pl.BlockSpec((t, DA), lambda b, h, i: (b, h)),
                  pl.BlockSpec((t, DA), lambda b, h, i: (b, v0 + h)),
                  pl.BlockSpec((1, 2, tq, tq), lambda b, h, i: (h, 0, 0, 0)),
                  pl.BlockSpec((tq, DA), lambda b, h, i: (b * nq + i, za0 + h)),
                  pl.BlockSpec((1, DA), lambda b, h, i: (0, 0))],
        out_specs=pl.BlockSpec((tq, DA), lambda b, h, i: (b * nq + i, h)),
        scratch_shapes=[pltpu.VMEM((2 * tq, LANE), F32), pltpu.VMEM((2 * tq, LANE), F32),
                        pltpu.VMEM((2 * tq, DA), F32)],
        compiler_params=_cparams("arbitrary", "arbitrary", "arbitrary"),
        name="attn_prompt",
    )(lam, q, k, p, bias, p, sw)


def _attn_decode_kernel(pt_ref, lam_ref, q_ref, kn_ref, vn_ref, *refs, page, npg, out_scale):
    ck_refs, cv_refs = refs[:npg], refs[npg:2 * npg]
    bias_ref, za_ref, sw_ref, o_ref = refs[2 * npg:]
    rows, w = 2 * H_A, page * H_A
    row = lax.broadcasted_iota(jnp.int32, (rows, DA), 0)
    lane = lax.broadcasted_iota(jnp.int32, (rows, DA), 1)

    def head_rows(x):
        out = jnp.zeros((rows, DA), F32)
        for h in range(H_A):
            out = jnp.where((row >> 1) == h, jnp.broadcast_to(x[:, h * DA:(h + 1) * DA], (rows, DA)), out)
        return out

    q8 = jnp.where((lane >> 6) == (row & 1), head_rows(q_ref[...].astype(F32)), 0.0)
    q8b = q8.astype(BF16)
    s_all = []
    for j in range(npg):
        s = lax.dot_general(q8b, ck_refs[j][...].astype(BF16), (((1,), (1,)), ((), ())), preferred_element_type=F32)
        s_all.append(s + (bias_ref[:, w:2 * w] if j == npg - 1 else bias_ref[:, 0:w]))
    s_new = (jnp.sum(q8 * head_rows(kn_ref[...].astype(F32)), axis=1, keepdims=True)
             + bias_ref[:, 2 * w:2 * w + 1])
    m = s_all[0]
    for s in s_all[1:]:
        m = jnp.maximum(m, s)
    m = jnp.maximum(jnp.max(m, axis=1, keepdims=True), s_new)
    p_new = jnp.exp2(s_new - m)
    l = p_new
    acc = p_new * head_rows(vn_ref[...].astype(F32))
    for j in range(npg):
        p = jnp.exp2(s_all[j] - m)
        l = l + jnp.sum(p, axis=1, keepdims=True)
        acc = acc + jnp.dot(p.astype(BF16), cv_refs[j][...].astype(BF16), preferred_element_type=F32)
    o = acc / l
    sw = sw_ref[...]
    z = za_ref[...]
    outs = []
    for h in range(H_A):
        oh = o[2 * h:2 * h + 1, :] - lam_ref[0] * o[2 * h + 1:2 * h + 2, :]
        outs.append(_attn_finish(oh, sw, z[:, h * DA:(h + 1) * DA], out_scale))
    o_ref[...] = jnp.concatenate(outs, axis=1).astype(BF16)


def _attn_decode(page_table, lam, q, kn, cache_k, cache_v, ei, bias, p, sw, out_scale):
    nb, npg = page_table.shape
    page = cache_k.shape[2]
    ck = cache_k.reshape(cache_k.shape[0], cache_k.shape[1], page * H_A, DA)
    cv = cache_v.reshape(cache_v.shape[0], cache_v.shape[1], page * H_A, DA)
    r3 = lambda a: a.reshape(nb, 1, a.shape[-1])
    vec = lambda c: pl.BlockSpec((None, 1, W_A), lambda b, pt, c=c: (b, 0, c))
    cache = [pl.BlockSpec((None, None, page * H_A, DA), lambda b, pt, j=j: (ei, pt[b * npg + j], 0, 0))
             for j in range(npg)]
    out = pl.pallas_call(
        functools.partial(_attn_decode_kernel, page=page, npg=npg, out_scale=out_scale),
        out_shape=jax.ShapeDtypeStruct((nb, 1, W_A), BF16),
        grid_spec=pltpu.PrefetchScalarGridSpec(
            num_scalar_prefetch=1,
            grid=(nb,),
            in_specs=[pl.BlockSpec(memory_space=pltpu.SMEM), vec(0), vec(0), vec(0)] + cache + cache
                     + [pl.BlockSpec(bias.shape, lambda b, pt: (0, 0)),
                        vec(1),
                        pl.BlockSpec((1, DA), lambda b, pt: (0, 0))],
            out_specs=pl.BlockSpec((None, 1, W_A), lambda b, pt: (b, 0, 0))),
        compiler_params=_cparams("arbitrary"),
        name="attn_decode",
    )(page_table.reshape(-1), lam, r3(q), r3(kn), r3(p), *([ck] * npg), *([cv] * npg), bias, r3(p), sw)
    return out.reshape(nb, W_A)


def _shifted_conv(x, prev, w):
    r, c = x.shape
    taps = w.shape[0]
    x3 = x.reshape(r // 8, 8, c)
    sub = lax.broadcasted_iota(jnp.int32, x3.shape, 1)
    y = x3 * w[taps - 1:taps, :].reshape(1, 1, c)
    for s in range(1, taps):
        rot = pltpu.roll(x3, s, 1)
        before = jnp.concatenate([pltpu.roll(prev, s, 0)[None], rot[:-1]], axis=0)
        y = y + jnp.where(sub < s, before, rot) * w[taps - 1 - s:taps - s, :].reshape(1, 1, c)
    return y.reshape(r, c)


def _gates(ab, alog, dtb):
    x = ab + dtb
    sp = jnp.maximum(x, 0.0) + jnp.log(1.0 + jnp.exp(-jnp.abs(x)))
    return -jnp.exp(alog) * sp, _sigmoid(ab)


def _l2norm(x):
    return x * lax.rsqrt(jnp.sum(x * x, axis=-1, keepdims=True) + EPS)


def _rms(x, w):
    return x * lax.rsqrt(jnp.mean(x * x, axis=-1, keepdims=True) + EPS) * w


def _level_masks(c):
    i = lax.broadcasted_iota(jnp.int32, (c, c), 0)
    j = lax.broadcasted_iota(jnp.int32, (c, c), 1)
    masks = []
    s = 1
    while s < c:
        sh = s.bit_length() - 1
        masks.append((((i >> (sh + 1)) == (j >> (sh + 1))) & ((i >> sh) != (j >> sh)) & (i > j)).astype(F32))
        s *= 2
    return masks


def _gdn_prompt_kernel(pq_ref, pk_ref, pv_ref, zb_ref, ab_ref, cw_ref, alog_ref, dtb_ref, gnw_ref,
                       ob_ref, s_ref, carry_ref, cb_ref, g_ref, beta_ref, u_ref, wq_ref, ak_ref, gl_ref,
                       *, rows, chunk):
    t = pl.program_id(1)
    c = chunk

    @pl.when(t == 0)
    def _():
        carry_ref[...] = jnp.zeros(carry_ref.shape, F32)
        s_ref[...] = jnp.zeros(s_ref.shape, F32)

    cw = cw_ref[...]
    for seg, ref in enumerate((pq_ref, pk_ref, pv_ref)):
        cols = slice(seg * W_B, (seg + 1) * W_B)
        x = ref[...].astype(F32)
        cb_ref[:, cols] = _silu(_shifted_conv(x, carry_ref[:, cols], cw[:, cols]))
        carry_ref[:, cols] = x[rows - 8:rows]
    g, beta = _gates(ab_ref[...], alog_ref[...], dtb_ref[...])
    g_ref[...] = g
    beta_ref[...] = beta

    ii = lax.broadcasted_iota(jnp.int32, (c, c), 0)
    jj = lax.broadcasted_iota(jnp.int32, (c, c), 1)
    incl = ii >= jj
    ltri = incl.astype(BF16)
    masks = _level_masks(c)

    def prep(ci):
        rs = pl.ds(pl.multiple_of(ci * c, c), c)
        gch = g_ref[rs, :]
        g1 = gch.astype(BF16)
        r1 = gch - g1.astype(F32)
        g2 = r1.astype(BF16)
        g3 = (r1 - g2.astype(F32)).astype(BF16)
        d = functools.partial(jnp.dot, preferred_element_type=F32)
        gc = d(ltri, g1) + d(ltri, g2) + d(ltri, g3)
        gct = jnp.concatenate([gc, gc], axis=0).T
        ge = jnp.exp(gc)
        kdec = jnp.exp(gc[c - 1:c, :] - gc)
        gl_ref[ci] = ge[c - 1:c, :]
        return rs, gc, gct, ge, kdec, beta_ref[rs, :]

    def local(ig, carry):
        items = []
        for ci in [LOCAL_GROUP * ig + k for k in range(LOCAL_GROUP)]:
            rs, gc, gct, ge, kdec, bch = prep(ci)
            for h in range(H_B):
                hc = slice(h * DK_B, (h + 1) * DK_B)
                qh = _l2norm(cb_ref[rs, hc]) * (DK_B ** -0.5)
                kh = _l2norm(cb_ref[rs, slice(W_B + h * DK_B, W_B + (h + 1) * DK_B)])
                vh = cb_ref[rs, slice(2 * W_B + h * DV_B, 2 * W_B + (h + 1) * DV_B)]
                bcol = bch[:, H_B + h:H_B + h + 1]
                gecol = ge[:, h:h + 1]
                decay = jnp.exp(jnp.where(incl, gc[:, h:h + 1] - gct[h:h + 1, :c], NEG))
                kbeta = kh * bcol
                wq_ref[ci, h, c:2 * c] = (qh * gecol).astype(BF16)
                ak_ref[ci, h, c:c + DK_B] = (kh * kdec[:, h:h + 1]).T.astype(BF16)
                items.append(dict(ci=ci, h=h, rs=rs, hc=hc, decay=decay, qk=(kbeta, qh, kh),
                                  rhs=jnp.concatenate([vh * bcol, kbeta * gecol], axis=1)))
        for it in items:
            kbeta, qh, kh = it.pop("qk")
            both = _dot_nt(jnp.concatenate([kbeta, qh], axis=0), kh)
            it["mm"] = jnp.where(ii > jj, both[:c] * it["decay"], 0.0)
            ak_ref[it["ci"], it["h"], 0:c] = (both[c:] * it["decay"]).astype(BF16)
        for it in items:
            it["pp"] = -(it["mm"] * masks[0])
        for mk in masks[1:]:
            for it in items:
                e = it["mm"] * mk
                it["x"] = e + _dot(it["pp"], e)
            for it in items:
                it["pp"] = it["pp"] - (it["x"] + _dot(it["x"], it["pp"]))
        for it in items:
            uw = it["rhs"] + _dot(it["pp"], it["rhs"])
            u_ref[it["rs"], it["hc"]] = uw[:, :DV_B]
            wq_ref[it["ci"], it["h"], 0:c] = uw[:, DV_B:].astype(BF16)
        return carry

    lax.fori_loop(0, rows // (LOCAL_GROUP * c), local, 0)
    gnw = gnw_ref[...]

    def scan(ci, carry):
        rs = pl.ds(pl.multiple_of(ci * c, c), c)
        gl = gl_ref[ci]
        heads = range(H_B)
        hcs = [slice(h * DK_B, (h + 1) * DK_B) for h in heads]
        s_old = [s_ref[h] for h in heads]
        r = [jnp.dot(wq_ref[ci, h], s_old[h].astype(BF16), preferred_element_type=F32) for h in heads]
        v_new = [u_ref[rs, hcs[h]] - r[h][:c] for h in heads]
        r2 = [jnp.dot(ak_ref[ci, h], v_new[h].astype(BF16), preferred_element_type=F32) for h in heads]
        for h in heads:
            s_ref[h] = s_old[h] * gl[:, h:h + 1] + r2[h][c:]
            ob_ref[rs, hcs[h]] = (_rms(r[h][c:] + r2[h][:c], gnw)
                                  * _silu(zb_ref[rs, hcs[h]].astype(F32))).astype(BF16)
        return carry

    lax.fori_loop(0, rows // c, scan, 0)


def _gdn_prompt(p, ab, cw, alog, dtb, gnw, nb, t, rows):
    m = p.shape[0]
    nt = t // rows
    c = math.gcd(GDN_CHUNK, t)
    col = lambda cidx: pl.BlockSpec((rows, W_B), lambda b, i, cidx=cidx: (b * nt + i, cidx))
    full = lambda a: pl.BlockSpec(a.shape, lambda b, i: (0,) * a.ndim)
    c0 = (2 * W_A) // W_B
    return pl.pallas_call(
        functools.partial(_gdn_prompt_kernel, rows=rows, chunk=c),
        out_shape=(jax.ShapeDtypeStruct((m, W_B), BF16), jax.ShapeDtypeStruct((nb, H_B, DK_B, DV_B), F32)),
        grid=(nb, nt),
        in_specs=[col(c0), col(c0 + 1), col(c0 + 2), col(c0 + 3),
                  pl.BlockSpec((rows, LANE), lambda b, i: (b * nt + i, 0)),
                  full(cw), full(alog), full(dtb), full(gnw)],
        out_specs=(pl.BlockSpec((rows, W_B), lambda b, i: (b * nt + i, 0)),
                   pl.BlockSpec((None, H_B, DK_B, DV_B), lambda b, i: (b, 0, 0, 0))),
        scratch_shapes=[pltpu.VMEM((8, QKV_B), F32), pltpu.VMEM((rows, QKV_B), F32),
                        pltpu.VMEM((rows, LANE), F32), pltpu.VMEM((rows, LANE), F32),
                        pltpu.VMEM((rows, W_B), F32),
                        pltpu.VMEM((rows // c, H_B, 2 * c, DK_B), BF16),
                        pltpu.VMEM((rows // c, H_B, c + DK_B, c), BF16),
                        pltpu.VMEM((rows // c, 1, LANE), F32)],
        compiler_params=_cparams("arbitrary", "arbitrary"),
        name="gdn_prompt",
    )(p, p, p, p, ab, cw, alog, dtb, gnw)


def _gdn_decode_prep_kernel(pq_ref, pk_ref, pv_ref, ab_ref, c0_ref, cw_ref, alog_ref, dtb_ref,
                            q_ref, k_ref, v_ref, eg_ref, beta_ref, qk_ref, cn_ref):
    cw = cw_ref[...]
    taps = cw.shape[0]
    outs = (q_ref, k_ref, v_ref)
    for seg, ref in enumerate((pq_ref, pk_ref, pv_ref)):
        cols = slice(seg * W_B, (seg + 1) * W_B)
        x = ref[...].astype(F32)
        y = x * cw[taps - 1:taps, cols]
        for j in range(taps - 1):
            y = y + c0_ref[j, :, cols] * cw[j:j + 1, cols]
            if j >= 1:
                cn_ref[j - 1, :, cols] = c0_ref[j, :, cols]
        cn_ref[taps - 2, :, cols] = x
        outs[seg][...] = _silu(y)
    lane = lax.broadcasted_iota(jnp.int32, eg_ref.shape, 1)
    qk = jnp.zeros(eg_ref.shape, F32)
    for h in range(H_B):
        hc = slice(h * DK_B, (h + 1) * DK_B)
        qh = _l2norm(q_ref[:, hc]) * (DK_B ** -0.5)
        kh = _l2norm(k_ref[:, hc])
        q_ref[:, hc] = qh
        k_ref[:, hc] = kh
        qk = jnp.where(lane == h, jnp.sum(qh * kh, axis=-1, keepdims=True), qk)
    g, beta = _gates(ab_ref[...], alog_ref[...], dtb_ref[...])
    eg_ref[...] = jnp.exp(g)
    beta_ref[...] = beta
    qk_ref[...] = qk


def _gdn_decode_prep(p, ab, conv0_t, cw, alog, dtb):
    nb = p.shape[0]
    c0 = (2 * W_A) // W_B
    col = lambda cidx: pl.BlockSpec((nb, W_B), lambda i, cidx=cidx: (0, cidx))
    full = lambda a: pl.BlockSpec(a.shape, lambda i: (0,) * a.ndim)
    wide = jax.ShapeDtypeStruct((nb, W_B), F32)
    narrow = jax.ShapeDtypeStruct((nb, LANE), F32)
    ospec = lambda s: pl.BlockSpec(s.shape, lambda i: (0,) * len(s.shape))
    outs = (wide, wide, wide, narrow, narrow, narrow, jax.ShapeDtypeStruct(conv0_t.shape, F32))
    return pl.pallas_call(
        _gdn_decode_prep_kernel,
        out_shape=outs,
        grid=(1,),
        in_specs=[col(c0), col(c0 + 1), col(c0 + 2),
                  full(ab), full(conv0_t), full(cw), full(alog), full(dtb)],
        out_specs=tuple(ospec(s) for s in outs),
        compiler_params=_cparams("arbitrary"),
        name="gdn_decode_prep",
    )(p, p, p, ab, conv0_t, cw, alog, dtb)


def _gdn_decode_kernel(s_ref, qt_ref, kt_ref, v_ref, eg_ref, beta_ref, qk_ref, zb_ref, gnw_ref, *refs, bb, n_alias):
    so_ref, ob_ref, o_scr = refs[n_alias:]
    v = v_ref[...]
    eg = eg_ref[...]
    beta = beta_ref[...]
    qk = qk_ref[...]
    for h in range(H_B):
        hc = slice(h * DV_B, (h + 1) * DV_B)
        qt = qt_ref[h]
        kt = kt_ref[h]
        for i in range(bb):
            s_old = s_ref[i, h]
            kc = kt[:, i:i + 1]
            qc = qt[:, i:i + 1]
            egs = eg[i:i + 1, h:h + 1]
            ks = jnp.sum(s_old * kc, axis=0, keepdims=True)
            qs = jnp.sum(s_old * qc, axis=0, keepdims=True)
            v_new = beta[i:i + 1, H_B + h:H_B + h + 1] * (v[i:i + 1, hc] - egs * ks)
            o_scr[i:i + 1, hc] = egs * qs + qk[i:i + 1, h:h + 1] * v_new
            so_ref[i, h] = s_old * egs + kc * v_new
    gnw = gnw_ref[...]
    z = zb_ref[...]
    outs = [_rms(o_scr[:, h * DV_B:(h + 1) * DV_B], gnw) * _silu(z[:, h * DV_B:(h + 1) * DV_B].astype(F32))
            for h in range(H_B)]
    ob_ref[...] = jnp.concatenate(outs, axis=1).astype(BF16)


def _gdn_decode(state, ei, qt, kt, v, eg, beta, qk, p, gnw, bb, s_prev):
    nb = v.shape[0]
    ns = nb // bb
    narrow = pl.BlockSpec((bb, LANE), lambda i: (i, 0))
    tr = pl.BlockSpec((None, H_B, DK_B, bb), lambda i: (i, 0, 0, 0))
    zb0 = (2 * W_A + QKV_B) // W_B
    n_alias = 0 if s_prev is None else 1
    return pl.pallas_call(
        functools.partial(_gdn_decode_kernel, bb=bb, n_alias=n_alias),
        out_shape=(jax.ShapeDtypeStruct(state.shape, F32), jax.ShapeDtypeStruct((nb, W_B), BF16)),
        grid=(ns,),
        in_specs=[pl.BlockSpec((None, bb, H_B, DK_B, DV_B), lambda i: (ei, i, 0, 0, 0)),
                  tr, tr,
                  pl.BlockSpec((bb, W_B), lambda i: (i, 0)),
                  narrow, narrow, narrow,
                  pl.BlockSpec((bb, W_B), lambda i: (i, zb0)),
                  pl.BlockSpec((1, DV_B), lambda i: (0, 0))] + [pl.BlockSpec(memory_space=pl.ANY)] * n_alias,
        out_specs=(pl.BlockSpec((None, bb, H_B, DK_B, DV_B), lambda i: (ei, i, 0, 0, 0)),
                   pl.BlockSpec((bb, W_B), lambda i: (i, 0))),
        input_output_aliases={9: 0} if n_alias else {},
        scratch_shapes=[pltpu.VMEM((bb, W_B), F32)],
        compiler_params=_cparams("arbitrary"),
        name="gdn_decode",
    )(state, qt, kt, v, eg, beta, qk, p, gnw, *(() if s_prev is None else (s_prev,)))


def _out_proj_kernel(x_ref, oa_ref, ob_ref, w_ref, y_ref):
    y_ref[...] = (x_ref[...]
                  + jnp.dot(oa_ref[...], w_ref[0:W_A, :], preferred_element_type=F32)
                  + jnp.dot(ob_ref[...], w_ref[W_A:W_A + W_B, :], preferred_element_type=F32))


def _out_proj(x, oa, ob, w, tm):
    m, d = x.shape
    return pl.pallas_call(
        _out_proj_kernel,
        out_shape=jax.ShapeDtypeStruct((m, d), F32),
        grid=(m // tm,),
        in_specs=[pl.BlockSpec((tm, d), lambda i: (i, 0)),
                  pl.BlockSpec((tm, W_A), lambda i: (i, 0)),
                  pl.BlockSpec((tm, W_B), lambda i: (i, 0)),
                  pl.BlockSpec(w.shape, lambda i: (0, 0))],
        out_specs=pl.BlockSpec((tm, d), lambda i: (i, 0)),
        compiler_params=_cparams("arbitrary"),
        name="out_proj",
    )(x, oa, ob, w)


def _odd_prompt_kernel(x_ref, nw_ref, wi_ref, cw_ref, wo_ref, y_ref, sc_ref, carry_ref, *, rows):
    t = pl.program_id(1)
    d = x_ref.shape[1]

    @pl.when(t == 0)
    def _():
        carry_ref[...] = jnp.zeros(carry_ref.shape, F32)

    x = x_ref[...]
    h = _rms_rows(x, nw_ref[...]).astype(BF16)
    proj = lambda c: jnp.dot(h, wi_ref[:, c * d:(c + 1) * d], preferred_element_type=F32)
    u = proj(1) * proj(2)
    cv = _shifted_conv(u, carry_ref[...], cw_ref[...])
    carry_ref[...] = u[rows - 8:rows]
    g = proj(0) * cv * _silu(proj(3))
    y_ref[...] = x + jnp.dot(g.astype(BF16), wo_ref[...], preferred_element_type=F32)

    @pl.when(t == pl.num_programs(1) - 1)
    def _():
        sc_ref[...] = u[rows - 8:rows]


def _odd_prompt(x, nw, w_in, cw, w_out, nb, t, rows):
    m, d = x.shape
    nt = t // rows
    full = lambda a: pl.BlockSpec(a.shape, lambda b, i: (0,) * a.ndim)
    return pl.pallas_call(
        functools.partial(_odd_prompt_kernel, rows=rows),
        out_shape=(jax.ShapeDtypeStruct((m, d), F32), jax.ShapeDtypeStruct((nb, 8, d), F32)),
        grid=(nb, nt),
        in_specs=[pl.BlockSpec((rows, d), lambda b, i: (b * nt + i, 0)),
                  full(nw), full(w_in), full(cw), full(w_out)],
        out_specs=(pl.BlockSpec((rows, d), lambda b, i: (b * nt + i, 0)),
                   pl.BlockSpec((None, 8, d), lambda b, i: (b, 0, 0))),
        scratch_shapes=[pltpu.VMEM((8, d), F32)],
        compiler_params=_cparams("arbitrary", "arbitrary"),
        name="odd_prompt",
    )(x, nw, w_in, cw, w_out)


def _odd_decode_kernel(x_ref, bg_ref, cg_ref, hh_ref, z_ref, b0_ref, b1_ref, cw_ref, w_ref, y_ref, u_ref):
    cw = cw_ref[...]
    u = cg_ref[...].astype(F32) * hh_ref[...].astype(F32)
    cv = b0_ref[...] * cw[0:1, :] + b1_ref[...] * cw[1:2, :] + u * cw[2:3, :]
    g = bg_ref[...].astype(F32) * cv * _silu(z_ref[...].astype(F32))
    y_ref[...] = x_ref[...] + jnp.dot(g.astype(BF16), w_ref[...], preferred_element_type=F32)
    u_ref[...] = u


def _odd_decode(x, p, b0, b1, cw, w):
    m, d = x.shape
    col = lambda c: pl.BlockSpec((m, d), lambda i, c=c: (0, c))
    full = lambda a: pl.BlockSpec(a.shape, lambda i: (0,) * a.ndim)
    return pl.pallas_call(
        _odd_decode_kernel,
        out_shape=(jax.ShapeDtypeStruct((m, d), F32), jax.ShapeDtypeStruct((m, d), F32)),
        grid=(1,),
        in_specs=[full(x), col(0), col(1), col(2), col(3), full(b0), full(b1), full(cw), full(w)],
        out_specs=(pl.BlockSpec((m, d), lambda i: (0, 0)), pl.BlockSpec((m, d), lambda i: (0, 0))),
        compiler_params=_cparams("arbitrary"),
        name="odd_decode",
    )(x, p, p, p, p, b0, b1, cw, w)


def _tile(n, want):
    t = math.gcd(n, want)
    assert t == n or t % 8 == 0, (n, want)
    return t


def _pad_lanes(v):
    return jnp.pad(v.astype(F32), (0, LANE - v.shape[0])).reshape(1, LANE)


def kernel(x_prompt, x_sample, cache_k, cache_v, page_table, state_gdn, state_gdn_conv, state_shortconv, norm_w, rel_table, w_in_even, w_out_even, qn_w, kn_w, lam_q1, lam_k1, lam_q2, lam_k2, subln_w, gdn_conv_w, gdn_a_log, gdn_dt_bias, gdn_norm_w, w_in_odd, sc_conv_w, w_out_odd):
    nbp, t, d = x_prompt.shape
    nbs = x_sample.shape[0]
    page = cache_k.shape[2]
    assert x_sample.shape[1] == 1 and page >= MAX_DISTANCE and t % 8 == 0
    tq = _tile(t, 512)
    assert tq >= MAX_DISTANCE

    xp = x_prompt.reshape(nbp * t, d)
    xs = x_sample.reshape(nbs, d)
    tm_p = _tile(nbp * t, 1024)

    g64 = jnp.arange(W_A) // D_HA
    bd = jnp.where(g64[:, None] == g64[None, :], 1.0 / D_HA, 0.0).astype(BF16)
    bias_p = _bias_tiles(rel_table.astype(F32), tq)
    bias_s = _bias_decode(rel_table.astype(F32), page)

    n_even = (DEPTH + 1) // 2
    kv_p = kv_s = s_s = None
    sp, gcp, scp = [], [], []
    gcs, scs = [], []
    ei = oi = 0
    for li in range(DEPTH):
        nw = norm_w[li].reshape(1, d)
        if li % 2 == 0:
            lambda_init = 0.8 - 0.6 * math.exp(-0.3 * li)
            w_in = w_in_even[ei][:, :P_MAIN].astype(BF16)
            w_ab = jnp.pad(w_in_even[ei][:, P_MAIN:], ((0, 0), (0, LANE - 2 * H_B))).astype(BF16)
            w_out = w_out_even[ei].astype(BF16)
            qw = jnp.tile(qn_w[ei], W_A // D_HA).reshape(1, W_A)
            kw = jnp.tile(kn_w[ei], W_A // D_HA).reshape(1, W_A)
            sw = subln_w[ei].reshape(1, DA)
            lam = (jnp.exp(jnp.sum(lam_q1[ei] * lam_k1[ei]).astype(F32))
                   - jnp.exp(jnp.sum(lam_q2[ei] * lam_k2[ei]).astype(F32)) + lambda_init).reshape(1)
            cw = gdn_conv_w[ei]
            alog = _pad_lanes(gdn_a_log[ei])
            dtb = _pad_lanes(gdn_dt_bias[ei])
            gnw = gdn_norm_w[ei].reshape(1, DV_B)

            qb, kb, *kv_p, p, ab = _even_in(xp, nw, w_in, w_ab, qw, kw, bd, _tile(nbp * t, 512), ei, n_even, kv_p)
            oa = _attn_prompt(lam, qb, kb, bias_p, p, sw, nbp, t, tq, 1.0 - lambda_init)
            ob, s_new = _gdn_prompt(p, ab, cw, alog, dtb, gnw, nbp, t, _tile(t, 512))
            xp = _out_proj(xp, oa, ob, w_out, tm_p)
            sp.append(s_new)
            c0 = 2 * W_A
            gcp.append(p.reshape(nbp, t, -1)[:, t - (GDN_CONV - 1):, c0:c0 + QKV_B].astype(F32))

            qb, kb, *kv_s, p, ab = _even_in(xs, nw, w_in, w_ab, qw, kw, bd, nbs, ei, n_even, kv_s)
            oa = _attn_decode(page_table, lam, qb, kb, cache_k, cache_v, ei, bias_s, p, sw, 1.0 - lambda_init)
            conv0_t = jnp.swapaxes(state_gdn_conv[ei], 0, 1)
            qn, kn, vv, eg, beta, qk, conv_new = _gdn_decode_prep(p, ab, conv0_t, cw, alog, dtb)
            bb = 8
            to_cols = lambda a: a.reshape(nbs // bb, bb, H_B, DK_B).transpose(0, 2, 3, 1)
            s_s, ob = _gdn_decode(state_gdn, ei, to_cols(qn), to_cols(kn), vv, eg, beta, qk, p, gnw, bb, s_s)
            xs = _out_proj(xs, oa, ob, w_out, nbs)
            gcs.append(jnp.swapaxes(conv_new, 0, 1))
            ei += 1
        else:
            w_in = w_in_odd[oi].astype(BF16)
            w_out = w_out_odd[oi].astype(BF16)
            cw = sc_conv_w[oi]
            xp, tail = _odd_prompt(xp, nw, w_in, cw, w_out, nbp, t, _tile(t, 512))
            scp.append(tail[:, 8 - (SC_WIDTH - 1):, :])

            p = _norm_proj(xs, nw, w_in, nbs, 1024)
            buf0 = state_shortconv[oi]
            xs, u = _odd_decode(xs, p, buf0[:, 0, :], buf0[:, 1, :], cw, w_out)
            scs.append(jnp.stack([buf0[:, 1, :], u], axis=1))
            oi += 1

    leaf_p = lambda a: a.reshape(n_even, nbp, t, H_A, DA)
    leaf_s = lambda a: a.reshape(n_even, nbs, 1, H_A, DA)
    return (xp.reshape(nbp, t, d), xs.reshape(nbs, 1, d),
            leaf_p(kv_p[0]), leaf_p(kv_p[1]), jnp.stack(sp), jnp.stack(gcp), jnp.stack(scp),
            leaf_s(kv_s[0]), leaf_s(kv_s[1]), s_s, jnp.stack(gcs), jnp.stack(scs))
```

```python
import functools
import math

import jax
import jax.numpy as jnp
from jax import lax
from jax.experimental import pallas as pl
from jax.experimental.pallas import tpu as pltpu

F32, BF16 = jnp.float32, jnp.bfloat16

DEPTH = 4
H_A, D_HA = 4, 64
DA = 2 * D_HA
W_A = H_A * DA
H_B, DK_B, DV_B = 4, 128, 128
W_B = H_B * DV_B
QKV_B = 2 * H_B * DK_B + H_B * DV_B
GDN_CONV, GDN_CHUNK, SC_WIDTH = 4, 64, 3
NUM_BUCKETS, MAX_EXACT, MAX_DISTANCE = 32, 16, 128
EPS, NEG = 1e-6, -1e30
LOG2E = math.log2(math.e)
LANE = 128
P_MAIN = 4 * W_A + QKV_B + W_B
VMEM_LIMIT = 48 * 1024 * 1024
LOCAL_GROUP = 4


def _cparams(*sem):
    return pltpu.CompilerParams(dimension_semantics=sem, vmem_limit_bytes=VMEM_LIMIT)


def _silu(z):
    h = 0.5 * z
    return h + h * jnp.tanh(h)


def _sigmoid(z):
    return 1.0 / (1.0 + jnp.exp(-z))


def _dot(a, b):
    return jnp.dot(a.astype(BF16), b.astype(BF16), preferred_element_type=F32)


def _dot_nt(a, b):
    return lax.dot_general(a.astype(BF16), b.astype(BF16), (((1,), (1,)), ((), ())), preferred_element_type=F32)


def _rms_rows(x, w):
    return x * lax.rsqrt(jnp.mean(x * x, axis=-1, keepdims=True) + EPS) * w


def _norm_proj_kernel(x_ref, nw_ref, w_ref, o_ref, h_ref):
    @pl.when(pl.program_id(1) == 0)
    def _():
        h_ref[...] = _rms_rows(x_ref[...], nw_ref[...]).astype(BF16)

    o_ref[...] = jnp.dot(h_ref[...], w_ref[...], preferred_element_type=F32).astype(BF16)


def _norm_proj(x, nw, w, tm, tn):
    m, d = x.shape
    n = w.shape[1]
    return pl.pallas_call(
        _norm_proj_kernel,
        out_shape=jax.ShapeDtypeStruct((m, n), BF16),
        grid=(m // tm, n // tn),
        in_specs=[pl.BlockSpec((tm, d), lambda i, j: (i, 0)),
                  pl.BlockSpec((1, d), lambda i, j: (0, 0)),
                  pl.BlockSpec((d, tn), lambda i, j: (0, j))],
        out_specs=pl.BlockSpec((tm, tn), lambda i, j: (i, j)),
        scratch_shapes=[pltpu.VMEM((tm, d), BF16)],
        compiler_params=_cparams("arbitrary", "arbitrary"),
        name="norm_proj",
    )(x, nw, w)


def _even_in_kernel(x_ref, nw_ref, w_ref, wab_ref, qw_ref, kw_ref, bd_ref, *refs, n_alias, tiles_per_seq):
    if tiles_per_seq:
        cw_ref = refs[0]
        qo_ref, kb_ref, kf_ref, vf_ref, p_ref, ab_ref, tail_ref, carry_ref = refs[1 + n_alias:]
    else:
        qo_ref, kb_ref, kf_ref, vf_ref, p_ref, ab_ref = refs[n_alias:]
    tm = x_ref.shape[0]
    h = _rms_rows(x_ref[...], nw_ref[...]).astype(BF16)
    proj = lambda c: jnp.dot(h, w_ref[:, c * W_A:(c + 1) * W_A], preferred_element_type=F32)
    ab_ref[...] = jnp.dot(h, wab_ref[...], preferred_element_type=F32)
    bd = bd_ref[...]

    def group_norm(x, w):
        ms = jnp.dot((x * x).astype(BF16), bd, preferred_element_type=F32)
        return x * lax.rsqrt(ms + EPS) * w

    def leaf(ref, val):
        for hd in range(H_A):
            ref[pl.ds(hd, tm, stride=H_A), :] = val[:, hd * DA:(hd + 1) * DA]

    qn = group_norm(proj(0), qw_ref[...])
    qo_ref[...] = (qn * (D_HA ** -0.5 * LOG2E)).astype(BF16)
    kn = group_norm(proj(1), kw_ref[...])
    kb_ref[...] = kn.astype(BF16)
    leaf(kf_ref, kn)
    v = proj(2)
    leaf(vf_ref, v)
    p_ref[:, 0:W_A] = v.astype(BF16)
    if tiles_per_seq:
        first = pl.program_id(0) % tiles_per_seq == 0

        @pl.when(first)
        def _():
            carry_ref[...] = jnp.zeros(carry_ref.shape, F32)

    c_qkv = (4 * W_A) // W_A
    for c in range(3, P_MAIN // W_A):
        y = proj(c)
        seg = c - c_qkv
        if tiles_per_seq and 0 <= seg < QKV_B // W_B:
            cols = slice(seg * W_B, (seg + 1) * W_B)
            raw = y
            y = _silu(_shifted_conv(raw, carry_ref[:, cols], cw_ref[:, cols]))
            carry_ref[:, cols] = raw[tm - 8:tm]
            tail_ref[:, cols] = raw[tm - 8:tm]
        p_ref[:, (c - 2) * W_A:(c - 1) * W_A] = y.astype(BF16)


def _even_in(x, nw, w, w_ab, qw, kw, bd, tm, ei, n_layers, kv_prev, conv_w=None, seq_len=None):
    m, d = x.shape
    tiles_per_seq = 0 if conv_w is None else seq_len // tm
    n_rest = P_MAIN - 2 * W_A
    full = lambda a: pl.BlockSpec(a.shape, lambda i: (0,) * a.ndim)
    row = lambda n: pl.BlockSpec((tm, n), lambda i: (i, 0))
    leaf = pl.BlockSpec((None, tm * H_A, DA), lambda i: (ei, i, 0))
    leaf_shape = jax.ShapeDtypeStruct((n_layers, m * H_A, DA), F32)
    n_alias = 0 if kv_prev is None else 2
    out_shape = [jax.ShapeDtypeStruct((m, W_A), BF16), jax.ShapeDtypeStruct((m, W_A), BF16), leaf_shape, leaf_shape,
                 jax.ShapeDtypeStruct((m, n_rest), BF16), jax.ShapeDtypeStruct((m, LANE), F32)]
    out_specs = [row(W_A), row(W_A), leaf, leaf, row(n_rest), row(LANE)]
    in_specs = [row(d), full(nw), full(w), full(w_ab), full(qw), full(kw), full(bd)]
    args = [x, nw, w, w_ab, qw, kw, bd]
    scratch = []
    if tiles_per_seq:
        in_specs.append(full(conv_w))
        args.append(conv_w)
        out_shape.append(jax.ShapeDtypeStruct((m // seq_len, 8, QKV_B), F32))
        out_specs.append(pl.BlockSpec((None, 8, QKV_B), lambda i: (i // tiles_per_seq, 0, 0)))
        scratch.append(pltpu.VMEM((8, QKV_B), F32))
    n_in = len(args)
    return pl.pallas_call(
        functools.partial(_even_in_kernel, n_alias=n_alias, tiles_per_seq=tiles_per_seq),
        out_shape=tuple(out_shape),
        grid=(m // tm,),
        in_specs=in_specs + [pl.BlockSpec(memory_space=pl.ANY)] * n_alias,
        out_specs=tuple(out_specs),
        input_output_aliases={} if kv_prev is None else {n_in: 2, n_in + 1: 3},
        scratch_shapes=scratch,
        compiler_params=_cparams("arbitrary"),
        name="even_in",
    )(*args, *(kv_prev or ()))


def _t5_bias(n, tab_ref, h):
    nf = jnp.maximum(n, 1).astype(F32)
    large = MAX_EXACT + (jnp.log(nf / MAX_EXACT) / math.log(MAX_DISTANCE / MAX_EXACT)
                         * (NUM_BUCKETS - MAX_EXACT)).astype(jnp.int32)
    large = jnp.minimum(large, NUM_BUCKETS - 1)
    bkt = jnp.where(n < MAX_EXACT, n, large)
    out = jnp.zeros(n.shape, F32)
    for b in range(NUM_BUCKETS):
        out = jnp.where(bkt == b, tab_ref[b, h], out)
    return (out - tab_ref[NUM_BUCKETS - 1, h]) * LOG2E


def _bias_tiles_kernel(tab_ref, o_ref, *, tq):
    h = pl.program_id(0)
    i = lax.broadcasted_iota(jnp.int32, (tq, tq), 0)
    j = lax.broadcasted_iota(jnp.int32, (tq, tq), 1)
    n0 = i - j
    o_ref[0, 0] = jnp.where(n0 >= 0, _t5_bias(jnp.maximum(n0, 0), tab_ref, h), NEG)
    o_ref[0, 1] = _t5_bias(n0 + tq, tab_ref, h)


def _bias_tiles(rel_table, tq):
    return pl.pallas_call(
        functools.partial(_bias_tiles_kernel, tq=tq),
        out_shape=jax.ShapeDtypeStruct((H_A, 2, tq, tq), F32),
        grid=(H_A,),
        in_specs=[pl.BlockSpec(memory_space=pltpu.SMEM)],
        out_specs=pl.BlockSpec((1, 2, tq, tq), lambda h: (h, 0, 0, 0)),
        compiler_params=_cparams("arbitrary"),
        name="bias_tiles",
    )(rel_table)


def _bias_decode_kernel(tab_ref, o_ref, *, page):
    w = page * H_A
    row = lax.broadcasted_iota(jnp.int32, (2 * H_A, w), 0)
    lane = lax.broadcasted_iota(jnp.int32, (2 * H_A, w), 1)
    valid = (lane & (H_A - 1)) == (row >> 1)
    n = page - (lane >> 2)
    near = jnp.zeros((2 * H_A, w), F32)
    new = jnp.zeros((2 * H_A, LANE), F32)
    for h in range(H_A):
        near = jnp.where((row >> 1) == h, _t5_bias(n, tab_ref, h), near)
        new = jnp.where((row[:, :LANE] >> 1) == h, _t5_bias(jnp.zeros((2 * H_A, LANE), jnp.int32), tab_ref, h), new)
    o_ref[:, 0:w] = jnp.where(valid, 0.0, NEG)
    o_ref[:, w:2 * w] = jnp.where(valid, near, NEG)
    o_ref[:, 2 * w:2 * w + LANE] = new


def _bias_decode(rel_table, page):
    assert H_A == 4
    return pl.pallas_call(
        functools.partial(_bias_decode_kernel, page=page),
        out_shape=jax.ShapeDtypeStruct((2 * H_A, 2 * page * H_A + LANE), F32),
        in_specs=[pl.BlockSpec(memory_space=pltpu.SMEM)],
        out_specs=pl.BlockSpec(memory_space=pltpu.VMEM),
        name="bias_decode",
    )(rel_table)


def _attn_finish(o, sw, z, out_scale):
    ms = jnp.mean(o * o, axis=-1, keepdims=True)
    return o * lax.rsqrt(ms + EPS) * sw * out_scale * _silu(z.astype(F32))


def _attn_prompt_kernel(lam_ref, q_ref, k_ref, v_ref, bias_ref, za_ref, sw_ref, o_ref, m, l, a, *, tq, out_scale):
    qi = pl.program_id(2)
    q = q_ref[...].astype(F32)
    lane = lax.broadcasted_iota(jnp.int32, q.shape, 1)
    q2 = jnp.concatenate([jnp.where(lane < D_HA, q, 0.0), jnp.where(lane >= D_HA, q, 0.0)], axis=0).astype(BF16)
    m[...] = jnp.full(m.shape, NEG, F32)
    l[...] = jnp.zeros(l.shape, F32)
    a[...] = jnp.zeros(a.shape, F32)
    reps = tq // LANE

    def step(j, bias):
        rows = pl.ds(pl.multiple_of(j * tq, tq), tq)
        s = lax.dot_general(q2, k_ref[rows, :], (((1,), (1,)), ((), ())), preferred_element_type=F32)
        if bias is not None:
            s = s + jnp.concatenate([bias, bias], axis=0)
        m_prev = m[...]
        m_new = jnp.maximum(m_prev, jnp.max(s, axis=1, keepdims=True))
        p = jnp.exp2(s - jnp.concatenate([m_new] * reps, axis=1))
        alpha = jnp.exp2(m_prev - m_new)
        l[...] = alpha * l[...] + jnp.sum(p, axis=1, keepdims=True)
        a[...] = alpha * a[...] + jnp.dot(p.astype(BF16), v_ref[rows, :], preferred_element_type=F32)
        m[...] = m_new

    def far(j, c):
        step(j, None)
        return c

    lax.fori_loop(0, jnp.maximum(qi - 1, 0), far, 0)

    @pl.when(qi >= 1)
    def _():
        step(qi - 1, bias_ref[0, 1])

    step(qi, bias_ref[0, 0])
    o = a[...] / l[...]
    o = o[:tq] - lam_ref[0] * o[tq:]
    o_ref[...] = _attn_finish(o, sw_ref[...], za_ref[...], out_scale).astype(BF16)


def _attn_prompt(lam, q, k, bias, p, sw, nb, t, tq, out_scale):
    m = q.shape[0]
    nq = t // tq
    za0 = W_A // DA
    v0 = 0
    return pl.pallas_call(
        functools.partial(_attn_prompt_kernel, tq=tq, out_scale=out_scale),
        out_shape=jax.ShapeDtypeStruct((m, W_A), BF16),
        grid=(nb, H_A, nq),
        in_specs=[pl.BlockSpec(memory_space=pltpu.SMEM),
                  pl.BlockSpec((tq, DA), lambda b, h, i: (b * nq + i, h)),
                  pl.BlockSpec((t, DA), lambda b, h, i: (b, h)),
                  pl.BlockSpec((t, DA), lambda b, h, i: (b, v0 + h)),
                  pl.BlockSpec((1, 2, tq, tq), lambda b, h, i: (h, 0, 0, 0)),
                  pl.BlockSpec((tq, DA), lambda b, h, i: (b * nq + i, za0 + h)),
                  pl.BlockSpec((1, DA), lambda b, h, i: (0, 0))],
        out_specs=pl.BlockSpec((tq, DA), lambda b, h, i: (b * nq + i, h)),
        scratch_shapes=[pltpu.VMEM((2 * tq, LANE), F32), pltpu.VMEM((2 * tq, LANE), F32),
                        pltpu.VMEM((2 * tq, DA), F32)],
        compiler_params=_cparams("arbitrary", "arbitrary", "arbitrary"),
        name="attn_prompt",
    )(lam, q, k, p, bias, p, sw)


def _attn_decode_kernel(pt_ref, lam_ref, q_ref, kn_ref, vn_ref, *refs, page, npg, out_scale):
    ck_refs, cv_refs = refs[:npg], refs[npg:2 * npg]
    bias_ref, za_ref, sw_ref, o_ref = refs[2 * npg:]
    rows, w = 2 * H_A, page * H_A
    row = lax.broadcasted_iota(jnp.int32, (rows, DA), 0)
    lane = lax.broadcasted_iota(jnp.int32, (rows, DA), 1)

    def head_rows(x):
        out = jnp.zeros((rows, DA), F32)
        for h in range(H_A):
            out = jnp.where((row >> 1) == h, jnp.broadcast_to(x[:, h * DA:(h + 1) * DA], (rows, DA)), out)
        return out

    q8 = jnp.where((lane >> 6) == (row & 1), head_rows(q_ref[...].astype(F32)), 0.0)
    q8b = q8.astype(BF16)
    s_all = []
    for j in range(npg):
        s = lax.dot_general(q8b, ck_refs[j][...].astype(BF16), (((1,), (1,)), ((), ())), preferred_element_type=F32)
        s_all.append(s + (bias_ref[:, w:2 * w] if j == npg - 1 else bias_ref[:, 0:w]))
    s_new = (jnp.sum(q8 * head_rows(kn_ref[...].astype(F32)), axis=1, keepdims=True)
             + bias_ref[:, 2 * w:2 * w + 1])
    m = s_all[0]
    for s in s_all[1:]:
        m = jnp.maximum(m, s)
    m = jnp.maximum(jnp.max(m, axis=1, keepdims=True), s_new)
    p_new = jnp.exp2(s_new - m)
    l = p_new
    acc = p_new * head_rows(vn_ref[...].astype(F32))
    for j in range(npg):
        p = jnp.exp2(s_all[j] - m)
        l = l + jnp.sum(p, axis=1, keepdims=True)
        acc = acc + jnp.dot(p.astype(BF16), cv_refs[j][...].astype(BF16), preferred_element_type=F32)
    o = acc / l
    sw = sw_ref[...]
    z = za_ref[...]
    outs = []
    for h in range(H_A):
        oh = o[2 * h:2 * h + 1, :] - lam_ref[0] * o[2 * h + 1:2 * h + 2, :]
        outs.append(_attn_finish(oh, sw, z[:, h * DA:(h + 1) * DA], out_scale))
    o_ref[...] = jnp.concatenate(outs, axis=1).astype(BF16)


def _attn_decode(page_table, lam, q, kn, cache_k, cache_v, ei, bias, p, sw, out_scale):
    nb, npg = page_table.shape
    page = cache_k.shape[2]
    ck = cache_k.reshape(cache_k.shape[0], cache_k.shape[1], page * H_A, DA)
    cv = cache_v.reshape(cache_v.shape[0], cache_v.shape[1], page * H_A, DA)
    r3 = lambda a: a.reshape(nb, 1, a.shape[-1])
    vec = lambda c: pl.BlockSpec((None, 1, W_A), lambda b, pt, c=c: (b, 0, c))
    cache = [pl.BlockSpec((None, None, page * H_A, DA), lambda b, pt, j=j: (ei, pt[b * npg + j], 0, 0))
             for j in range(npg)]
    out = pl.pallas_call(
        functools.partial(_attn_decode_kernel, page=page, npg=npg, out_scale=out_scale),
        out_shape=jax.ShapeDtypeStruct((nb, 1, W_A), BF16),
        grid_spec=pltpu.PrefetchScalarGridSpec(
            num_scalar_prefetch=1,
            grid=(nb,),
            in_specs=[pl.BlockSpec(memory_space=pltpu.SMEM), vec(0), vec(0), vec(0)] + cache + cache
                     + [pl.BlockSpec(bias.shape, lambda b, pt: (0, 0)),
                        vec(1),
                        pl.BlockSpec((1, DA), lambda b, pt: (0, 0))],
            out_specs=pl.BlockSpec((None, 1, W_A), lambda b, pt: (b, 0, 0))),
        compiler_params=_cparams("arbitrary"),
        name="attn_decode",
    )(page_table.reshape(-1), lam, r3(q), r3(kn), r3(p), *([ck] * npg), *([cv] * npg), bias, r3(p), sw)
    return out.reshape(nb, W_A)


def _shifted_conv(x, prev, w):
    r, c = x.shape
    taps = w.shape[0]
    x3 = x.reshape(r // 8, 8, c)
    sub = lax.broadcasted_iota(jnp.int32, x3.shape, 1)
    y = x3 * w[taps - 1:taps, :].reshape(1, 1, c)
    for s in range(1, taps):
        rot = pltpu.roll(x3, s, 1)
        before = jnp.concatenate([pltpu.roll(prev, s, 0)[None], rot[:-1]], axis=0)
        y = y + jnp.where(sub < s, before, rot) * w[taps - 1 - s:taps - s, :].reshape(1, 1, c)
    return y.reshape(r, c)


def _gates(ab, alog, dtb):
    x = ab + dtb
    sp = jnp.maximum(x, 0.0) + jnp.log(1.0 + jnp.exp(-jnp.abs(x)))
    return -jnp.exp(alog) * sp, _sigmoid(ab)


def _l2norm(x):
    return x * lax.rsqrt(jnp.sum(x * x, axis=-1, keepdims=True) + EPS)


def _rms(x, w):
    return x * lax.rsqrt(jnp.mean(x * x, axis=-1, keepdims=True) + EPS) * w


def _level_masks(c):
    i = lax.broadcasted_iota(jnp.int32, (c, c), 0)
    j = lax.broadcasted_iota(jnp.int32, (c, c), 1)
    masks = []
    s = 1
    while s < c:
        sh = s.bit_length() - 1
        masks.append((((i >> (sh + 1)) == (j >> (sh + 1))) & ((i >> sh) != (j >> sh)) & (i > j)).astype(F32))
        s *= 2
    return masks


def _gdn_prompt_kernel(cq_ref, ck_ref, cv_ref, zb_ref, ab_ref, alog_ref, dtb_ref, gnw_ref,
                       ob_ref, s_ref, g_ref, beta_ref, u_ref, wq_ref, ak_ref, gl_ref, *, rows, chunk, nseq):
    t = pl.program_id(1)
    c = chunk

    @pl.when(t == 0)
    def _():
        s_ref[...] = jnp.zeros(s_ref.shape, F32)

    ii = lax.broadcasted_iota(jnp.int32, (c, c), 0)
    jj = lax.broadcasted_iota(jnp.int32, (c, c), 1)
    incl = ii >= jj
    ltri = incl.astype(BF16)
    masks = _level_masks(c)

    def prep(b, ci):
        rs = pl.ds(pl.multiple_of(ci * c, c), c)
        gch = g_ref[b, rs, :]
        g1 = gch.astype(BF16)
        r1 = gch - g1.astype(F32)
        g2 = r1.astype(BF16)
        g3 = (r1 - g2.astype(F32)).astype(BF16)
        d = functools.partial(jnp.dot, preferred_element_type=F32)
        gc = d(ltri, g1) + d(ltri, g2) + d(ltri, g3)
        gct = jnp.concatenate([gc, gc], axis=0).T
        ge = jnp.exp(gc)
        kdec = jnp.exp(gc[c - 1:c, :] - gc)
        gl_ref[b, ci] = ge[c - 1:c, :]
        return rs, gc, gct, ge, kdec, beta_ref[b, rs, :]

    def local(b, ig):
        items = []
        for ci in [LOCAL_GROUP * ig + k for k in range(LOCAL_GROUP)]:
            rs, gc, gct, ge, kdec, bch = prep(b, ci)
            for h in range(H_B):
                hc = slice(h * DK_B, (h + 1) * DK_B)
                qh = _l2norm(cq_ref[b, rs, hc].astype(F32)) * (DK_B ** -0.5)
                kh = _l2norm(ck_ref[b, rs, hc].astype(F32))
                vh = cv_ref[b, rs, hc].astype(F32)
                bcol = bch[:, H_B + h:H_B + h + 1]
                gecol = ge[:, h:h + 1]
                decay = jnp.exp(jnp.where(incl, gc[:, h:h + 1] - gct[h:h + 1, :c], NEG))
                kbeta = kh * bcol
                wq_ref[b, ci, h, c:2 * c] = (qh * gecol).astype(BF16)
                ak_ref[b, ci, h, c:c + DK_B] = (kh * kdec[:, h:h + 1]).T.astype(BF16)
                items.append(dict(ci=ci, h=h, rs=rs, hc=hc, decay=decay, qk=(kbeta, qh, kh),
                                  rhs=jnp.concatenate([vh * bcol, kbeta * gecol], axis=1)))
        for it in items:
            kbeta, qh, kh = it.pop("qk")
            both = _dot_nt(jnp.concatenate([kbeta, qh], axis=0), kh)
            it["mm"] = jnp.where(ii > jj, both[:c] * it["decay"], 0.0)
            ak_ref[b, it["ci"], it["h"], 0:c] = (both[c:] * it["decay"]).astype(BF16)
        for it in items:
            it["pp"] = -(it["mm"] * masks[0])
        for mk in masks[1:]:
            for it in items:
                e = it["mm"] * mk
                it["x"] = e + _dot(it["pp"], e)
            for it in items:
                it["pp"] = it["pp"] - (it["x"] + _dot(it["x"], it["pp"]))
        for it in items:
            uw = it["rhs"] + _dot(it["pp"], it["rhs"])
            u_ref[b, it["rs"], it["hc"]] = uw[:, :DV_B]
            wq_ref[b, it["ci"], it["h"], 0:c] = uw[:, DV_B:].astype(BF16)

    for b in range(nseq):
        g, beta = _gates(ab_ref[b], alog_ref[...], dtb_ref[...])
        g_ref[b] = g
        beta_ref[b] = beta

        def local_b(ig, carry, b=b):
            local(b, ig)
            return carry

        lax.fori_loop(0, rows // (LOCAL_GROUP * c), local_b, 0)
    gnw = gnw_ref[...]

    def scan(ci, carry):
        rs = pl.ds(pl.multiple_of(ci * c, c), c)
        chains = [(b, h) for b in range(nseq) for h in range(H_B)]
        hc = lambda h: slice(h * DK_B, (h + 1) * DK_B)
        s_old = [s_ref[b, h] for b, h in chains]
        r = [jnp.dot(wq_ref[b, ci, h], s_old[k].astype(BF16), preferred_element_type=F32)
             for k, (b, h) in enumerate(chains)]
        v_new = [u_ref[b, rs, hc(h)] - r[k][:c] for k, (b, h) in enumerate(chains)]
        r2 = [jnp.dot(ak_ref[b, ci, h], v_new[k].astype(BF16), preferred_element_type=F32)
              for k, (b, h) in enumerate(chains)]
        for k, (b, h) in enumerate(chains):
            s_ref[b, h] = s_old[k] * gl_ref[b, ci][:, h:h + 1] + r2[k][c:]
            ob_ref[b, rs, hc(h)] = (_rms(r[k][c:] + r2[k][:c], gnw)
                                    * _silu(zb_ref[b, rs, hc(h)].astype(F32))).astype(BF16)
        return carry

    lax.fori_loop(0, rows // c, scan, 0)


def _gdn_prompt(p, ab, alog, dtb, gnw, nb, t, rows):
    m = p.shape[0]
    nt = t // rows
    c = math.gcd(GDN_CHUNK, t)
    nseq = 2 if nb % 2 == 0 else 1
    p3 = p.reshape(nb, t, p.shape[1])
    col = lambda cidx: pl.BlockSpec((nseq, rows, W_B), lambda b, i, cidx=cidx: (b, i, cidx))
    full = lambda a: pl.BlockSpec(a.shape, lambda b, i: (0,) * a.ndim)
    c0 = (2 * W_A) // W_B
    ob, s_new = pl.pallas_call(
        functools.partial(_gdn_prompt_kernel, rows=rows, chunk=c, nseq=nseq),
        out_shape=(jax.ShapeDtypeStruct((nb, t, W_B), BF16), jax.ShapeDtypeStruct((nb, H_B, DK_B, DV_B), F32)),
        grid=(nb // nseq, nt),
        in_specs=[col(c0), col(c0 + 1), col(c0 + 2), col(c0 + 3),
                  pl.BlockSpec((nseq, rows, LANE), lambda b, i: (b, i, 0)),
                  full(alog), full(dtb), full(gnw)],
        out_specs=(pl.BlockSpec((nseq, rows, W_B), lambda b, i: (b, i, 0)),
                   pl.BlockSpec((nseq, H_B, DK_B, DV_B), lambda b, i: (b, 0, 0, 0))),
        scratch_shapes=[pltpu.VMEM((nseq, rows, LANE), F32), pltpu.VMEM((nseq, rows, LANE), F32),
                        pltpu.VMEM((nseq, rows, W_B), F32),
                        pltpu.VMEM((nseq, rows // c, H_B, 2 * c, DK_B), BF16),
                        pltpu.VMEM((nseq, rows // c, H_B, c + DK_B, c), BF16),
                        pltpu.VMEM((nseq, rows // c, 1, LANE), F32)],
        compiler_params=_cparams("arbitrary", "arbitrary"),
        name="gdn_prompt",
    )(p3, p3, p3, p3, ab.reshape(nb, t, LANE), alog, dtb, gnw)
    return ob.reshape(m, W_B), s_new


def _gdn_decode_prep_kernel(pq_ref, pk_ref, pv_ref, ab_ref, c0_ref, cw_ref, alog_ref, dtb_ref,
                            q_ref, k_ref, v_ref, eg_ref, beta_ref, qk_ref, cn_ref):
    cw = cw_ref[...]
    taps = cw.shape[0]
    outs = (q_ref, k_ref, v_ref)
    for seg, ref in enumerate((pq_ref, pk_ref, pv_ref)):
        cols = slice(seg * W_B, (seg + 1) * W_B)
        x = ref[...].astype(F32)
        y = x * cw[taps - 1:taps, cols]
        for j in range(taps - 1):
            y = y + c0_ref[j, :, cols] * cw[j:j + 1, cols]
            if j >= 1:
                cn_ref[j - 1, :, cols] = c0_ref[j, :, cols]
        cn_ref[taps - 2, :, cols] = x
        outs[seg][...] = _silu(y)
    lane = lax.broadcasted_iota(jnp.int32, eg_ref.shape, 1)
    qk = jnp.zeros(eg_ref.shape, F32)
    for h in range(H_B):
        hc = slice(h * DK_B, (h + 1) * DK_B)
        qh = _l2norm(q_ref[:, hc]) * (DK_B ** -0.5)
        kh = _l2norm(k_ref[:, hc])
        q_ref[:, hc] = qh
        k_ref[:, hc] = kh
        qk = jnp.where(lane == h, jnp.sum(qh * kh, axis=-1, keepdims=True), qk)
    g, beta = _gates(ab_ref[...], alog_ref[...], dtb_ref[...])
    eg_ref[...] = jnp.exp(g)
    beta_ref[...] = beta
    qk_ref[...] = qk


def _gdn_decode_prep(p, ab, conv0_t, cw, alog, dtb):
    nb = p.shape[0]
    c0 = (2 * W_A) // W_B
    col = lambda cidx: pl.BlockSpec((nb, W_B), lambda i, cidx=cidx: (0, cidx))
    full = lambda a: pl.BlockSpec(a.shape, lambda i: (0,) * a.ndim)
    wide = jax.ShapeDtypeStruct((nb, W_B), F32)
    narrow = jax.ShapeDtypeStruct((nb, LANE), F32)
    ospec = lambda s: pl.BlockSpec(s.shape, lambda i: (0,) * len(s.shape))
    outs = (wide, wide, wide, narrow, narrow, narrow, jax.ShapeDtypeStruct(conv0_t.shape, F32))
    return pl.pallas_call(
        _gdn_decode_prep_kernel,
        out_shape=outs,
        grid=(1,),
        in_specs=[col(c0), col(c0 + 1), col(c0 + 2),
                  full(ab), full(conv0_t), full(cw), full(alog), full(dtb)],
        out_specs=tuple(ospec(s) for s in outs),
        compiler_params=_cparams("arbitrary"),
        name="gdn_decode_prep",
    )(p, p, p, ab, conv0_t, cw, alog, dtb)


def _gdn_decode_kernel(s_ref, qt_ref, kt_ref, v_ref, eg_ref, beta_ref, qk_ref, zb_ref, gnw_ref, *refs, bb, n_alias):
    so_ref, ob_ref, o_scr = refs[n_alias:]
    v = v_ref[...]
    eg = eg_ref[...]
    beta = beta_ref[...]
    qk = qk_ref[...]
    for h in range(H_B):
        hc = slice(h * DV_B, (h + 1) * DV_B)
        qt = qt_ref[h]
        kt = kt_ref[h]
        for i in range(bb):
            s_old = s_ref[i, h]
            kc = kt[:, i:i + 1]
            qc = qt[:, i:i + 1]
            egs = eg[i:i + 1, h:h + 1]
            ks = jnp.sum(s_old * kc, axis=0, keepdims=True)
            qs = jnp.sum(s_old * qc, axis=0, keepdims=True)
            v_new = beta[i:i + 1, H_B + h:H_B + h + 1] * (v[i:i + 1, hc] - egs * ks)
            o_scr[i:i + 1, hc] = egs * qs + qk[i:i + 1, h:h + 1] * v_new
            so_ref[i, h] = s_old * egs + kc * v_new
    gnw = gnw_ref[...]
    z = zb_ref[...]
    outs = [_rms(o_scr[:, h * DV_B:(h + 1) * DV_B], gnw) * _silu(z[:, h * DV_B:(h + 1) * DV_B].astype(F32))
            for h in range(H_B)]
    ob_ref[...] = jnp.concatenate(outs, axis=1).astype(BF16)


def _gdn_decode(state, ei, qt, kt, v, eg, beta, qk, p, gnw, bb, s_prev):
    nb = v.shape[0]
    ns = nb // bb
    narrow = pl.BlockSpec((bb, LANE), lambda i: (i, 0))
    tr = pl.BlockSpec((None, H_B, DK_B, bb), lambda i: (i, 0, 0, 0))
    zb0 = (2 * W_A + QKV_B) // W_B
    n_alias = 0 if s_prev is None else 1
    return pl.pallas_call(
        functools.partial(_gdn_decode_kernel, bb=bb, n_alias=n_alias),
        out_shape=(jax.ShapeDtypeStruct(state.shape, F32), jax.ShapeDtypeStruct((nb, W_B), BF16)),
        grid=(ns,),
        in_specs=[pl.BlockSpec((None, bb, H_B, DK_B, DV_B), lambda i: (ei, i, 0, 0, 0)),
                  tr, tr,
                  pl.BlockSpec((bb, W_B), lambda i: (i, 0)),
                  narrow, narrow, narrow,
                  pl.BlockSpec((bb, W_B), lambda i: (i, zb0)),
                  pl.BlockSpec((1, DV_B), lambda i: (0, 0))] + [pl.BlockSpec(memory_space=pl.ANY)] * n_alias,
        out_specs=(pl.BlockSpec((None, bb, H_B, DK_B, DV_B), lambda i: (ei, i, 0, 0, 0)),
                   pl.BlockSpec((bb, W_B), lambda i: (i, 0))),
        input_output_aliases={9: 0} if n_alias else {},
        scratch_shapes=[pltpu.VMEM((bb, W_B), F32)],
        compiler_params=_cparams("arbitrary"),
        name="gdn_decode",
    )(state, qt, kt, v, eg, beta, qk, p, gnw, *(() if s_prev is None else (s_prev,)))


def _out_proj_kernel(x_ref, oa_ref, ob_ref, w_ref, y_ref):
    y_ref[...] = (x_ref[...]
                  + jnp.dot(oa_ref[...], w_ref[0:W_A, :], preferred_element_type=F32)
                  + jnp.dot(ob_ref[...], w_ref[W_A:W_A + W_B, :], preferred_element_type=F32))


def _out_proj(x, oa, ob, w, tm):
    m, d = x.shape
    return pl.pallas_call(
        _out_proj_kernel,
        out_shape=jax.ShapeDtypeStruct((m, d), F32),
        grid=(m // tm,),
        in_specs=[pl.BlockSpec((tm, d), lambda i: (i, 0)),
                  pl.BlockSpec((tm, W_A), lambda i: (i, 0)),
                  pl.BlockSpec((tm, W_B), lambda i: (i, 0)),
                  pl.BlockSpec(w.shape, lambda i: (0, 0))],
        out_specs=pl.BlockSpec((tm, d), lambda i: (i, 0)),
        compiler_params=_cparams("arbitrary"),
        name="out_proj",
    )(x, oa, ob, w)


def _odd_prompt_kernel(x_ref, oa_ref, ob_ref, wp_ref, nw_ref, wi_ref, cw_ref, wo_ref, y_ref, sc_ref, carry_ref, *, rows):
    t = pl.program_id(1)
    d = x_ref.shape[1]

    @pl.when(t == 0)
    def _():
        carry_ref[...] = jnp.zeros(carry_ref.shape, F32)

    x = (x_ref[...]
         + jnp.dot(oa_ref[...], wp_ref[0:W_A, :], preferred_element_type=F32)
         + jnp.dot(ob_ref[...], wp_ref[W_A:W_A + W_B, :], preferred_element_type=F32))
    h = _rms_rows(x, nw_ref[...]).astype(BF16)
    proj = lambda c: jnp.dot(h, wi_ref[:, c * d:(c + 1) * d], preferred_element_type=F32)
    u = proj(1) * proj(2)
    cv = _shifted_conv(u, carry_ref[...], cw_ref[...])
    carry_ref[...] = u[rows - 8:rows]
    g = proj(0) * cv * _silu(proj(3))
    y_ref[...] = x + jnp.dot(g.astype(BF16), wo_ref[...], preferred_element_type=F32)

    @pl.when(t == pl.num_programs(1) - 1)
    def _():
        sc_ref[...] = u[rows - 8:rows]


def _odd_prompt(x, oa, ob, w_prev, nw, w_in, cw, w_out, nb, t, rows):
    m, d = x.shape
    nt = t // rows
    full = lambda a: pl.BlockSpec(a.shape, lambda b, i: (0,) * a.ndim)
    row = lambda n: pl.BlockSpec((rows, n), lambda b, i: (b * nt + i, 0))
    return pl.pallas_call(
        functools.partial(_odd_prompt_kernel, rows=rows),
        out_shape=(jax.ShapeDtypeStruct((m, d), F32), jax.ShapeDtypeStruct((nb, 8, d), F32)),
        grid=(nb, nt),
        in_specs=[row(d), row(W_A), row(W_B), full(w_prev), full(nw), full(w_in), full(cw), full(w_out)],
        out_specs=(row(d), pl.BlockSpec((None, 8, d), lambda b, i: (b, 0, 0))),
        scratch_shapes=[pltpu.VMEM((8, d), F32)],
        compiler_params=_cparams("arbitrary", "arbitrary"),
        name="odd_prompt",
    )(x, oa, ob, w_prev, nw, w_in, cw, w_out)


def _odd_decode_kernel(x_ref, bg_ref, cg_ref, hh_ref, z_ref, b0_ref, b1_ref, cw_ref, w_ref, y_ref, u_ref):
    cw = cw_ref[...]
    u = cg_ref[...].astype(F32) * hh_ref[...].astype(F32)
    cv = b0_ref[...] * cw[0:1, :] + b1_ref[...] * cw[1:2, :] + u * cw[2:3, :]
    g = bg_ref[...].astype(F32) * cv * _silu(z_ref[...].astype(F32))
    y_ref[...] = x_ref[...] + jnp.dot(g.astype(BF16), w_ref[...], preferred_element_type=F32)
    u_ref[...] = u


def _odd_decode(x, p, b0, b1, cw, w):
    m, d = x.shape
    col = lambda c: pl.BlockSpec((m, d), lambda i, c=c: (0, c))
    full = lambda a: pl.BlockSpec(a.shape, lambda i: (0,) * a.ndim)
    return pl.pallas_call(
        _odd_decode_kernel,
        out_shape=(jax.ShapeDtypeStruct((m, d), F32), jax.ShapeDtypeStruct((m, d), F32)),
        grid=(1,),
        in_specs=[full(x), col(0), col(1), col(2), col(3), full(b0), full(b1), full(cw), full(w)],
        out_specs=(pl.BlockSpec((m, d), lambda i: (0, 0)), pl.BlockSpec((m, d), lambda i: (0, 0))),
        compiler_params=_cparams("arbitrary"),
        name="odd_decode",
    )(x, p, p, p, p, b0, b1, cw, w)


def _tile(n, want):
    t = math.gcd(n, want)
    assert t == n or t % 8 == 0, (n, want)
    return t


def _pad_lanes(v):
    return jnp.pad(v.astype(F32), (0, LANE - v.shape[0])).reshape(1, LANE)


def kernel(x_prompt, x_sample, cache_k, cache_v, page_table, state_gdn, state_gdn_conv, state_shortconv, norm_w, rel_table, w_in_even, w_out_even, qn_w, kn_w, lam_q1, lam_k1, lam_q2, lam_k2, subln_w, gdn_conv_w, gdn_a_log, gdn_dt_bias, gdn_norm_w, w_in_odd, sc_conv_w, w_out_odd):
    nbp, t, d = x_prompt.shape
    nbs = x_sample.shape[0]
    page = cache_k.shape[2]
    assert x_sample.shape[1] == 1 and page >= MAX_DISTANCE and t % 8 == 0 and DEPTH % 2 == 0
    tq = _tile(t, 512)
    assert tq >= MAX_DISTANCE

    xp = x_prompt.reshape(nbp * t, d)
    xs = x_sample.reshape(nbs, d)

    g64 = jnp.arange(W_A) // D_HA
    bd = jnp.where(g64[:, None] == g64[None, :], 1.0 / D_HA, 0.0).astype(BF16)
    bias_p = _bias_tiles(rel_table.astype(F32), tq)
    bias_s = _bias_decode(rel_table.astype(F32), page)

    n_even = (DEPTH + 1) // 2
    kv_p = kv_s = s_s = None
    sp, gcp, scp = [], [], []
    gcs, scs = [], []
    ei = oi = 0
    for li in range(DEPTH):
        nw = norm_w[li].reshape(1, d)
        if li % 2 == 0:
            lambda_init = 0.8 - 0.6 * math.exp(-0.3 * li)
            w_in = w_in_even[ei][:, :P_MAIN].astype(BF16)
            w_ab = jnp.pad(w_in_even[ei][:, P_MAIN:], ((0, 0), (0, LANE - 2 * H_B))).astype(BF16)
            w_out = w_out_even[ei].astype(BF16)
            qw = jnp.tile(qn_w[ei], W_A // D_HA).reshape(1, W_A)
            kw = jnp.tile(kn_w[ei], W_A // D_HA).reshape(1, W_A)
            sw = subln_w[ei].reshape(1, DA)
            lam = (jnp.exp(jnp.sum(lam_q1[ei] * lam_k1[ei]).astype(F32))
                   - jnp.exp(jnp.sum(lam_q2[ei] * lam_k2[ei]).astype(F32)) + lambda_init).reshape(1)
            cw = gdn_conv_w[ei]
            alog = _pad_lanes(gdn_a_log[ei])
            dtb = _pad_lanes(gdn_dt_bias[ei])
            gnw = gdn_norm_w[ei].reshape(1, DV_B)

            qb, kb, *kv_p, p, ab, tail = _even_in(xp, nw, w_in, w_ab, qw, kw, bd, _tile(t, 512), ei, n_even, kv_p,
                                                  conv_w=cw, seq_len=t)
            oa = _attn_prompt(lam, qb, kb, bias_p, p, sw, nbp, t, tq, 1.0 - lambda_init)
            ob, s_new = _gdn_prompt(p, ab, alog, dtb, gnw, nbp, t, _tile(t, 512))
            prev_p = (oa, ob, w_out)
            sp.append(s_new)
            gcp.append(tail[:, 8 - (GDN_CONV - 1):, :])

            qb, kb, *kv_s, p, ab = _even_in(xs, nw, w_in, w_ab, qw, kw, bd, nbs, ei, n_even, kv_s)
            oa = _attn_decode(page_table, lam, qb, kb, cache_k, cache_v, ei, bias_s, p, sw, 1.0 - lambda_init)
            conv0_t = jnp.swapaxes(state_gdn_conv[ei], 0, 1)
            qn, kn, vv, eg, beta, qk, conv_new = _gdn_decode_prep(p, ab, conv0_t, cw, alog, dtb)
            bb = 8
            to_cols = lambda a: a.reshape(nbs // bb, bb, H_B, DK_B).transpose(0, 2, 3, 1)
            s_s, ob = _gdn_decode(state_gdn, ei, to_cols(qn), to_cols(kn), vv, eg, beta, qk, p, gnw, bb, s_s)
            xs = _out_proj(xs, oa, ob, w_out, nbs)
            gcs.append(jnp.swapaxes(conv_new, 0, 1))
            ei += 1
        else:
            w_in = w_in_odd[oi].astype(BF16)
            w_out = w_out_odd[oi].astype(BF16)
            cw = sc_conv_w[oi]
            xp, tail = _odd_prompt(xp, *prev_p, nw, w_in, cw, w_out, nbp, t, _tile(t, 512))
            scp.append(tail[:, 8 - (SC_WIDTH - 1):, :])

            p = _norm_proj(xs, nw, w_in, nbs, 1024)
            buf0 = state_shortconv[oi]
            xs, u = _odd_decode(xs, p, buf0[:, 0, :], buf0[:, 1, :], cw, w_out)
            scs.append(jnp.stack([buf0[:, 1, :], u], axis=1))
            oi += 1

    leaf_p = lambda a: a.reshape(n_even, nbp, t, H_A, DA)
    leaf_s = lambda a: a.reshape(n_even, nbs, 1, H_A, DA)
    return (xp.reshape(nbp, t, d), xs.reshape(nbs, 1, d),
            leaf_p(kv_p[0]), leaf_p(kv_p[1]), jnp.stack(sp), jnp.stack(gcp), jnp.stack(scp),
            leaf_s(kv_s[0]), leaf_s(kv_s[1]), s_s, jnp.stack(gcs), jnp.stack(scs))
```

```python
import functools
import math

import jax
import jax.numpy as jnp
from jax import lax
from jax.experimental import pallas as pl
from jax.experimental.pallas import tpu as pltpu

F32, BF16 = jnp.float32, jnp.bfloat16

DEPTH = 4
H_A, D_HA = 4, 64
DA = 2 * D_HA
W_A = H_A * DA
H_B, DK_B, DV_B = 4, 128, 128
W_B = H_B * DV_B
QKV_B = 2 * H_B * DK_B + H_B * DV_B
GDN_CONV, GDN_CHUNK, SC_WIDTH = 4, 64, 3
NUM_BUCKETS, MAX_EXACT, MAX_DISTANCE = 32, 16, 128
EPS, NEG = 1e-6, -1e30
LOG2E = math.log2(math.e)
LANE = 128
P_MAIN = 4 * W_A + QKV_B + W_B
VMEM_LIMIT = 48 * 1024 * 1024
LOCAL_GROUP = 4


def _layer_spec(w3, li, cols=None):
    return pl.BlockSpec((None, w3.shape[1], cols or w3.shape[2]), lambda *_: (li, 0, 0))


def _cparams(*sem):
    return pltpu.CompilerParams(dimension_semantics=sem, vmem_limit_bytes=VMEM_LIMIT)


def _silu(z):
    h = 0.5 * z
    return h + h * jnp.tanh(h)


def _sigmoid(z):
    return 1.0 / (1.0 + jnp.exp(-z))


def _dot(a, b):
    return jnp.dot(a.astype(BF16), b.astype(BF16), preferred_element_type=F32)


def _dot_nt(a, b):
    return lax.dot_general(a.astype(BF16), b.astype(BF16), (((1,), (1,)), ((), ())), preferred_element_type=F32)


def _rms_rows(x, w):
    return x * lax.rsqrt(jnp.mean(x * x, axis=-1, keepdims=True) + EPS) * w


def _norm_proj_kernel(x_ref, nw_ref, w_ref, o_ref, h_ref):
    @pl.when(pl.program_id(1) == 0)
    def _():
        h_ref[...] = _rms_rows(x_ref[...], nw_ref[...]).astype(BF16)

    o_ref[...] = jnp.dot(h_ref[...], w_ref[...], preferred_element_type=F32).astype(BF16)


def _norm_proj(x, nw, w3, li, tm, tn):
    m, d = x.shape
    n = w3.shape[2]
    return pl.pallas_call(
        _norm_proj_kernel,
        out_shape=jax.ShapeDtypeStruct((m, n), BF16),
        grid=(m // tm, n // tn),
        in_specs=[pl.BlockSpec((tm, d), lambda i, j: (i, 0)),
                  pl.BlockSpec((1, d), lambda i, j: (0, 0)),
                  pl.BlockSpec((None, d, tn), lambda i, j: (li, 0, j))],
        out_specs=pl.BlockSpec((tm, tn), lambda i, j: (i, j)),
        scratch_shapes=[pltpu.VMEM((tm, d), BF16)],
        compiler_params=_cparams("arbitrary", "arbitrary"),
        name="norm_proj",
    )(x, nw, w3)


def _even_in_kernel(x_ref, nw_ref, w_ref, wab_ref, qw_ref, kw_ref, bd_ref, *refs, n_alias, tiles_per_seq):
    if tiles_per_seq:
        cw_ref = refs[0]
        qo_ref, kb_ref, kf_ref, vf_ref, p_ref, ab_ref, tail_ref, carry_ref = refs[1 + n_alias:]
    else:
        qo_ref, kb_ref, kf_ref, vf_ref, p_ref, ab_ref = refs[n_alias:]
    tm = x_ref.shape[0]
    h = _rms_rows(x_ref[...], nw_ref[...]).astype(BF16)
    proj = lambda c: jnp.dot(h, w_ref[:, c * W_A:(c + 1) * W_A], preferred_element_type=F32)
    ab_ref[...] = jnp.dot(h, wab_ref[...], preferred_element_type=F32)
    bd = bd_ref[...]

    def group_norm(x, w):
        ms = jnp.dot((x * x).astype(BF16), bd, preferred_element_type=F32)
        return x * lax.rsqrt(ms + EPS) * w

    def leaf(ref, val):
        for hd in range(H_A):
            ref[pl.ds(hd, tm, stride=H_A), :] = val[:, hd * DA:(hd + 1) * DA]

    qn = group_norm(proj(0), qw_ref[...])
    qo_ref[...] = (qn * (D_HA ** -0.5 * LOG2E)).astype(BF16)
    kn = group_norm(proj(1), kw_ref[...])
    kb_ref[...] = kn.astype(BF16)
    leaf(kf_ref, kn)
    v = proj(2)
    leaf(vf_ref, v)
    p_ref[:, 0:W_A] = v.astype(BF16)
    if tiles_per_seq:
        first = pl.program_id(0) % tiles_per_seq == 0

        @pl.when(first)
        def _():
            carry_ref[...] = jnp.zeros(carry_ref.shape, F32)

    c_qkv = (4 * W_A) // W_A
    for c in range(3, P_MAIN // W_A):
        y = proj(c)
        seg = c - c_qkv
        if tiles_per_seq and 0 <= seg < QKV_B // W_B:
            cols = slice(seg * W_B, (seg + 1) * W_B)
            raw = y
            y = _silu(_shifted_conv(raw, carry_ref[:, cols], cw_ref[:, cols]))
            carry_ref[:, cols] = raw[tm - 8:tm]
            tail_ref[:, cols] = raw[tm - 8:tm]
        p_ref[:, (c - 2) * W_A:(c - 1) * W_A] = y.astype(BF16)


def _even_in(x, nw, w3, w_ab, qw, kw, bd, tm, ei, n_layers, kv_prev, conv_w=None, seq_len=None):
    m, d = x.shape
    tiles_per_seq = 0 if conv_w is None else seq_len // tm
    n_rest = P_MAIN - 2 * W_A
    full = lambda a: pl.BlockSpec(a.shape, lambda i: (0,) * a.ndim)
    row = lambda n: pl.BlockSpec((tm, n), lambda i: (i, 0))
    leaf = pl.BlockSpec((None, tm * H_A, DA), lambda i: (ei, i, 0))
    leaf_shape = jax.ShapeDtypeStruct((n_layers, m * H_A, DA), F32)
    n_alias = 0 if kv_prev is None else 2
    out_shape = [jax.ShapeDtypeStruct((m, W_A), BF16), jax.ShapeDtypeStruct((m, W_A), BF16), leaf_shape, leaf_shape,
                 jax.ShapeDtypeStruct((m, n_rest), BF16), jax.ShapeDtypeStruct((m, LANE), F32)]
    out_specs = [row(W_A), row(W_A), leaf, leaf, row(n_rest), row(LANE)]
    in_specs = [row(d), full(nw), _layer_spec(w3, ei, P_MAIN), full(w_ab), full(qw), full(kw), full(bd)]
    args = [x, nw, w3, w_ab, qw, kw, bd]
    scratch = []
    if tiles_per_seq:
        in_specs.append(full(conv_w))
        args.append(conv_w)
        out_shape.append(jax.ShapeDtypeStruct((m // seq_len, 8, QKV_B), F32))
        out_specs.append(pl.BlockSpec((None, 8, QKV_B), lambda i: (i // tiles_per_seq, 0, 0)))
        scratch.append(pltpu.VMEM((8, QKV_B), F32))
    n_in = len(args)
    return pl.pallas_call(
        functools.partial(_even_in_kernel, n_alias=n_alias, tiles_per_seq=tiles_per_seq),
        out_shape=tuple(out_shape),
        grid=(m // tm,),
        in_specs=in_specs + [pl.BlockSpec(memory_space=pl.ANY)] * n_alias,
        out_specs=tuple(out_specs),
        input_output_aliases={} if kv_prev is None else {n_in: 2, n_in + 1: 3},
        scratch_shapes=scratch,
        compiler_params=_cparams("arbitrary"),
        name="even_in",
    )(*args, *(kv_prev or ()))


def _t5_bias(n, tab_ref, h):
    nf = jnp.maximum(n, 1).astype(F32)
    large = MAX_EXACT + (jnp.log(nf / MAX_EXACT) / math.log(MAX_DISTANCE / MAX_EXACT)
                         * (NUM_BUCKETS - MAX_EXACT)).astype(jnp.int32)
    large = jnp.minimum(large, NUM_BUCKETS - 1)
    bkt = jnp.where(n < MAX_EXACT, n, large)
    out = jnp.zeros(n.shape, F32)
    for b in range(NUM_BUCKETS):
        out = jnp.where(bkt == b, tab_ref[b, h], out)
    return (out - tab_ref[NUM_BUCKETS - 1, h]) * LOG2E


def _bias_tiles_kernel(tab_ref, o_ref, *, tq):
    h = pl.program_id(0)
    i = lax.broadcasted_iota(jnp.int32, (tq, tq), 0)
    j = lax.broadcasted_iota(jnp.int32, (tq, tq), 1)
    n0 = i - j
    o_ref[0, 0] = jnp.where(n0 >= 0, _t5_bias(jnp.maximum(n0, 0), tab_ref, h), NEG)
    o_ref[0, 1] = _t5_bias(n0 + tq, tab_ref, h)


def _bias_tiles(rel_table, tq):
    return pl.pallas_call(
        functools.partial(_bias_tiles_kernel, tq=tq),
        out_shape=jax.ShapeDtypeStruct((H_A, 2, tq, tq), F32),
        grid=(H_A,),
        in_specs=[pl.BlockSpec(memory_space=pltpu.SMEM)],
        out_specs=pl.BlockSpec((1, 2, tq, tq), lambda h: (h, 0, 0, 0)),
        compiler_params=_cparams("arbitrary"),
        name="bias_tiles",
    )(rel_table)


def _bias_decode_kernel(tab_ref, o_ref, *, page):
    w = page * H_A
    row = lax.broadcasted_iota(jnp.int32, (2 * H_A, w), 0)
    lane = lax.broadcasted_iota(jnp.int32, (2 * H_A, w), 1)
    valid = (lane & (H_A - 1)) == (row >> 1)
    n = page - (lane >> 2)
    near = jnp.zeros((2 * H_A, w), F32)
    new = jnp.zeros((2 * H_A, LANE), F32)
    for h in range(H_A):
        near = jnp.where((row >> 1) == h, _t5_bias(n, tab_ref, h), near)
        new = jnp.where((row[:, :LANE] >> 1) == h, _t5_bias(jnp.zeros((2 * H_A, LANE), jnp.int32), tab_ref, h), new)
    o_ref[:, 0:w] = jnp.where(valid, 0.0, NEG)
    o_ref[:, w:2 * w] = jnp.where(valid, near, NEG)
    o_ref[:, 2 * w:2 * w + LANE] = new


def _bias_decode(rel_table, page):
    assert H_A == 4
    return pl.pallas_call(
        functools.partial(_bias_decode_kernel, page=page),
        out_shape=jax.ShapeDtypeStruct((2 * H_A, 2 * page * H_A + LANE), F32),
        in_specs=[pl.BlockSpec(memory_space=pltpu.SMEM)],
        out_specs=pl.BlockSpec(memory_space=pltpu.VMEM),
        name="bias_decode",
    )(rel_table)


def _attn_finish(o, sw, z, out_scale):
    ms = jnp.mean(o * o, axis=-1, keepdims=True)
    return o * lax.rsqrt(ms + EPS) * sw * out_scale * _silu(z.astype(F32))


def _attn_prompt_kernel(lam_ref, q_ref, k_ref, v_ref, bias_ref, za_ref, sw_ref, o_ref, m, l, a, *, tq, out_scale):
    reps = tq // LANE
    lane = lax.broadcasted_iota(jnp.int32, (tq, DA), 1)

    def q_block(qi, carry):
        qrows = pl.ds(pl.multiple_of(qi * tq, tq), tq)
        q = q_ref[qrows, :].astype(F32)
        q2 = jnp.concatenate([jnp.where(lane < D_HA, q, 0.0), jnp.where(lane >= D_HA, q, 0.0)], axis=0).astype(BF16)
        m[...] = jnp.full(m.shape, NEG, F32)
        l[...] = jnp.zeros(l.shape, F32)
        a[...] = jnp.zeros(a.shape, F32)

        def step(j, bias):
            rows = pl.ds(pl.multiple_of(j * tq, tq), tq)
            s = lax.dot_general(q2, k_ref[rows, :], (((1,), (1,)), ((), ())), preferred_element_type=F32)
            if bias is not None:
                s = s + jnp.concatenate([bias, bias], axis=0)
            m_prev = m[...]
            m_new = jnp.maximum(m_prev, jnp.max(s, axis=1, keepdims=True))
            p = jnp.exp2(s - jnp.concatenate([m_new] * reps, axis=1))
            alpha = jnp.exp2(m_prev - m_new)
            l[...] = alpha * l[...] + jnp.sum(p, axis=1, keepdims=True)
            a[...] = alpha * a[...] + jnp.dot(p.astype(BF16), v_ref[rows, :], preferred_element_type=F32)
            m[...] = m_new

        def far(j, c):
            step(j, None)
            return c

        lax.fori_loop(0, jnp.maximum(qi - 1, 0), far, 0)

        @pl.when(qi >= 1)
        def _():
            step(qi - 1, bias_ref[0, 1])

        step(qi, bias_ref[0, 0])
        o = a[...] / l[...]
        o = o[:tq] - lam_ref[0] * o[tq:]
        o_ref[qrows, :] = _attn_finish(o, sw_ref[...], za_ref[qrows, :], out_scale).astype(BF16)
        return carry

    lax.fori_loop(0, q_ref.shape[0] // tq, q_block, 0)


def _attn_prompt(lam, q, k, bias, p, sw, nb, t, tq, out_scale):
    m = q.shape[0]
    za0 = W_A // DA
    v0 = 0
    seq = lambda c0: pl.BlockSpec((t, DA), lambda b, h, c0=c0: (b, c0 + h))
    return pl.pallas_call(
        functools.partial(_attn_prompt_kernel, tq=tq, out_scale=out_scale),
        out_shape=jax.ShapeDtypeStruct((m, W_A), BF16),
        grid=(nb, H_A),
        in_specs=[pl.BlockSpec(memory_space=pltpu.SMEM),
                  seq(0), seq(0), seq(v0),
                  pl.BlockSpec((1, 2, tq, tq), lambda b, h: (h, 0, 0, 0)),
                  seq(za0),
                  pl.BlockSpec((1, DA), lambda b, h: (0, 0))],
        out_specs=seq(0),
        scratch_shapes=[pltpu.VMEM((2 * tq, LANE), F32), pltpu.VMEM((2 * tq, LANE), F32),
                        pltpu.VMEM((2 * tq, DA), F32)],
        compiler_params=_cparams("arbitrary", "arbitrary"),
        name="attn_prompt",
    )(lam, q, k, p, bias, p, sw)


def _attn_decode_kernel(pt_ref, lam_ref, q_ref, kn_ref, vn_ref, *refs, page, npg, out_scale):
    ck_refs, cv_refs = refs[:npg], refs[npg:2 * npg]
    bias_ref, za_ref, sw_ref, o_ref = refs[2 * npg:]
    rows, w = 2 * H_A, page * H_A
    row = lax.broadcasted_iota(jnp.int32, (rows, DA), 0)
    lane = lax.broadcasted_iota(jnp.int32, (rows, DA), 1)

    def head_rows(x):
        out = jnp.zeros((rows, DA), F32)
        for h in range(H_A):
            out = jnp.where((row >> 1) == h, jnp.broadcast_to(x[:, h * DA:(h + 1) * DA], (rows, DA)), out)
        return out

    q8 = jnp.where((lane >> 6) == (row & 1), head_rows(q_ref[...].astype(F32)), 0.0)
    q8b = q8.astype(BF16)
    s_all = []
    for j in range(npg):
        s = lax.dot_general(q8b, ck_refs[j][...].astype(BF16), (((1,), (1,)), ((), ())), preferred_element_type=F32)
        s_all.append(s + (bias_ref[:, w:2 * w] if j == npg - 1 else bias_ref[:, 0:w]))
    s_new = (jnp.sum(q8 * head_rows(kn_ref[...].astype(F32)), axis=1, keepdims=True)
             + bias_ref[:, 2 * w:2 * w + 1])
    m = s_all[0]
    for s in s_all[1:]:
        m = jnp.maximum(m, s)
    m = jnp.maximum(jnp.max(m, axis=1, keepdims=True), s_new)
    p_new = jnp.exp2(s_new - m)
    l = p_new
    acc = p_new * head_rows(vn_ref[...].astype(F32))
    for j in range(npg):
        p = jnp.exp2(s_all[j] - m)
        l = l + jnp.sum(p, axis=1, keepdims=True)
        acc = acc + jnp.dot(p.astype(BF16), cv_refs[j][...].astype(BF16), preferred_element_type=F32)
    o = acc / l
    sw = sw_ref[...]
    z = za_ref[...]
    outs = []
    for h in range(H_A):
        oh = o[2 * h:2 * h + 1, :] - lam_ref[0] * o[2 * h + 1:2 * h + 2, :]
        outs.append(_attn_finish(oh, sw, z[:, h * DA:(h + 1) * DA], out_scale))
    o_ref[...] = jnp.concatenate(outs, axis=1).astype(BF16)


def _attn_decode(page_table, lam, q, kn, cache_k, cache_v, ei, bias, p, sw, out_scale):
    nb, npg = page_table.shape
    page = cache_k.shape[2]
    ck = cache_k.reshape(cache_k.shape[0], cache_k.shape[1], page * H_A, DA)
    cv = cache_v.reshape(cache_v.shape[0], cache_v.shape[1], page * H_A, DA)
    r3 = lambda a: a.reshape(nb, 1, a.shape[-1])
    vec = lambda c: pl.BlockSpec((None, 1, W_A), lambda b, pt, c=c: (b, 0, c))
    cache = [pl.BlockSpec((None, None, page * H_A, DA), lambda b, pt, j=j: (ei, pt[b * npg + j], 0, 0))
             for j in range(npg)]
    out = pl.pallas_call(
        functools.partial(_attn_decode_kernel, page=page, npg=npg, out_scale=out_scale),
        out_shape=jax.ShapeDtypeStruct((nb, 1, W_A), BF16),
        grid_spec=pltpu.PrefetchScalarGridSpec(
            num_scalar_prefetch=1,
            grid=(nb,),
            in_specs=[pl.BlockSpec(memory_space=pltpu.SMEM), vec(0), vec(0), vec(0)] + cache + cache
                     + [pl.BlockSpec(bias.shape, lambda b, pt: (0, 0)),
                        vec(1),
                        pl.BlockSpec((1, DA), lambda b, pt: (0, 0))],
            out_specs=pl.BlockSpec((None, 1, W_A), lambda b, pt: (b, 0, 0))),
        compiler_params=_cparams("arbitrary"),
        name="attn_decode",
    )(page_table.reshape(-1), lam, r3(q), r3(kn), r3(p), *([ck] * npg), *([cv] * npg), bias, r3(p), sw)
    return out.reshape(nb, W_A)


def _shifted_conv(x, prev, w):
    r, c = x.shape
    taps = w.shape[0]
    x3 = x.reshape(r // 8, 8, c)
    sub = lax.broadcasted_iota(jnp.int32, x3.shape, 1)
    y = x3 * w[taps - 1:taps, :].reshape(1, 1, c)
    for s in range(1, taps):
        rot = pltpu.roll(x3, s, 1)
        before = jnp.concatenate([pltpu.roll(prev, s, 0)[None], rot[:-1]], axis=0)
        y = y + jnp.where(sub < s, before, rot) * w[taps - 1 - s:taps - s, :].reshape(1, 1, c)
    return y.reshape(r, c)


def _gates(ab, alog, dtb):
    x = ab + dtb
    sp = jnp.maximum(x, 0.0) + jnp.log(1.0 + jnp.exp(-jnp.abs(x)))
    return -jnp.exp(alog) * sp, _sigmoid(ab)


def _l2norm(x):
    return x * lax.rsqrt(jnp.sum(x * x, axis=-1, keepdims=True) + EPS)


def _rms(x, w):
    return x * lax.rsqrt(jnp.mean(x * x, axis=-1, keepdims=True) + EPS) * w


def _level_masks(c):
    i = lax.broadcasted_iota(jnp.int32, (c, c), 0)
    j = lax.broadcasted_iota(jnp.int32, (c, c), 1)
    masks = []
    s = 1
    while s < c:
        sh = s.bit_length() - 1
        masks.append((((i >> (sh + 1)) == (j >> (sh + 1))) & ((i >> sh) != (j >> sh)) & (i > j)).astype(F32))
        s *= 2
    return masks


def _gdn_prompt_kernel(cq_ref, ck_ref, cv_ref, zb_ref, ab_ref, alog_ref, dtb_ref, gnw_ref,
                       ob_ref, s_ref, g_ref, beta_ref, u_ref, wq_ref, ak_ref, gl_ref, *, rows, chunk, nseq):
    t = pl.program_id(1)
    c = chunk

    @pl.when(t == 0)
    def _():
        s_ref[...] = jnp.zeros(s_ref.shape, F32)

    ii = lax.broadcasted_iota(jnp.int32, (c, c), 0)
    jj = lax.broadcasted_iota(jnp.int32, (c, c), 1)
    incl = ii >= jj
    ltri = incl.astype(BF16)
    masks = _level_masks(c)

    def prep(b, ci):
        rs = pl.ds(pl.multiple_of(ci * c, c), c)
        gch = g_ref[b, rs, :]
        g1 = gch.astype(BF16)
        r1 = gch - g1.astype(F32)
        g2 = r1.astype(BF16)
        g3 = (r1 - g2.astype(F32)).astype(BF16)
        d = functools.partial(jnp.dot, preferred_element_type=F32)
        gc = d(ltri, g1) + d(ltri, g2) + d(ltri, g3)
        gct = jnp.concatenate([gc, gc], axis=0).T
        ge = jnp.exp(gc)
        kdec = jnp.exp(gc[c - 1:c, :] - gc)
        gl_ref[b, ci] = ge[c - 1:c, :]
        return rs, gc, gct, ge, kdec, beta_ref[b, rs, :]

    def local(b, ig):
        items = []
        for ci in [LOCAL_GROUP * ig + k for k in range(LOCAL_GROUP)]:
            rs, gc, gct, ge, kdec, bch = prep(b, ci)
            for h in range(H_B):
                hc = slice(h * DK_B, (h + 1) * DK_B)
                qh = _l2norm(cq_ref[b, rs, hc].astype(F32)) * (DK_B ** -0.5)
                kh = _l2norm(ck_ref[b, rs, hc].astype(F32))
                vh = cv_ref[b, rs, hc].astype(F32)
                bcol = bch[:, H_B + h:H_B + h + 1]
                gecol = ge[:, h:h + 1]
                decay = jnp.exp(jnp.where(incl, gc[:, h:h + 1] - gct[h:h + 1, :c], NEG))
                kbeta = kh * bcol
                wq_ref[b, ci, h, c:2 * c] = (qh * gecol).astype(BF16)
                ak_ref[b, ci, h, c:c + DK_B] = (kh * kdec[:, h:h + 1]).T.astype(BF16)
                items.append(dict(ci=ci, h=h, rs=rs, hc=hc, decay=decay, qk=(kbeta, qh, kh),
                                  rhs=jnp.concatenate([vh * bcol, kbeta * gecol], axis=1)))
        for it in items:
            kbeta, qh, kh = it.pop("qk")
            both = _dot_nt(jnp.concatenate([kbeta, qh], axis=0), kh)
            it["mm"] = jnp.where(ii > jj, both[:c] * it["decay"], 0.0)
            ak_ref[b, it["ci"], it["h"], 0:c] = (both[c:] * it["decay"]).astype(BF16)
        for it in items:
            it["pp"] = -(it["mm"] * masks[0])
        for mk in masks[1:]:
            for it in items:
                e = it["mm"] * mk
                it["x"] = e + _dot(it["pp"], e)
            for it in items:
                it["pp"] = it["pp"] - (it["x"] + _dot(it["x"], it["pp"]))
        for it in items:
            uw = it["rhs"] + _dot(it["pp"], it["rhs"])
            u_ref[b, it["rs"], it["hc"]] = uw[:, :DV_B]
            wq_ref[b, it["ci"], it["h"], 0:c] = uw[:, DV_B:].astype(BF16)

    for b in range(nseq):
        g, beta = _gates(ab_ref[b], alog_ref[...], dtb_ref[...])
        g_ref[b] = g
        beta_ref[b] = beta

        def local_b(ig, carry, b=b):
            local(b, ig)
            return carry

        lax.fori_loop(0, rows // (LOCAL_GROUP * c), local_b, 0)
    gnw = gnw_ref[...]

    def scan(ci, carry):
        rs = pl.ds(pl.multiple_of(ci * c, c), c)
        chains = [(b, h) for b in range(nseq) for h in range(H_B)]
        hc = lambda h: slice(h * DK_B, (h + 1) * DK_B)
        s_old = [s_ref[b, h] for b, h in chains]
        r = [jnp.dot(wq_ref[b, ci, h], s_old[k].astype(BF16), preferred_element_type=F32)
             for k, (b, h) in enumerate(chains)]
        v_new = [u_ref[b, rs, hc(h)] - r[k][:c] for k, (b, h) in enumerate(chains)]
        r2 = [jnp.dot(ak_ref[b, ci, h], v_new[k].astype(BF16), preferred_element_type=F32)
              for k, (b, h) in enumerate(chains)]
        for k, (b, h) in enumerate(chains):
            s_ref[b, h] = s_old[k] * gl_ref[b, ci][:, h:h + 1] + r2[k][c:]
            ob_ref[b, rs, hc(h)] = (_rms(r[k][c:] + r2[k][:c], gnw)
                                    * _silu(zb_ref[b, rs, hc(h)].astype(F32))).astype(BF16)
        return carry

    lax.fori_loop(0, rows // c, scan, 0)


def _gdn_prompt(p, ab, alog, dtb, gnw, nb, t, rows):
    m = p.shape[0]
    nt = t // rows
    c = math.gcd(GDN_CHUNK, t)
    nseq = 2 if nb % 2 == 0 else 1
    p3 = p.reshape(nb, t, p.shape[1])
    col = lambda cidx: pl.BlockSpec((nseq, rows, W_B), lambda b, i, cidx=cidx: (b, i, cidx))
    full = lambda a: pl.BlockSpec(a.shape, lambda b, i: (0,) * a.ndim)
    c0 = (2 * W_A) // W_B
    ob, s_new = pl.pallas_call(
        functools.partial(_gdn_prompt_kernel, rows=rows, chunk=c, nseq=nseq),
        out_shape=(jax.ShapeDtypeStruct((nb, t, W_B), BF16), jax.ShapeDtypeStruct((nb, H_B, DK_B, DV_B), F32)),
        grid=(nb // nseq, nt),
        in_specs=[col(c0), col(c0 + 1), col(c0 + 2), col(c0 + 3),
                  pl.BlockSpec((nseq, rows, LANE), lambda b, i: (b, i, 0)),
                  full(alog), full(dtb), full(gnw)],
        out_specs=(pl.BlockSpec((nseq, rows, W_B), lambda b, i: (b, i, 0)),
                   pl.BlockSpec((nseq, H_B, DK_B, DV_B), lambda b, i: (b, 0, 0, 0))),
        scratch_shapes=[pltpu.VMEM((nseq, rows, LANE), F32), pltpu.VMEM((nseq, rows, LANE), F32),
                        pltpu.VMEM((nseq, rows, W_B), F32),
                        pltpu.VMEM((nseq, rows // c, H_B, 2 * c, DK_B), BF16),
                        pltpu.VMEM((nseq, rows // c, H_B, c + DK_B, c), BF16),
                        pltpu.VMEM((nseq, rows // c, 1, LANE), F32)],
        compiler_params=_cparams("arbitrary", "arbitrary"),
        name="gdn_prompt",
    )(p3, p3, p3, p3, ab.reshape(nb, t, LANE), alog, dtb, gnw)
    return ob.reshape(m, W_B), s_new


def _gdn_decode_prep_kernel(pq_ref, pk_ref, pv_ref, ab_ref, c0_ref, cw_ref, alog_ref, dtb_ref,
                            q_ref, k_ref, v_ref, eg_ref, beta_ref, qk_ref, cn_ref):
    cw = cw_ref[...]
    taps = cw.shape[0]
    outs = (q_ref, k_ref, v_ref)
    for seg, ref in enumerate((pq_ref, pk_ref, pv_ref)):
        cols = slice(seg * W_B, (seg + 1) * W_B)
        x = ref[...].astype(F32)
        y = x * cw[taps - 1:taps, cols]
        for j in range(taps - 1):
            y = y + c0_ref[j, :, cols] * cw[j:j + 1, cols]
            if j >= 1:
                cn_ref[j - 1, :, cols] = c0_ref[j, :, cols]
        cn_ref[taps - 2, :, cols] = x
        outs[seg][...] = _silu(y)
    lane = lax.broadcasted_iota(jnp.int32, eg_ref.shape, 1)
    qk = jnp.zeros(eg_ref.shape, F32)
    for h in range(H_B):
        hc = slice(h * DK_B, (h + 1) * DK_B)
        qh = _l2norm(q_ref[:, hc]) * (DK_B ** -0.5)
        kh = _l2norm(k_ref[:, hc])
        q_ref[:, hc] = qh
        k_ref[:, hc] = kh
        qk = jnp.where(lane == h, jnp.sum(qh * kh, axis=-1, keepdims=True), qk)
    g, beta = _gates(ab_ref[...], alog_ref[...], dtb_ref[...])
    eg_ref[...] = jnp.exp(g)
    beta_ref[...] = beta
    qk_ref[...] = qk


def _gdn_decode_prep(p, ab, conv0_t, cw, alog, dtb):
    nb = p.shape[0]
    c0 = (2 * W_A) // W_B
    col = lambda cidx: pl.BlockSpec((nb, W_B), lambda i, cidx=cidx: (0, cidx))
    full = lambda a: pl.BlockSpec(a.shape, lambda i: (0,) * a.ndim)
    wide = jax.ShapeDtypeStruct((nb, W_B), F32)
    narrow = jax.ShapeDtypeStruct((nb, LANE), F32)
    ospec = lambda s: pl.BlockSpec(s.shape, lambda i: (0,) * len(s.shape))
    outs = (wide, wide, wide, narrow, narrow, narrow, jax.ShapeDtypeStruct(conv0_t.shape, F32))
    return pl.pallas_call(
        _gdn_decode_prep_kernel,
        out_shape=outs,
        grid=(1,),
        in_specs=[col(c0), col(c0 + 1), col(c0 + 2),
                  full(ab), full(conv0_t), full(cw), full(alog), full(dtb)],
        out_specs=tuple(ospec(s) for s in outs),
        compiler_params=_cparams("arbitrary"),
        name="gdn_decode_prep",
    )(p, p, p, ab, conv0_t, cw, alog, dtb)


def _gdn_decode_kernel(s_ref, qt_ref, kt_ref, v_ref, eg_ref, beta_ref, qk_ref, zb_ref, gnw_ref, *refs, bb, n_alias):
    so_ref, ob_ref, o_scr = refs[n_alias:]
    v = v_ref[...]
    eg = eg_ref[...]
    beta = beta_ref[...]
    qk = qk_ref[...]
    for h in range(H_B):
        hc = slice(h * DV_B, (h + 1) * DV_B)
        qt = qt_ref[h]
        kt = kt_ref[h]
        for i in range(bb):
            s_old = s_ref[i, h]
            kc = kt[:, i:i + 1]
            qc = qt[:, i:i + 1]
            egs = eg[i:i + 1, h:h + 1]
            ks = jnp.sum(s_old * kc, axis=0, keepdims=True)
            qs = jnp.sum(s_old * qc, axis=0, keepdims=True)
            v_new = beta[i:i + 1, H_B + h:H_B + h + 1] * (v[i:i + 1, hc] - egs * ks)
            o_scr[i:i + 1, hc] = egs * qs + qk[i:i + 1, h:h + 1] * v_new
            so_ref[i, h] = s_old * egs + kc * v_new
    gnw = gnw_ref[...]
    z = zb_ref[...]
    outs = [_rms(o_scr[:, h * DV_B:(h + 1) * DV_B], gnw) * _silu(z[:, h * DV_B:(h + 1) * DV_B].astype(F32))
            for h in range(H_B)]
    ob_ref[...] = jnp.concatenate(outs, axis=1).astype(BF16)


def _gdn_decode(state, ei, qt, kt, v, eg, beta, qk, p, gnw, bb, s_prev):
    nb = v.shape[0]
    ns = nb // bb
    narrow = pl.BlockSpec((bb, LANE), lambda i: (i, 0))
    tr = pl.BlockSpec((None, H_B, DK_B, bb), lambda i: (i, 0, 0, 0))
    zb0 = (2 * W_A + QKV_B) // W_B
    n_alias = 0 if s_prev is None else 1
    return pl.pallas_call(
        functools.partial(_gdn_decode_kernel, bb=bb, n_alias=n_alias),
        out_shape=(jax.ShapeDtypeStruct(state.shape, F32), jax.ShapeDtypeStruct((nb, W_B), BF16)),
        grid=(ns,),
        in_specs=[pl.BlockSpec((None, bb, H_B, DK_B, DV_B), lambda i: (ei, i, 0, 0, 0)),
                  tr, tr,
                  pl.BlockSpec((bb, W_B), lambda i: (i, 0)),
                  narrow, narrow, narrow,
                  pl.BlockSpec((bb, W_B), lambda i: (i, zb0)),
                  pl.BlockSpec((1, DV_B), lambda i: (0, 0))] + [pl.BlockSpec(memory_space=pl.ANY)] * n_alias,
        out_specs=(pl.BlockSpec((None, bb, H_B, DK_B, DV_B), lambda i: (ei, i, 0, 0, 0)),
                   pl.BlockSpec((bb, W_B), lambda i: (i, 0))),
        input_output_aliases={9: 0} if n_alias else {},
        scratch_shapes=[pltpu.VMEM((bb, W_B), F32)],
        compiler_params=_cparams("arbitrary"),
        name="gdn_decode",
    )(state, qt, kt, v, eg, beta, qk, p, gnw, *(() if s_prev is None else (s_prev,)))


def _out_proj_kernel(x_ref, oa_ref, ob_ref, w_ref, y_ref):
    y_ref[...] = (x_ref[...]
                  + jnp.dot(oa_ref[...], w_ref[0:W_A, :], preferred_element_type=F32)
                  + jnp.dot(ob_ref[...], w_ref[W_A:W_A + W_B, :], preferred_element_type=F32))


def _out_proj(x, oa, ob, w3, li, tm):
    m, d = x.shape
    return pl.pallas_call(
        _out_proj_kernel,
        out_shape=jax.ShapeDtypeStruct((m, d), F32),
        grid=(m // tm,),
        in_specs=[pl.BlockSpec((tm, d), lambda i: (i, 0)),
                  pl.BlockSpec((tm, W_A), lambda i: (i, 0)),
                  pl.BlockSpec((tm, W_B), lambda i: (i, 0)),
                  _layer_spec(w3, li)],
        out_specs=pl.BlockSpec((tm, d), lambda i: (i, 0)),
        compiler_params=_cparams("arbitrary"),
        name="out_proj",
    )(x, oa, ob, w3)


def _odd_prompt_kernel(x_ref, oa_ref, ob_ref, wp_ref, nw_ref, wi_ref, cw_ref, wo_ref, y_ref, sc_ref, carry_ref, *, rows):
    t = pl.program_id(1)
    d = x_ref.shape[1]

    @pl.when(t == 0)
    def _():
        carry_ref[...] = jnp.zeros(carry_ref.shape, F32)

    x = (x_ref[...]
         + jnp.dot(oa_ref[...], wp_ref[0:W_A, :], preferred_element_type=F32)
         + jnp.dot(ob_ref[...], wp_ref[W_A:W_A + W_B, :], preferred_element_type=F32))
    h = _rms_rows(x, nw_ref[...]).astype(BF16)
    proj = lambda c: jnp.dot(h, wi_ref[:, c * d:(c + 1) * d], preferred_element_type=F32)
    u = proj(1) * proj(2)
    cv = _shifted_conv(u, carry_ref[...], cw_ref[...])
    carry_ref[...] = u[rows - 8:rows]
    g = proj(0) * cv * _silu(proj(3))
    y_ref[...] = x + jnp.dot(g.astype(BF16), wo_ref[...], preferred_element_type=F32)

    @pl.when(t == pl.num_programs(1) - 1)
    def _():
        sc_ref[...] = u[rows - 8:rows]


def _odd_prompt(x, oa, ob, w_prev3, ei, nw, w_in3, cw, w_out3, oi, nb, t, rows):
    m, d = x.shape
    nt = t // rows
    full = lambda a: pl.BlockSpec(a.shape, lambda b, i: (0,) * a.ndim)
    row = lambda n: pl.BlockSpec((rows, n), lambda b, i: (b * nt + i, 0))
    return pl.pallas_call(
        functools.partial(_odd_prompt_kernel, rows=rows),
        out_shape=(jax.ShapeDtypeStruct((m, d), F32), jax.ShapeDtypeStruct((nb, 8, d), F32)),
        grid=(nb, nt),
        in_specs=[row(d), row(W_A), row(W_B), _layer_spec(w_prev3, ei), full(nw), _layer_spec(w_in3, oi), full(cw),
                  _layer_spec(w_out3, oi)],
        out_specs=(row(d), pl.BlockSpec((None, 8, d), lambda b, i: (b, 0, 0))),
        scratch_shapes=[pltpu.VMEM((8, d), F32)],
        compiler_params=_cparams("arbitrary", "arbitrary"),
        name="odd_prompt",
    )(x, oa, ob, w_prev3, nw, w_in3, cw, w_out3)


def _odd_decode_kernel(x_ref, bg_ref, cg_ref, hh_ref, z_ref, b0_ref, b1_ref, cw_ref, w_ref, y_ref, u_ref):
    cw = cw_ref[...]
    u = cg_ref[...].astype(F32) * hh_ref[...].astype(F32)
    cv = b0_ref[...] * cw[0:1, :] + b1_ref[...] * cw[1:2, :] + u * cw[2:3, :]
    g = bg_ref[...].astype(F32) * cv * _silu(z_ref[...].astype(F32))
    y_ref[...] = x_ref[...] + jnp.dot(g.astype(BF16), w_ref[...], preferred_element_type=F32)
    u_ref[...] = u


def _odd_decode(x, p, b0, b1, cw, w3, li):
    m, d = x.shape
    col = lambda c: pl.BlockSpec((m, d), lambda i, c=c: (0, c))
    full = lambda a: pl.BlockSpec(a.shape, lambda i: (0,) * a.ndim)
    return pl.pallas_call(
        _odd_decode_kernel,
        out_shape=(jax.ShapeDtypeStruct((m, d), F32), jax.ShapeDtypeStruct((m, d), F32)),
        grid=(1,),
        in_specs=[full(x), col(0), col(1), col(2), col(3), full(b0), full(b1), full(cw), _layer_spec(w3, li)],
        out_specs=(pl.BlockSpec((m, d), lambda i: (0, 0)), pl.BlockSpec((m, d), lambda i: (0, 0))),
        compiler_params=_cparams("arbitrary"),
        name="odd_decode",
    )(x, p, p, p, p, b0, b1, cw, w3)


def _tile(n, want):
    t = math.gcd(n, want)
    assert t == n or t % 8 == 0, (n, want)
    return t


def _pad_lanes(v):
    return jnp.pad(v.astype(F32), (0, LANE - v.shape[0])).reshape(1, LANE)


def kernel(x_prompt, x_sample, cache_k, cache_v, page_table, state_gdn, state_gdn_conv, state_shortconv, norm_w, rel_table, w_in_even, w_out_even, qn_w, kn_w, lam_q1, lam_k1, lam_q2, lam_k2, subln_w, gdn_conv_w, gdn_a_log, gdn_dt_bias, gdn_norm_w, w_in_odd, sc_conv_w, w_out_odd):
    nbp, t, d = x_prompt.shape
    nbs = x_sample.shape[0]
    page = cache_k.shape[2]
    assert x_sample.shape[1] == 1 and page >= MAX_DISTANCE and t % 8 == 0 and DEPTH % 2 == 0
    tq = _tile(t, 512)
    assert tq >= MAX_DISTANCE

    xp = x_prompt.reshape(nbp * t, d)
    xs = x_sample.reshape(nbs, d)

    g64 = jnp.arange(W_A) // D_HA
    bd = jnp.where(g64[:, None] == g64[None, :], 1.0 / D_HA, 0.0).astype(BF16)
    bias_p = _bias_tiles(rel_table.astype(F32), tq)
    bias_s = _bias_decode(rel_table.astype(F32), page)

    n_even = (DEPTH + 1) // 2
    w_in_e, w_out_e = w_in_even.astype(BF16), w_out_even.astype(BF16)
    w_in_o, w_out_o = w_in_odd.astype(BF16), w_out_odd.astype(BF16)
    kv_p = kv_s = s_s = None
    sp, gcp, scp = [], [], []
    gcs, scs = [], []
    ei = oi = 0
    for li in range(DEPTH):
        nw = norm_w[li].reshape(1, d)
        if li % 2 == 0:
            lambda_init = 0.8 - 0.6 * math.exp(-0.3 * li)
            w_ab = jnp.pad(w_in_e[ei][:, P_MAIN:], ((0, 0), (0, LANE - 2 * H_B)))
            qw = jnp.tile(qn_w[ei], W_A // D_HA).reshape(1, W_A)
            kw = jnp.tile(kn_w[ei], W_A // D_HA).reshape(1, W_A)
            sw = subln_w[ei].reshape(1, DA)
            lam = (jnp.exp(jnp.sum(lam_q1[ei] * lam_k1[ei]).astype(F32))
                   - jnp.exp(jnp.sum(lam_q2[ei] * lam_k2[ei]).astype(F32)) + lambda_init).reshape(1)
            cw = gdn_conv_w[ei]
            alog = _pad_lanes(gdn_a_log[ei])
            dtb = _pad_lanes(gdn_dt_bias[ei])
            gnw = gdn_norm_w[ei].reshape(1, DV_B)

            qb, kb, *kv_p, p, ab, tail = _even_in(xp, nw, w_in_e, w_ab, qw, kw, bd, _tile(t, 512), ei, n_even, kv_p,
                                                  conv_w=cw, seq_len=t)
            oa = _attn_prompt(lam, qb, kb, bias_p, p, sw, nbp, t, tq, 1.0 - lambda_init)
            ob, s_new = _gdn_prompt(p, ab, alog, dtb, gnw, nbp, t, _tile(t, 512))
            prev_p = (oa, ob, w_out_e, ei)
            sp.append(s_new)
            gcp.append(tail[:, 8 - (GDN_CONV - 1):, :])

            qb, kb, *kv_s, p, ab = _even_in(xs, nw, w_in_e, w_ab, qw, kw, bd, nbs, ei, n_even, kv_s)
            oa = _attn_decode(page_table, lam, qb, kb, cache_k, cache_v, ei, bias_s, p, sw, 1.0 - lambda_init)
            conv0_t = jnp.swapaxes(state_gdn_conv[ei], 0, 1)
            qn, kn, vv, eg, beta, qk, conv_new = _gdn_decode_prep(p, ab, conv0_t, cw, alog, dtb)
            bb = 8
            to_cols = lambda a: a.reshape(nbs // bb, bb, H_B, DK_B).transpose(0, 2, 3, 1)
            s_s, ob = _gdn_decode(state_gdn, ei, to_cols(qn), to_cols(kn), vv, eg, beta, qk, p, gnw, bb, s_s)
            xs = _out_proj(xs, oa, ob, w_out_e, ei, nbs)
            gcs.append(jnp.swapaxes(conv_new, 0, 1))
            ei += 1
        else:
            cw = sc_conv_w[oi]
            xp, tail = _odd_prompt(xp, *prev_p, nw, w_in_o, cw, w_out_o, oi, nbp, t, _tile(t, 512))
            scp.append(tail[:, 8 - (SC_WIDTH - 1):, :])

            p = _norm_proj(xs, nw, w_in_o, oi, nbs, 1024)
            buf0 = state_shortconv[oi]
            xs, u = _odd_decode(xs, p, buf0[:, 0, :], buf0[:, 1, :], cw, w_out_o, oi)
            scs.append(jnp.stack([buf0[:, 1, :], u], axis=1))
            oi += 1

    leaf_p = lambda a: a.reshape(n_even, nbp, t, H_A, DA)
    leaf_s = lambda a: a.reshape(n_even, nbs, 1, H_A, DA)
    return (xp.reshape(nbp, t, d), xs.reshape(nbs, 1, d),
            leaf_p(kv_p[0]), leaf_p(kv_p[1]), jnp.stack(sp), jnp.stack(gcp), jnp.stack(scp),
            leaf_s(kv_s[0]), leaf_s(kv_s[1]), s_s, jnp.stack(gcs), jnp.stack(scs))
```

```python
import functools
import math

import jax
import jax.numpy as jnp
from jax import lax
from jax.experimental import pallas as pl
from jax.experimental.pallas import tpu as pltpu

F32, BF16 = jnp.float32, jnp.bfloat16

DEPTH = 4
H_A, D_HA = 4, 64
DA = 2 * D_HA
W_A = H_A * DA
H_B, DK_B, DV_B = 4, 128, 128
W_B = H_B * DV_B
QKV_B = 2 * H_B * DK_B + H_B * DV_B
GDN_CONV, GDN_CHUNK, SC_WIDTH = 4, 64, 3
NUM_BUCKETS, MAX_EXACT, MAX_DISTANCE = 32, 16, 128
EPS, NEG = 1e-6, -1e30
LOG2E = math.log2(math.e)
LANE = 128
P_MAIN = 4 * W_A + QKV_B + W_B
VMEM_LIMIT = 48 * 1024 * 1024
LOCAL_GROUP = 4
SEQ_ROWS = 512
DECODE_STATE_SEQS = 8
DECODE_PROJ_COLS = 1024


def _layer_spec(w3, li, cols=None):
    return pl.BlockSpec((None, w3.shape[1], cols or w3.shape[2]), lambda *_: (li, 0, 0))


def _cparams(*sem):
    return pltpu.CompilerParams(dimension_semantics=sem, vmem_limit_bytes=VMEM_LIMIT)


def _silu(z):
    h = 0.5 * z
    return h + h * jnp.tanh(h)


def _sigmoid(z):
    return 1.0 / (1.0 + jnp.exp(-z))


def _dot(a, b):
    return jnp.dot(a.astype(BF16), b.astype(BF16), preferred_element_type=F32)


def _dot_nt(a, b):
    return lax.dot_general(a.astype(BF16), b.astype(BF16), (((1,), (1,)), ((), ())), preferred_element_type=F32)


def _rms_rows(x, w):
    return x * lax.rsqrt(jnp.mean(x * x, axis=-1, keepdims=True) + EPS) * w


def _norm_proj_kernel(x_ref, nw_ref, w_ref, o_ref, h_ref):
    @pl.when(pl.program_id(1) == 0)
    def _():
        h_ref[...] = _rms_rows(x_ref[...], nw_ref[...]).astype(BF16)

    o_ref[...] = jnp.dot(h_ref[...], w_ref[...], preferred_element_type=F32).astype(BF16)


def _norm_proj(x, nw, w3, li, tm, tn):
    m, d = x.shape
    n = w3.shape[2]
    return pl.pallas_call(
        _norm_proj_kernel,
        out_shape=jax.ShapeDtypeStruct((m, n), BF16),
        grid=(m // tm, n // tn),
        in_specs=[pl.BlockSpec((tm, d), lambda i, j: (i, 0)),
                  pl.BlockSpec((1, d), lambda i, j: (0, 0)),
                  pl.BlockSpec((None, d, tn), lambda i, j: (li, 0, j))],
        out_specs=pl.BlockSpec((tm, tn), lambda i, j: (i, j)),
        scratch_shapes=[pltpu.VMEM((tm, d), BF16)],
        compiler_params=_cparams("arbitrary", "arbitrary"),
        name="norm_proj",
    )(x, nw, w3)


def _even_in_kernel(x_ref, nw_ref, w_ref, wab_ref, qw_ref, kw_ref, bd_ref, *refs, n_alias, tiles_per_seq):
    if tiles_per_seq:
        cw_ref = refs[0]
        qo_ref, kb_ref, kf_ref, vf_ref, p_ref, ab_ref, tail_ref, carry_ref = refs[1 + n_alias:]
    else:
        qo_ref, kb_ref, kf_ref, vf_ref, p_ref, ab_ref = refs[n_alias:]
    tm = x_ref.shape[0]
    h = _rms_rows(x_ref[...], nw_ref[...]).astype(BF16)
    proj = lambda c: jnp.dot(h, w_ref[:, c * W_A:(c + 1) * W_A], preferred_element_type=F32)
    ab_ref[...] = jnp.dot(h, wab_ref[...], preferred_element_type=F32)
    bd = bd_ref[...]

    def group_norm(x, w):
        ms = jnp.dot((x * x).astype(BF16), bd, preferred_element_type=F32)
        return x * lax.rsqrt(ms + EPS) * w

    def leaf(ref, val):
        for hd in range(H_A):
            ref[pl.ds(hd, tm, stride=H_A), :] = val[:, hd * DA:(hd + 1) * DA]

    qn = group_norm(proj(0), qw_ref[...])
    qo_ref[...] = (qn * (D_HA ** -0.5 * LOG2E)).astype(BF16)
    kn = group_norm(proj(1), kw_ref[...])
    kb_ref[...] = kn.astype(BF16)
    leaf(kf_ref, kn)
    v = proj(2)
    leaf(vf_ref, v)
    p_ref[:, 0:W_A] = v.astype(BF16)
    if tiles_per_seq:
        first = pl.program_id(0) % tiles_per_seq == 0

        @pl.when(first)
        def _():
            carry_ref[...] = jnp.zeros(carry_ref.shape, F32)

    c_qkv = (4 * W_A) // W_A
    for c in range(3, P_MAIN // W_A):
        y = proj(c)
        seg = c - c_qkv
        if tiles_per_seq and 0 <= seg < QKV_B // W_B:
            cols = slice(seg * W_B, (seg + 1) * W_B)
            raw = y
            y = _silu(_shifted_conv(raw, carry_ref[:, cols], cw_ref[:, cols]))
            carry_ref[:, cols] = raw[tm - 8:tm]
            tail_ref[:, cols] = raw[tm - 8:tm]
        p_ref[:, (c - 2) * W_A:(c - 1) * W_A] = y.astype(BF16)


def _even_in(x, nw, w3, w_ab, qw, kw, bd, tm, ei, n_layers, kv_prev, conv_w=None, seq_len=None):
    m, d = x.shape
    tiles_per_seq = 0 if conv_w is None else seq_len // tm
    n_rest = P_MAIN - 2 * W_A
    full = lambda a: pl.BlockSpec(a.shape, lambda i: (0,) * a.ndim)
    row = lambda n: pl.BlockSpec((tm, n), lambda i: (i, 0))
    leaf = pl.BlockSpec((None, tm * H_A, DA), lambda i: (ei, i, 0))
    leaf_shape = jax.ShapeDtypeStruct((n_layers, m * H_A, DA), F32)
    n_alias = 0 if kv_prev is None else 2
    out_shape = [jax.ShapeDtypeStruct((m, W_A), BF16), jax.ShapeDtypeStruct((m, W_A), BF16), leaf_shape, leaf_shape,
                 jax.ShapeDtypeStruct((m, n_rest), BF16), jax.ShapeDtypeStruct((m, LANE), F32)]
    out_specs = [row(W_A), row(W_A), leaf, leaf, row(n_rest), row(LANE)]
    in_specs = [row(d), full(nw), _layer_spec(w3, ei, P_MAIN), full(w_ab), full(qw), full(kw), full(bd)]
    args = [x, nw, w3, w_ab, qw, kw, bd]
    scratch = []
    if tiles_per_seq:
        in_specs.append(full(conv_w))
        args.append(conv_w)
        out_shape.append(jax.ShapeDtypeStruct((m // seq_len, 8, QKV_B), F32))
        out_specs.append(pl.BlockSpec((None, 8, QKV_B), lambda i: (i // tiles_per_seq, 0, 0)))
        scratch.append(pltpu.VMEM((8, QKV_B), F32))
    n_in = len(args)
    return pl.pallas_call(
        functools.partial(_even_in_kernel, n_alias=n_alias, tiles_per_seq=tiles_per_seq),
        out_shape=tuple(out_shape),
        grid=(m // tm,),
        in_specs=in_specs + [pl.BlockSpec(memory_space=pl.ANY)] * n_alias,
        out_specs=tuple(out_specs),
        input_output_aliases={} if kv_prev is None else {n_in: 2, n_in + 1: 3},
        scratch_shapes=scratch,
        compiler_params=_cparams("arbitrary"),
        name="even_in",
    )(*args, *(kv_prev or ()))


def _t5_bias(n, tab_ref, h):
    nf = jnp.maximum(n, 1).astype(F32)
    large = MAX_EXACT + (jnp.log(nf / MAX_EXACT) / math.log(MAX_DISTANCE / MAX_EXACT)
                         * (NUM_BUCKETS - MAX_EXACT)).astype(jnp.int32)
    large = jnp.minimum(large, NUM_BUCKETS - 1)
    bkt = jnp.where(n < MAX_EXACT, n, large)
    out = jnp.zeros(n.shape, F32)
    for b in range(NUM_BUCKETS):
        out = jnp.where(bkt == b, tab_ref[b, h], out)
    return (out - tab_ref[NUM_BUCKETS - 1, h]) * LOG2E


def _bias_tiles_kernel(tab_ref, o_ref, *, tq):
    h = pl.program_id(0)
    i = lax.broadcasted_iota(jnp.int32, (tq, tq), 0)
    j = lax.broadcasted_iota(jnp.int32, (tq, tq), 1)
    n0 = i - j
    o_ref[0, 0] = jnp.where(n0 >= 0, _t5_bias(jnp.maximum(n0, 0), tab_ref, h), NEG)
    o_ref[0, 1] = _t5_bias(n0 + tq, tab_ref, h)


def _bias_tiles(rel_table, tq):
    return pl.pallas_call(
        functools.partial(_bias_tiles_kernel, tq=tq),
        out_shape=jax.ShapeDtypeStruct((H_A, 2, tq, tq), F32),
        grid=(H_A,),
        in_specs=[pl.BlockSpec(memory_space=pltpu.SMEM)],
        out_specs=pl.BlockSpec((1, 2, tq, tq), lambda h: (h, 0, 0, 0)),
        compiler_params=_cparams("arbitrary"),
        name="bias_tiles",
    )(rel_table)


def _bias_decode_kernel(tab_ref, o_ref, *, page):
    w = page * H_A
    row = lax.broadcasted_iota(jnp.int32, (2 * H_A, w), 0)
    lane = lax.broadcasted_iota(jnp.int32, (2 * H_A, w), 1)
    valid = (lane & (H_A - 1)) == (row >> 1)
    n = page - (lane >> 2)
    near = jnp.zeros((2 * H_A, w), F32)
    new = jnp.zeros((2 * H_A, LANE), F32)
    for h in range(H_A):
        near = jnp.where((row >> 1) == h, _t5_bias(n, tab_ref, h), near)
        new = jnp.where((row[:, :LANE] >> 1) == h, _t5_bias(jnp.zeros((2 * H_A, LANE), jnp.int32), tab_ref, h), new)
    o_ref[:, 0:w] = jnp.where(valid, 0.0, NEG)
    o_ref[:, w:2 * w] = jnp.where(valid, near, NEG)
    o_ref[:, 2 * w:2 * w + LANE] = new


def _bias_decode(rel_table, page):
    assert H_A == 4
    return pl.pallas_call(
        functools.partial(_bias_decode_kernel, page=page),
        out_shape=jax.ShapeDtypeStruct((2 * H_A, 2 * page * H_A + LANE), F32),
        in_specs=[pl.BlockSpec(memory_space=pltpu.SMEM)],
        out_specs=pl.BlockSpec(memory_space=pltpu.VMEM),
        name="bias_decode",
    )(rel_table)


def _attn_finish(o, sw, z, out_scale):
    ms = jnp.mean(o * o, axis=-1, keepdims=True)
    return o * lax.rsqrt(ms + EPS) * sw * out_scale * _silu(z.astype(F32))


def _attn_prompt_kernel(lam_ref, q_ref, k_ref, v_ref, bias_ref, za_ref, sw_ref, o_ref, m, l, a, *, tq, out_scale):
    reps = tq // LANE
    lane = lax.broadcasted_iota(jnp.int32, (tq, DA), 1)

    def q_block(qi, carry):
        qrows = pl.ds(pl.multiple_of(qi * tq, tq), tq)
        q = q_ref[qrows, :].astype(F32)
        q2 = jnp.concatenate([jnp.where(lane < D_HA, q, 0.0), jnp.where(lane >= D_HA, q, 0.0)], axis=0).astype(BF16)
        m[...] = jnp.full(m.shape, NEG, F32)
        l[...] = jnp.zeros(l.shape, F32)
        a[...] = jnp.zeros(a.shape, F32)

        def step(j, bias):
            rows = pl.ds(pl.multiple_of(j * tq, tq), tq)
            s = lax.dot_general(q2, k_ref[rows, :], (((1,), (1,)), ((), ())), preferred_element_type=F32)
            if bias is not None:
                s = s + jnp.concatenate([bias, bias], axis=0)
            m_prev = m[...]
            m_new = jnp.maximum(m_prev, jnp.max(s, axis=1, keepdims=True))
            p = jnp.exp2(s - jnp.concatenate([m_new] * reps, axis=1))
            alpha = jnp.exp2(m_prev - m_new)
            l[...] = alpha * l[...] + jnp.sum(p, axis=1, keepdims=True)
            a[...] = alpha * a[...] + jnp.dot(p.astype(BF16), v_ref[rows, :], preferred_element_type=F32)
            m[...] = m_new

        def far(j, c):
            step(j, None)
            return c

        lax.fori_loop(0, jnp.maximum(qi - 1, 0), far, 0)

        @pl.when(qi >= 1)
        def _():
            step(qi - 1, bias_ref[0, 1])

        step(qi, bias_ref[0, 0])
        o = a[...] / l[...]
        o = o[:tq] - lam_ref[0] * o[tq:]
        o_ref[qrows, :] = _attn_finish(o, sw_ref[...], za_ref[qrows, :], out_scale).astype(BF16)
        return carry

    lax.fori_loop(0, q_ref.shape[0] // tq, q_block, 0)


def _attn_prompt(lam, q, k, bias, p, sw, nb, t, tq, out_scale):
    m = q.shape[0]
    za0 = W_A // DA
    v0 = 0
    seq = lambda c0: pl.BlockSpec((t, DA), lambda b, h, c0=c0: (b, c0 + h))
    return pl.pallas_call(
        functools.partial(_attn_prompt_kernel, tq=tq, out_scale=out_scale),
        out_shape=jax.ShapeDtypeStruct((m, W_A), BF16),
        grid=(nb, H_A),
        in_specs=[pl.BlockSpec(memory_space=pltpu.SMEM),
                  seq(0), seq(0), seq(v0),
                  pl.BlockSpec((1, 2, tq, tq), lambda b, h: (h, 0, 0, 0)),
                  seq(za0),
                  pl.BlockSpec((1, DA), lambda b, h: (0, 0))],
        out_specs=seq(0),
        scratch_shapes=[pltpu.VMEM((2 * tq, LANE), F32), pltpu.VMEM((2 * tq, LANE), F32),
                        pltpu.VMEM((2 * tq, DA), F32)],
        compiler_params=_cparams("arbitrary", "arbitrary"),
        name="attn_prompt",
    )(lam, q, k, p, bias, p, sw)


def _attn_decode_kernel(pt_ref, lam_ref, q_ref, kn_ref, vn_ref, *refs, page, npg, out_scale):
    ck_refs, cv_refs = refs[:npg], refs[npg:2 * npg]
    bias_ref, za_ref, sw_ref, o_ref = refs[2 * npg:]
    rows, w = 2 * H_A, page * H_A
    row = lax.broadcasted_iota(jnp.int32, (rows, DA), 0)
    lane = lax.broadcasted_iota(jnp.int32, (rows, DA), 1)

    def head_rows(x):
        out = jnp.zeros((rows, DA), F32)
        for h in range(H_A):
            out = jnp.where((row >> 1) == h, jnp.broadcast_to(x[:, h * DA:(h + 1) * DA], (rows, DA)), out)
        return out

    q8 = jnp.where((lane >> 6) == (row & 1), head_rows(q_ref[...].astype(F32)), 0.0)
    q8b = q8.astype(BF16)
    s_all = []
    for j in range(npg):
        s = lax.dot_general(q8b, ck_refs[j][...].astype(BF16), (((1,), (1,)), ((), ())), preferred_element_type=F32)
        s_all.append(s + (bias_ref[:, w:2 * w] if j == npg - 1 else bias_ref[:, 0:w]))
    s_new = (jnp.sum(q8 * head_rows(kn_ref[...].astype(F32)), axis=1, keepdims=True)
             + bias_ref[:, 2 * w:2 * w + 1])
    m = s_all[0]
    for s in s_all[1:]:
        m = jnp.maximum(m, s)
    m = jnp.maximum(jnp.max(m, axis=1, keepdims=True), s_new)
    p_new = jnp.exp2(s_new - m)
    l = p_new
    acc = p_new * head_rows(vn_ref[...].astype(F32))
    for j in range(npg):
        p = jnp.exp2(s_all[j] - m)
        l = l + jnp.sum(p, axis=1, keepdims=True)
        acc = acc + jnp.dot(p.astype(BF16), cv_refs[j][...].astype(BF16), preferred_element_type=F32)
    o = acc / l
    sw = sw_ref[...]
    z = za_ref[...]
    outs = []
    for h in range(H_A):
        oh = o[2 * h:2 * h + 1, :] - lam_ref[0] * o[2 * h + 1:2 * h + 2, :]
        outs.append(_attn_finish(oh, sw, z[:, h * DA:(h + 1) * DA], out_scale))
    o_ref[...] = jnp.concatenate(outs, axis=1).astype(BF16)


def _attn_decode(page_table, lam, q, kn, cache_k, cache_v, ei, bias, p, sw, out_scale):
    nb, npg = page_table.shape
    page = cache_k.shape[2]
    ck = cache_k.reshape(cache_k.shape[0], cache_k.shape[1], page * H_A, DA)
    cv = cache_v.reshape(cache_v.shape[0], cache_v.shape[1], page * H_A, DA)
    r3 = lambda a: a.reshape(nb, 1, a.shape[-1])
    vec = lambda c: pl.BlockSpec((None, 1, W_A), lambda b, pt, c=c: (b, 0, c))
    cache = [pl.BlockSpec((None, None, page * H_A, DA), lambda b, pt, j=j: (ei, pt[b * npg + j], 0, 0))
             for j in range(npg)]
    out = pl.pallas_call(
        functools.partial(_attn_decode_kernel, page=page, npg=npg, out_scale=out_scale),
        out_shape=jax.ShapeDtypeStruct((nb, 1, W_A), BF16),
        grid_spec=pltpu.PrefetchScalarGridSpec(
            num_scalar_prefetch=1,
            grid=(nb,),
            in_specs=[pl.BlockSpec(memory_space=pltpu.SMEM), vec(0), vec(0), vec(0)] + cache + cache
                     + [pl.BlockSpec(bias.shape, lambda b, pt: (0, 0)),
                        vec(1),
                        pl.BlockSpec((1, DA), lambda b, pt: (0, 0))],
            out_specs=pl.BlockSpec((None, 1, W_A), lambda b, pt: (b, 0, 0))),
        compiler_params=_cparams("arbitrary"),
        name="attn_decode",
    )(page_table.reshape(-1), lam, r3(q), r3(kn), r3(p), *([ck] * npg), *([cv] * npg), bias, r3(p), sw)
    return out.reshape(nb, W_A)


def _shifted_conv(x, prev, w):
    r, c = x.shape
    taps = w.shape[0]
    x3 = x.reshape(r // 8, 8, c)
    sub = lax.broadcasted_iota(jnp.int32, x3.shape, 1)
    y = x3 * w[taps - 1:taps, :].reshape(1, 1, c)
    for s in range(1, taps):
        rot = pltpu.roll(x3, s, 1)
        before = jnp.concatenate([pltpu.roll(prev, s, 0)[None], rot[:-1]], axis=0)
        y = y + jnp.where(sub < s, before, rot) * w[taps - 1 - s:taps - s, :].reshape(1, 1, c)
    return y.reshape(r, c)


def _gates(ab, alog, dtb):
    x = ab + dtb
    sp = jnp.maximum(x, 0.0) + jnp.log(1.0 + jnp.exp(-jnp.abs(x)))
    return -jnp.exp(alog) * sp, _sigmoid(ab)


def _l2norm(x):
    return x * lax.rsqrt(jnp.sum(x * x, axis=-1, keepdims=True) + EPS)


def _rms(x, w):
    return x * lax.rsqrt(jnp.mean(x * x, axis=-1, keepdims=True) + EPS) * w


def _level_masks(c):
    i = lax.broadcasted_iota(jnp.int32, (c, c), 0)
    j = lax.broadcasted_iota(jnp.int32, (c, c), 1)
    masks = []
    s = 1
    while s < c:
        sh = s.bit_length() - 1
        masks.append((((i >> (sh + 1)) == (j >> (sh + 1))) & ((i >> sh) != (j >> sh)) & (i > j)).astype(F32))
        s *= 2
    return masks


def _gdn_prompt_kernel(cq_ref, ck_ref, cv_ref, zb_ref, ab_ref, alog_ref, dtb_ref, gnw_ref,
                       ob_ref, s_ref, g_ref, beta_ref, u_ref, wq_ref, ak_ref, gl_ref, *, rows, chunk, nseq):
    t = pl.program_id(1)
    c = chunk

    @pl.when(t == 0)
    def _():
        s_ref[...] = jnp.zeros(s_ref.shape, F32)

    ii = lax.broadcasted_iota(jnp.int32, (c, c), 0)
    jj = lax.broadcasted_iota(jnp.int32, (c, c), 1)
    incl = ii >= jj
    ltri = incl.astype(BF16)
    masks = _level_masks(c)

    def prep(b, ci):
        rs = pl.ds(pl.multiple_of(ci * c, c), c)
        gch = g_ref[b, rs, :]
        g1 = gch.astype(BF16)
        r1 = gch - g1.astype(F32)
        g2 = r1.astype(BF16)
        g3 = (r1 - g2.astype(F32)).astype(BF16)
        d = functools.partial(jnp.dot, preferred_element_type=F32)
        gc = d(ltri, g1) + d(ltri, g2) + d(ltri, g3)
        gct = jnp.concatenate([gc, gc], axis=0).T
        ge = jnp.exp(gc)
        kdec = jnp.exp(gc[c - 1:c, :] - gc)
        gl_ref[b, ci] = ge[c - 1:c, :]
        return rs, gc, gct, ge, kdec, beta_ref[b, rs, :]

    def local(b, ig):
        items = []
        for ci in [LOCAL_GROUP * ig + k for k in range(LOCAL_GROUP)]:
            rs, gc, gct, ge, kdec, bch = prep(b, ci)
            for h in range(H_B):
                hc = slice(h * DK_B, (h + 1) * DK_B)
                qh = _l2norm(cq_ref[b, rs, hc].astype(F32)) * (DK_B ** -0.5)
                kh = _l2norm(ck_ref[b, rs, hc].astype(F32))
                vh = cv_ref[b, rs, hc].astype(F32)
                bcol = bch[:, H_B + h:H_B + h + 1]
                gecol = ge[:, h:h + 1]
                decay = jnp.exp(jnp.where(incl, gc[:, h:h + 1] - gct[h:h + 1, :c], NEG))
                kbeta = kh * bcol
                wq_ref[b, ci, h, c:2 * c] = (qh * gecol).astype(BF16)
                ak_ref[b, ci, h, c:c + DK_B] = (kh * kdec[:, h:h + 1]).T.astype(BF16)
                items.append(dict(ci=ci, h=h, rs=rs, hc=hc, decay=decay, qk=(kbeta, qh, kh),
                                  rhs=jnp.concatenate([vh * bcol, kbeta * gecol], axis=1)))
        for it in items:
            kbeta, qh, kh = it.pop("qk")
            both = _dot_nt(jnp.concatenate([kbeta, qh], axis=0), kh)
            it["mm"] = jnp.where(ii > jj, both[:c] * it["decay"], 0.0)
            ak_ref[b, it["ci"], it["h"], 0:c] = (both[c:] * it["decay"]).astype(BF16)
        for it in items:
            it["pp"] = -(it["mm"] * masks[0])
        for mk in masks[1:]:
            for it in items:
                e = it["mm"] * mk
                it["x"] = e + _dot(it["pp"], e)
            for it in items:
                it["pp"] = it["pp"] - (it["x"] + _dot(it["x"], it["pp"]))
        for it in items:
            uw = it["rhs"] + _dot(it["pp"], it["rhs"])
            u_ref[b, it["rs"], it["hc"]] = uw[:, :DV_B]
            wq_ref[b, it["ci"], it["h"], 0:c] = uw[:, DV_B:].astype(BF16)

    for b in range(nseq):
        g, beta = _gates(ab_ref[b], alog_ref[...], dtb_ref[...])
        g_ref[b] = g
        beta_ref[b] = beta

        def local_b(ig, carry, b=b):
            local(b, ig)
            return carry

        lax.fori_loop(0, rows // (LOCAL_GROUP * c), local_b, 0)
    gnw = gnw_ref[...]

    def scan(ci, carry):
        rs = pl.ds(pl.multiple_of(ci * c, c), c)
        chains = [(b, h) for b in range(nseq) for h in range(H_B)]
        hc = lambda h: slice(h * DK_B, (h + 1) * DK_B)
        s_old = [s_ref[b, h] for b, h in chains]
        r = [jnp.dot(wq_ref[b, ci, h], s_old[k].astype(BF16), preferred_element_type=F32)
             for k, (b, h) in enumerate(chains)]
        v_new = [u_ref[b, rs, hc(h)] - r[k][:c] for k, (b, h) in enumerate(chains)]
        r2 = [jnp.dot(ak_ref[b, ci, h], v_new[k].astype(BF16), preferred_element_type=F32)
              for k, (b, h) in enumerate(chains)]
        for k, (b, h) in enumerate(chains):
            s_ref[b, h] = s_old[k] * gl_ref[b, ci][:, h:h + 1] + r2[k][c:]
            ob_ref[b, rs, hc(h)] = (_rms(r[k][c:] + r2[k][:c], gnw)
                                    * _silu(zb_ref[b, rs, hc(h)].astype(F32))).astype(BF16)
        return carry

    lax.fori_loop(0, rows // c, scan, 0)


def _gdn_prompt(p, ab, alog, dtb, gnw, nb, t, rows):
    m = p.shape[0]
    nt = t // rows
    c = math.gcd(GDN_CHUNK, t)
    nseq = 2 if nb % 2 == 0 else 1
    p3 = p.reshape(nb, t, p.shape[1])
    col = lambda cidx: pl.BlockSpec((nseq, rows, W_B), lambda b, i, cidx=cidx: (b, i, cidx))
    full = lambda a: pl.BlockSpec(a.shape, lambda b, i: (0,) * a.ndim)
    c0 = (2 * W_A) // W_B
    ob, s_new = pl.pallas_call(
        functools.partial(_gdn_prompt_kernel, rows=rows, chunk=c, nseq=nseq),
        out_shape=(jax.ShapeDtypeStruct((nb, t, W_B), BF16), jax.ShapeDtypeStruct((nb, H_B, DK_B, DV_B), F32)),
        grid=(nb // nseq, nt),
        in_specs=[col(c0), col(c0 + 1), col(c0 + 2), col(c0 + 3),
                  pl.BlockSpec((nseq, rows, LANE), lambda b, i: (b, i, 0)),
                  full(alog), full(dtb), full(gnw)],
        out_specs=(pl.BlockSpec((nseq, rows, W_B), lambda b, i: (b, i, 0)),
                   pl.BlockSpec((nseq, H_B, DK_B, DV_B), lambda b, i: (b, 0, 0, 0))),
        scratch_shapes=[pltpu.VMEM((nseq, rows, LANE), F32), pltpu.VMEM((nseq, rows, LANE), F32),
                        pltpu.VMEM((nseq, rows, W_B), F32),
                        pltpu.VMEM((nseq, rows // c, H_B, 2 * c, DK_B), BF16),
                        pltpu.VMEM((nseq, rows // c, H_B, c + DK_B, c), BF16),
                        pltpu.VMEM((nseq, rows // c, 1, LANE), F32)],
        compiler_params=_cparams("arbitrary", "arbitrary"),
        name="gdn_prompt",
    )(p3, p3, p3, p3, ab.reshape(nb, t, LANE), alog, dtb, gnw)
    return ob.reshape(m, W_B), s_new


def _gdn_decode_prep_kernel(pq_ref, pk_ref, pv_ref, ab_ref, c0_ref, cw_ref, alog_ref, dtb_ref,
                            q_ref, k_ref, v_ref, eg_ref, beta_ref, qk_ref, cn_ref):
    cw = cw_ref[...]
    taps = cw.shape[0]
    outs = (q_ref, k_ref, v_ref)
    for seg, ref in enumerate((pq_ref, pk_ref, pv_ref)):
        cols = slice(seg * W_B, (seg + 1) * W_B)
        x = ref[...].astype(F32)
        y = x * cw[taps - 1:taps, cols]
        for j in range(taps - 1):
            y = y + c0_ref[j, :, cols] * cw[j:j + 1, cols]
            if j >= 1:
                cn_ref[j - 1, :, cols] = c0_ref[j, :, cols]
        cn_ref[taps - 2, :, cols] = x
        outs[seg][...] = _silu(y)
    lane = lax.broadcasted_iota(jnp.int32, eg_ref.shape, 1)
    qk = jnp.zeros(eg_ref.shape, F32)
    for h in range(H_B):
        hc = slice(h * DK_B, (h + 1) * DK_B)
        qh = _l2norm(q_ref[:, hc]) * (DK_B ** -0.5)
        kh = _l2norm(k_ref[:, hc])
        q_ref[:, hc] = qh
        k_ref[:, hc] = kh
        qk = jnp.where(lane == h, jnp.sum(qh * kh, axis=-1, keepdims=True), qk)
    g, beta = _gates(ab_ref[...], alog_ref[...], dtb_ref[...])
    eg_ref[...] = jnp.exp(g)
    beta_ref[...] = beta
    qk_ref[...] = qk


def _gdn_decode_prep(p, ab, conv0_t, cw, alog, dtb):
    nb = p.shape[0]
    c0 = (2 * W_A) // W_B
    col = lambda cidx: pl.BlockSpec((nb, W_B), lambda i, cidx=cidx: (0, cidx))
    full = lambda a: pl.BlockSpec(a.shape, lambda i: (0,) * a.ndim)
    wide = jax.ShapeDtypeStruct((nb, W_B), F32)
    narrow = jax.ShapeDtypeStruct((nb, LANE), F32)
    ospec = lambda s: pl.BlockSpec(s.shape, lambda i: (0,) * len(s.shape))
    outs = (wide, wide, wide, narrow, narrow, narrow, jax.ShapeDtypeStruct(conv0_t.shape, F32))
    return pl.pallas_call(
        _gdn_decode_prep_kernel,
        out_shape=outs,
        grid=(1,),
        in_specs=[col(c0), col(c0 + 1), col(c0 + 2),
                  full(ab), full(conv0_t), full(cw), full(alog), full(dtb)],
        out_specs=tuple(ospec(s) for s in outs),
        compiler_params=_cparams("arbitrary"),
        name="gdn_decode_prep",
    )(p, p, p, ab, conv0_t, cw, alog, dtb)


def _gdn_decode_kernel(s_ref, qt_ref, kt_ref, v_ref, eg_ref, beta_ref, qk_ref, zb_ref, gnw_ref, *refs, bb, n_alias):
    so_ref, ob_ref, o_scr = refs[n_alias:]
    v = v_ref[...]
    eg = eg_ref[...]
    beta = beta_ref[...]
    qk = qk_ref[...]
    for h in range(H_B):
        hc = slice(h * DV_B, (h + 1) * DV_B)
        qt = qt_ref[h]
        kt = kt_ref[h]
        for i in range(bb):
            s_old = s_ref[i, h]
            kc = kt[:, i:i + 1]
            qc = qt[:, i:i + 1]
            egs = eg[i:i + 1, h:h + 1]
            ks = jnp.sum(s_old * kc, axis=0, keepdims=True)
            qs = jnp.sum(s_old * qc, axis=0, keepdims=True)
            v_new = beta[i:i + 1, H_B + h:H_B + h + 1] * (v[i:i + 1, hc] - egs * ks)
            o_scr[i:i + 1, hc] = egs * qs + qk[i:i + 1, h:h + 1] * v_new
            so_ref[i, h] = s_old * egs + kc * v_new
    gnw = gnw_ref[...]
    z = zb_ref[...]
    outs = [_rms(o_scr[:, h * DV_B:(h + 1) * DV_B], gnw) * _silu(z[:, h * DV_B:(h + 1) * DV_B].astype(F32))
            for h in range(H_B)]
    ob_ref[...] = jnp.concatenate(outs, axis=1).astype(BF16)


def _gdn_decode(state, ei, qt, kt, v, eg, beta, qk, p, gnw, bb, s_prev):
    nb = v.shape[0]
    ns = nb // bb
    narrow = pl.BlockSpec((bb, LANE), lambda i: (i, 0))
    tr = pl.BlockSpec((None, H_B, DK_B, bb), lambda i: (i, 0, 0, 0))
    zb0 = (2 * W_A + QKV_B) // W_B
    n_alias = 0 if s_prev is None else 1
    return pl.pallas_call(
        functools.partial(_gdn_decode_kernel, bb=bb, n_alias=n_alias),
        out_shape=(jax.ShapeDtypeStruct(state.shape, F32), jax.ShapeDtypeStruct((nb, W_B), BF16)),
        grid=(ns,),
        in_specs=[pl.BlockSpec((None, bb, H_B, DK_B, DV_B), lambda i: (ei, i, 0, 0, 0)),
                  tr, tr,
                  pl.BlockSpec((bb, W_B), lambda i: (i, 0)),
                  narrow, narrow, narrow,
                  pl.BlockSpec((bb, W_B), lambda i: (i, zb0)),
                  pl.BlockSpec((1, DV_B), lambda i: (0, 0))] + [pl.BlockSpec(memory_space=pl.ANY)] * n_alias,
        out_specs=(pl.BlockSpec((None, bb, H_B, DK_B, DV_B), lambda i: (ei, i, 0, 0, 0)),
                   pl.BlockSpec((bb, W_B), lambda i: (i, 0))),
        input_output_aliases={9: 0} if n_alias else {},
        scratch_shapes=[pltpu.VMEM((bb, W_B), F32)],
        compiler_params=_cparams("arbitrary"),
        name="gdn_decode",
    )(state, qt, kt, v, eg, beta, qk, p, gnw, *(() if s_prev is None else (s_prev,)))


def _out_proj_kernel(x_ref, oa_ref, ob_ref, w_ref, y_ref):
    y_ref[...] = (x_ref[...]
                  + jnp.dot(oa_ref[...], w_ref[0:W_A, :], preferred_element_type=F32)
                  + jnp.dot(ob_ref[...], w_ref[W_A:W_A + W_B, :], preferred_element_type=F32))


def _out_proj(x, oa, ob, w3, li, tm):
    m, d = x.shape
    return pl.pallas_call(
        _out_proj_kernel,
        out_shape=jax.ShapeDtypeStruct((m, d), F32),
        grid=(m // tm,),
        in_specs=[pl.BlockSpec((tm, d), lambda i: (i, 0)),
                  pl.BlockSpec((tm, W_A), lambda i: (i, 0)),
                  pl.BlockSpec((tm, W_B), lambda i: (i, 0)),
                  _layer_spec(w3, li)],
        out_specs=pl.BlockSpec((tm, d), lambda i: (i, 0)),
        compiler_params=_cparams("arbitrary"),
        name="out_proj",
    )(x, oa, ob, w3)


def _odd_prompt_kernel(x_ref, oa_ref, ob_ref, wp_ref, nw_ref, wi_ref, cw_ref, wo_ref, y_ref, sc_ref, carry_ref, *, rows):
    t = pl.program_id(1)
    d = x_ref.shape[1]

    @pl.when(t == 0)
    def _():
        carry_ref[...] = jnp.zeros(carry_ref.shape, F32)

    x = (x_ref[...]
         + jnp.dot(oa_ref[...], wp_ref[0:W_A, :], preferred_element_type=F32)
         + jnp.dot(ob_ref[...], wp_ref[W_A:W_A + W_B, :], preferred_element_type=F32))
    h = _rms_rows(x, nw_ref[...]).astype(BF16)
    proj = lambda c: jnp.dot(h, wi_ref[:, c * d:(c + 1) * d], preferred_element_type=F32)
    u = proj(1) * proj(2)
    cv = _shifted_conv(u, carry_ref[...], cw_ref[...])
    carry_ref[...] = u[rows - 8:rows]
    g = proj(0) * cv * _silu(proj(3))
    y_ref[...] = x + jnp.dot(g.astype(BF16), wo_ref[...], preferred_element_type=F32)

    @pl.when(t == pl.num_programs(1) - 1)
    def _():
        sc_ref[...] = u[rows - 8:rows]


def _odd_prompt(x, oa, ob, w_prev3, ei, nw, w_in3, cw, w_out3, oi, nb, t, rows):
    m, d = x.shape
    nt = t // rows
    full = lambda a: pl.BlockSpec(a.shape, lambda b, i: (0,) * a.ndim)
    row = lambda n: pl.BlockSpec((rows, n), lambda b, i: (b * nt + i, 0))
    return pl.pallas_call(
        functools.partial(_odd_prompt_kernel, rows=rows),
        out_shape=(jax.ShapeDtypeStruct((m, d), F32), jax.ShapeDtypeStruct((nb, 8, d), F32)),
        grid=(nb, nt),
        in_specs=[row(d), row(W_A), row(W_B), _layer_spec(w_prev3, ei), full(nw), _layer_spec(w_in3, oi), full(cw),
                  _layer_spec(w_out3, oi)],
        out_specs=(row(d), pl.BlockSpec((None, 8, d), lambda b, i: (b, 0, 0))),
        scratch_shapes=[pltpu.VMEM((8, d), F32)],
        compiler_params=_cparams("arbitrary", "arbitrary"),
        name="odd_prompt",
    )(x, oa, ob, w_prev3, nw, w_in3, cw, w_out3)


def _odd_decode_kernel(x_ref, bg_ref, cg_ref, hh_ref, z_ref, b0_ref, b1_ref, cw_ref, w_ref, y_ref, u_ref):
    cw = cw_ref[...]
    u = cg_ref[...].astype(F32) * hh_ref[...].astype(F32)
    cv = b0_ref[...] * cw[0:1, :] + b1_ref[...] * cw[1:2, :] + u * cw[2:3, :]
    g = bg_ref[...].astype(F32) * cv * _silu(z_ref[...].astype(F32))
    y_ref[...] = x_ref[...] + jnp.dot(g.astype(BF16), w_ref[...], preferred_element_type=F32)
    u_ref[...] = u


def _odd_decode(x, p, b0, b1, cw, w3, li):
    m, d = x.shape
    col = lambda c: pl.BlockSpec((m, d), lambda i, c=c: (0, c))
    full = lambda a: pl.BlockSpec(a.shape, lambda i: (0,) * a.ndim)
    return pl.pallas_call(
        _odd_decode_kernel,
        out_shape=(jax.ShapeDtypeStruct((m, d), F32), jax.ShapeDtypeStruct((m, d), F32)),
        grid=(1,),
        in_specs=[full(x), col(0), col(1), col(2), col(3), full(b0), full(b1), full(cw), _layer_spec(w3, li)],
        out_specs=(pl.BlockSpec((m, d), lambda i: (0, 0)), pl.BlockSpec((m, d), lambda i: (0, 0))),
        compiler_params=_cparams("arbitrary"),
        name="odd_decode",
    )(x, p, p, p, p, b0, b1, cw, w3)


def _tile(n, want):
    t = math.gcd(n, want)
    assert t == n or t % 8 == 0, (n, want)
    return t


def _pad_lanes(v):
    return jnp.pad(v.astype(F32), (0, LANE - v.shape[0])).reshape(1, LANE)


def kernel(x_prompt, x_sample, cache_k, cache_v, page_table, state_gdn, state_gdn_conv, state_shortconv, norm_w, rel_table, w_in_even, w_out_even, qn_w, kn_w, lam_q1, lam_k1, lam_q2, lam_k2, subln_w, gdn_conv_w, gdn_a_log, gdn_dt_bias, gdn_norm_w, w_in_odd, sc_conv_w, w_out_odd):
    nbp, t, d = x_prompt.shape
    nbs = x_sample.shape[0]
    page = cache_k.shape[2]
    assert x_sample.shape[1] == 1 and page >= MAX_DISTANCE and t % 8 == 0 and DEPTH % 2 == 0
    rows = _tile(t, SEQ_ROWS)
    tq = rows
    assert tq >= MAX_DISTANCE

    xp = x_prompt.reshape(nbp * t, d)
    xs = x_sample.reshape(nbs, d)

    g64 = jnp.arange(W_A) // D_HA
    bd = jnp.where(g64[:, None] == g64[None, :], 1.0 / D_HA, 0.0).astype(BF16)
    bias_p = _bias_tiles(rel_table.astype(F32), tq)
    bias_s = _bias_decode(rel_table.astype(F32), page)

    n_even = (DEPTH + 1) // 2
    w_in_e, w_out_e = w_in_even.astype(BF16), w_out_even.astype(BF16)
    w_in_o, w_out_o = w_in_odd.astype(BF16), w_out_odd.astype(BF16)
    kv_p = kv_s = s_s = None
    sp, gcp, scp = [], [], []
    gcs, scs = [], []
    ei = oi = 0
    for li in range(DEPTH):
        nw = norm_w[li].reshape(1, d)
        if li % 2 == 0:
            lambda_init = 0.8 - 0.6 * math.exp(-0.3 * li)
            w_ab = jnp.pad(w_in_even[ei, :, P_MAIN:], ((0, 0), (0, LANE - 2 * H_B))).astype(BF16)
            qw = jnp.tile(qn_w[ei], W_A // D_HA).reshape(1, W_A)
            kw = jnp.tile(kn_w[ei], W_A // D_HA).reshape(1, W_A)
            sw = subln_w[ei].reshape(1, DA)
            lam = (jnp.exp(jnp.sum(lam_q1[ei] * lam_k1[ei]).astype(F32))
                   - jnp.exp(jnp.sum(lam_q2[ei] * lam_k2[ei]).astype(F32)) + lambda_init).reshape(1)
            cw = gdn_conv_w[ei]
            alog = _pad_lanes(gdn_a_log[ei])
            dtb = _pad_lanes(gdn_dt_bias[ei])
            gnw = gdn_norm_w[ei].reshape(1, DV_B)

            qb, kb, *kv_p, p, ab, tail = _even_in(xp, nw, w_in_e, w_ab, qw, kw, bd, rows, ei, n_even, kv_p,
                                                  conv_w=cw, seq_len=t)
            oa = _attn_prompt(lam, qb, kb, bias_p, p, sw, nbp, t, tq, 1.0 - lambda_init)
            ob, s_new = _gdn_prompt(p, ab, alog, dtb, gnw, nbp, t, rows)
            prev_p = (oa, ob, w_out_e, ei)
            sp.append(s_new)
            gcp.append(tail[:, 8 - (GDN_CONV - 1):, :])

            qb, kb, *kv_s, p, ab = _even_in(xs, nw, w_in_e, w_ab, qw, kw, bd, nbs, ei, n_even, kv_s)
            oa = _attn_decode(page_table, lam, qb, kb, cache_k, cache_v, ei, bias_s, p, sw, 1.0 - lambda_init)
            conv0_t = jnp.swapaxes(state_gdn_conv[ei], 0, 1)
            qn, kn, vv, eg, beta, qk, conv_new = _gdn_decode_prep(p, ab, conv0_t, cw, alog, dtb)
            bb = math.gcd(nbs, DECODE_STATE_SEQS)
            to_cols = lambda a: a.reshape(nbs // bb, bb, H_B, DK_B).transpose(0, 2, 3, 1)
            s_s, ob = _gdn_decode(state_gdn, ei, to_cols(qn), to_cols(kn), vv, eg, beta, qk, p, gnw, bb, s_s)
            xs = _out_proj(xs, oa, ob, w_out_e, ei, nbs)
            gcs.append(jnp.swapaxes(conv_new, 0, 1))
            ei += 1
        else:
            cw = sc_conv_w[oi]
            xp, tail = _odd_prompt(xp, *prev_p, nw, w_in_o, cw, w_out_o, oi, nbp, t, rows)
            scp.append(tail[:, 8 - (SC_WIDTH - 1):, :])

            p = _norm_proj(xs, nw, w_in_o, oi, nbs, DECODE_PROJ_COLS)
            buf0 = state_shortconv[oi]
            xs, u = _odd_decode(xs, p, buf0[:, 0, :], buf0[:, 1, :], cw, w_out_o, oi)
            scs.append(jnp.stack([buf0[:, 1, :], u], axis=1))
            oi += 1

    leaf_p = lambda a: a.reshape(n_even, nbp, t, H_A, DA)
    leaf_s = lambda a: a.reshape(n_even, nbs, 1, H_A, DA)
    return (xp.reshape(nbp, t, d), xs.reshape(nbs, 1, d),
            leaf_p(kv_p[0]), leaf_p(kv_p[1]), jnp.stack(sp), jnp.stack(gcp), jnp.stack(scp),
            leaf_s(kv_s[0]), leaf_s(kv_s[1]), s_s, jnp.stack(gcs), jnp.stack(scs))
```

```python
import functools
import math

import jax
import jax.numpy as jnp
from jax import lax
from jax.experimental import pallas as pl
from jax.experimental.pallas import tpu as pltpu

F32, BF16 = jnp.float32, jnp.bfloat16

DEPTH = 4
H_A, D_HA = 4, 64
DA = 2 * D_HA
W_A = H_A * DA
H_B, DK_B, DV_B = 4, 128, 128
W_B = H_B * DV_B
QKV_B = 2 * H_B * DK_B + H_B * DV_B
GDN_CONV, GDN_CHUNK, SC_WIDTH = 4, 64, 3
NUM_BUCKETS, MAX_EXACT, MAX_DISTANCE = 32, 16, 128
EPS, NEG = 1e-6, -1e30
LOG2E = math.log2(math.e)
LANE = 128
P_MAIN = 4 * W_A + QKV_B + W_B
VMEM_LIMIT = 48 * 1024 * 1024
LOCAL_GROUP = 8
SEQ_ROWS = 512
DECODE_STATE_SEQS = 8
DECODE_PROJ_COLS = 1024
DECODE_ATTN_SEQS = 2


def _layer_spec(w3, li, cols=None):
    return pl.BlockSpec((None, w3.shape[1], cols or w3.shape[2]), lambda *_: (li, 0, 0))


def _cparams(*sem):
    return pltpu.CompilerParams(dimension_semantics=sem, vmem_limit_bytes=VMEM_LIMIT)


def _silu(z):
    h = 0.5 * z
    return h + h * jnp.tanh(h)


def _sigmoid(z):
    return 1.0 / (1.0 + jnp.exp(-z))


def _dot(a, b):
    return jnp.dot(a.astype(BF16), b.astype(BF16), preferred_element_type=F32)


def _dot_nt(a, b):
    return lax.dot_general(a.astype(BF16), b.astype(BF16), (((1,), (1,)), ((), ())), preferred_element_type=F32)


def _rms_rows(x, w):
    return x * lax.rsqrt(jnp.mean(x * x, axis=-1, keepdims=True) + EPS) * w


def _norm_proj_kernel(x_ref, nw_ref, w_ref, o_ref, h_ref):
    @pl.when(pl.program_id(1) == 0)
    def _():
        h_ref[...] = _rms_rows(x_ref[...], nw_ref[...]).astype(BF16)

    o_ref[...] = jnp.dot(h_ref[...], w_ref[...], preferred_element_type=F32).astype(BF16)


def _norm_proj(x, nw, w3, li, tm, tn):
    m, d = x.shape
    n = w3.shape[2]
    return pl.pallas_call(
        _norm_proj_kernel,
        out_shape=jax.ShapeDtypeStruct((m, n), BF16),
        grid=(m // tm, n // tn),
        in_specs=[pl.BlockSpec((tm, d), lambda i, j: (i, 0)),
                  pl.BlockSpec((1, d), lambda i, j: (0, 0)),
                  pl.BlockSpec((None, d, tn), lambda i, j: (li, 0, j))],
        out_specs=pl.BlockSpec((tm, tn), lambda i, j: (i, j)),
        scratch_shapes=[pltpu.VMEM((tm, d), BF16)],
        compiler_params=_cparams("arbitrary", "arbitrary"),
        name="norm_proj",
    )(x, nw, w3)


def _even_in_kernel(x_ref, nw_ref, w_ref, wab_ref, qw_ref, kw_ref, bd_ref, *refs, n_alias, tiles_per_seq):
    if tiles_per_seq:
        cw_ref = refs[0]
        qo_ref, kb_ref, kf_ref, vf_ref, p_ref, ab_ref, tail_ref, carry_ref = refs[1 + n_alias:]
    else:
        qo_ref, kb_ref, kf_ref, vf_ref, p_ref, ab_ref = refs[n_alias:]
    tm = x_ref.shape[0]
    h = _rms_rows(x_ref[...], nw_ref[...]).astype(BF16)
    proj = lambda c: jnp.dot(h, w_ref[:, c * W_A:(c + 1) * W_A], preferred_element_type=F32)
    ab_ref[...] = jnp.dot(h, wab_ref[...], preferred_element_type=F32)
    bd = bd_ref[...]

    def group_norm(x, w):
        ms = jnp.dot((x * x).astype(BF16), bd, preferred_element_type=F32)
        return x * lax.rsqrt(ms + EPS) * w

    def leaf(ref, val):
        for hd in range(H_A):
            ref[pl.ds(hd, tm, stride=H_A), :] = val[:, hd * DA:(hd + 1) * DA]

    qn = group_norm(proj(0), qw_ref[...])
    qo_ref[...] = (qn * (D_HA ** -0.5 * LOG2E)).astype(BF16)
    kn = group_norm(proj(1), kw_ref[...])
    kb_ref[...] = kn.astype(BF16)
    leaf(kf_ref, kn)
    v = proj(2)
    leaf(vf_ref, v)
    p_ref[:, 0:W_A] = v.astype(BF16)
    if tiles_per_seq:
        first = pl.program_id(0) % tiles_per_seq == 0

        @pl.when(first)
        def _():
            carry_ref[...] = jnp.zeros(carry_ref.shape, F32)

    c_qkv = (4 * W_A) // W_A
    for c in range(3, P_MAIN // W_A):
        y = proj(c)
        seg = c - c_qkv
        if tiles_per_seq and 0 <= seg < QKV_B // W_B:
            cols = slice(seg * W_B, (seg + 1) * W_B)
            raw = y
            y = _silu(_shifted_conv(raw, carry_ref[:, cols], cw_ref[:, cols]))
            carry_ref[:, cols] = raw[tm - 8:tm]
            tail_ref[:, cols] = raw[tm - 8:tm]
        p_ref[:, (c - 2) * W_A:(c - 1) * W_A] = y.astype(BF16)


def _even_in(x, nw, w3, w_ab, qw, kw, bd, tm, ei, n_layers, kv_prev, conv_w=None, seq_len=None):
    m, d = x.shape
    tiles_per_seq = 0 if conv_w is None else seq_len // tm
    n_rest = P_MAIN - 2 * W_A
    full = lambda a: pl.BlockSpec(a.shape, lambda i: (0,) * a.ndim)
    row = lambda n: pl.BlockSpec((tm, n), lambda i: (i, 0))
    leaf = pl.BlockSpec((None, tm * H_A, DA), lambda i: (ei, i, 0))
    leaf_shape = jax.ShapeDtypeStruct((n_layers, m * H_A, DA), F32)
    n_alias = 0 if kv_prev is None else 2
    out_shape = [jax.ShapeDtypeStruct((m, W_A), BF16), jax.ShapeDtypeStruct((m, W_A), BF16), leaf_shape, leaf_shape,
                 jax.ShapeDtypeStruct((m, n_rest), BF16), jax.ShapeDtypeStruct((m, LANE), F32)]
    out_specs = [row(W_A), row(W_A), leaf, leaf, row(n_rest), row(LANE)]
    in_specs = [row(d), full(nw), _layer_spec(w3, ei, P_MAIN), full(w_ab), full(qw), full(kw), full(bd)]
    args = [x, nw, w3, w_ab, qw, kw, bd]
    scratch = []
    if tiles_per_seq:
        in_specs.append(full(conv_w))
        args.append(conv_w)
        out_shape.append(jax.ShapeDtypeStruct((m // seq_len, 8, QKV_B), F32))
        out_specs.append(pl.BlockSpec((None, 8, QKV_B), lambda i: (i // tiles_per_seq, 0, 0)))
        scratch.append(pltpu.VMEM((8, QKV_B), F32))
    n_in = len(args)
    return pl.pallas_call(
        functools.partial(_even_in_kernel, n_alias=n_alias, tiles_per_seq=tiles_per_seq),
        out_shape=tuple(out_shape),
        grid=(m // tm,),
        in_specs=in_specs + [pl.BlockSpec(memory_space=pl.ANY)] * n_alias,
        out_specs=tuple(out_specs),
        input_output_aliases={} if kv_prev is None else {n_in: 2, n_in + 1: 3},
        scratch_shapes=scratch,
        compiler_params=_cparams("arbitrary"),
        name="even_in",
    )(*args, *(kv_prev or ()))


def _t5_bias(n, tab_ref, h):
    nf = jnp.maximum(n, 1).astype(F32)
    large = MAX_EXACT + (jnp.log(nf / MAX_EXACT) / math.log(MAX_DISTANCE / MAX_EXACT)
                         * (NUM_BUCKETS - MAX_EXACT)).astype(jnp.int32)
    large = jnp.minimum(large, NUM_BUCKETS - 1)
    bkt = jnp.where(n < MAX_EXACT, n, large)
    out = jnp.zeros(n.shape, F32)
    for b in range(NUM_BUCKETS):
        out = jnp.where(bkt == b, tab_ref[b, h], out)
    return (out - tab_ref[NUM_BUCKETS - 1, h]) * LOG2E


def _bias_tiles_kernel(tab_ref, o_ref, *, tq):
    h = pl.program_id(0)
    i = lax.broadcasted_iota(jnp.int32, (tq, tq), 0)
    j = lax.broadcasted_iota(jnp.int32, (tq, tq), 1)
    n0 = i - j
    o_ref[0, 0] = jnp.where(n0 >= 0, _t5_bias(jnp.maximum(n0, 0), tab_ref, h), NEG)
    o_ref[0, 1] = _t5_bias(n0 + tq, tab_ref, h)


def _bias_tiles(rel_table, tq):
    return pl.pallas_call(
        functools.partial(_bias_tiles_kernel, tq=tq),
        out_shape=jax.ShapeDtypeStruct((H_A, 2, tq, tq), F32),
        grid=(H_A,),
        in_specs=[pl.BlockSpec(memory_space=pltpu.SMEM)],
        out_specs=pl.BlockSpec((1, 2, tq, tq), lambda h: (h, 0, 0, 0)),
        compiler_params=_cparams("arbitrary"),
        name="bias_tiles",
    )(rel_table)


def _bias_decode_kernel(tab_ref, o_ref, *, page):
    w = page * H_A
    row = lax.broadcasted_iota(jnp.int32, (2 * H_A, w), 0)
    lane = lax.broadcasted_iota(jnp.int32, (2 * H_A, w), 1)
    valid = (lane & (H_A - 1)) == (row >> 1)
    n = page - (lane >> 2)
    near = jnp.zeros((2 * H_A, w), F32)
    new = jnp.zeros((2 * H_A, LANE), F32)
    for h in range(H_A):
        near = jnp.where((row >> 1) == h, _t5_bias(n, tab_ref, h), near)
        new = jnp.where((row[:, :LANE] >> 1) == h, _t5_bias(jnp.zeros((2 * H_A, LANE), jnp.int32), tab_ref, h), new)
    o_ref[:, 0:w] = jnp.where(valid, 0.0, NEG)
    o_ref[:, w:2 * w] = jnp.where(valid, near, NEG)
    o_ref[:, 2 * w:2 * w + LANE] = new


def _bias_decode(rel_table, page):
    assert H_A == 4
    return pl.pallas_call(
        functools.partial(_bias_decode_kernel, page=page),
        out_shape=jax.ShapeDtypeStruct((2 * H_A, 2 * page * H_A + LANE), F32),
        in_specs=[pl.BlockSpec(memory_space=pltpu.SMEM)],
        out_specs=pl.BlockSpec(memory_space=pltpu.VMEM),
        name="bias_decode",
    )(rel_table)


def _attn_finish(o, sw, z, out_scale):
    ms = jnp.mean(o * o, axis=-1, keepdims=True)
    return o * lax.rsqrt(ms + EPS) * sw * out_scale * _silu(z.astype(F32))


def _attn_prompt_kernel(lam_ref, q_ref, k_ref, v_ref, bias_ref, za_ref, sw_ref, o_ref, m, l, a, *, tq, out_scale):
    reps = tq // LANE
    lane = lax.broadcasted_iota(jnp.int32, (tq, DA), 1)

    def q_block(qi, carry):
        qrows = pl.ds(pl.multiple_of(qi * tq, tq), tq)
        q = q_ref[qrows, :].astype(F32)
        q2 = jnp.concatenate([jnp.where(lane < D_HA, q, 0.0), jnp.where(lane >= D_HA, q, 0.0)], axis=0).astype(BF16)
        m[...] = jnp.full(m.shape, NEG, F32)
        l[...] = jnp.zeros(l.shape, F32)
        a[...] = jnp.zeros(a.shape, F32)

        def step(j, bias):
            rows = pl.ds(pl.multiple_of(j * tq, tq), tq)
            s = lax.dot_general(q2, k_ref[rows, :], (((1,), (1,)), ((), ())), preferred_element_type=F32)
            if bias is not None:
                s = s + jnp.concatenate([bias, bias], axis=0)
            m_prev = m[...]
            m_new = jnp.maximum(m_prev, jnp.max(s, axis=1, keepdims=True))
            p = jnp.exp2(s - jnp.concatenate([m_new] * reps, axis=1))
            alpha = jnp.exp2(m_prev - m_new)
            l[...] = alpha * l[...] + jnp.sum(p, axis=1, keepdims=True)
            a[...] = alpha * a[...] + jnp.dot(p.astype(BF16), v_ref[rows, :], preferred_element_type=F32)
            m[...] = m_new

        def far(j, c):
            step(j, None)
            return c

        lax.fori_loop(0, jnp.maximum(qi - 1, 0), far, 0)

        @pl.when(qi >= 1)
        def _():
            step(qi - 1, bias_ref[0, 1])

        step(qi, bias_ref[0, 0])
        o = a[...] / l[...]
        o = o[:tq] - lam_ref[0] * o[tq:]
        o_ref[qrows, :] = _attn_finish(o, sw_ref[...], za_ref[qrows, :], out_scale).astype(BF16)
        return carry

    lax.fori_loop(0, q_ref.shape[0] // tq, q_block, 0)


def _attn_prompt(lam, q, k, bias, p, sw, nb, t, tq, out_scale):
    m = q.shape[0]
    za0 = W_A // DA
    v0 = 0
    seq = lambda c0: pl.BlockSpec((t, DA), lambda b, h, c0=c0: (b, c0 + h))
    return pl.pallas_call(
        functools.partial(_attn_prompt_kernel, tq=tq, out_scale=out_scale),
        out_shape=jax.ShapeDtypeStruct((m, W_A), BF16),
        grid=(nb, H_A),
        in_specs=[pl.BlockSpec(memory_space=pltpu.SMEM),
                  seq(0), seq(0), seq(v0),
                  pl.BlockSpec((1, 2, tq, tq), lambda b, h: (h, 0, 0, 0)),
                  seq(za0),
                  pl.BlockSpec((1, DA), lambda b, h: (0, 0))],
        out_specs=seq(0),
        scratch_shapes=[pltpu.VMEM((2 * tq, LANE), F32), pltpu.VMEM((2 * tq, LANE), F32),
                        pltpu.VMEM((2 * tq, DA), F32)],
        compiler_params=_cparams("arbitrary", "arbitrary"),
        name="attn_prompt",
    )(lam, q, k, p, bias, p, sw)


def _attn_decode_kernel(pt_ref, lam_ref, q_ref, kn_ref, vn_ref, *refs, page, npg, nseq, out_scale):
    ck_refs, cv_refs = refs[:nseq * npg], refs[nseq * npg:2 * nseq * npg]
    bias_ref, za_ref, sw_ref, o_ref = refs[2 * nseq * npg:]
    rows, w = 2 * H_A, page * H_A
    row = lax.broadcasted_iota(jnp.int32, (rows, DA), 0)
    lane = lax.broadcasted_iota(jnp.int32, (rows, DA), 1)

    def head_rows(x):
        out = jnp.zeros((rows, DA), F32)
        for h in range(H_A):
            out = jnp.where((row >> 1) == h, jnp.broadcast_to(x[:, h * DA:(h + 1) * DA], (rows, DA)), out)
        return out

    sw = sw_ref[...]
    for sq in range(nseq):
        q8 = jnp.where((lane >> 6) == (row & 1), head_rows(q_ref[sq].astype(F32)), 0.0)
        q8b = q8.astype(BF16)
        s_all = []
        for j in range(npg):
            s = lax.dot_general(q8b, ck_refs[sq * npg + j][...].astype(BF16), (((1,), (1,)), ((), ())),
                                preferred_element_type=F32)
            s_all.append(s + (bias_ref[:, w:2 * w] if j == npg - 1 else bias_ref[:, 0:w]))
        s_new = (jnp.sum(q8 * head_rows(kn_ref[sq].astype(F32)), axis=1, keepdims=True)
                 + bias_ref[:, 2 * w:2 * w + 1])
        m = s_all[0]
        for s in s_all[1:]:
            m = jnp.maximum(m, s)
        m = jnp.maximum(jnp.max(m, axis=1, keepdims=True), s_new)
        p_new = jnp.exp2(s_new - m)
        l = p_new
        acc = p_new * head_rows(vn_ref[sq].astype(F32))
        for j in range(npg):
            p = jnp.exp2(s_all[j] - m)
            l = l + jnp.sum(p, axis=1, keepdims=True)
            acc = acc + jnp.dot(p.astype(BF16), cv_refs[sq * npg + j][...].astype(BF16), preferred_element_type=F32)
        o = acc / l
        z = za_ref[sq]
        outs = []
        for h in range(H_A):
            oh = o[2 * h:2 * h + 1, :] - lam_ref[0] * o[2 * h + 1:2 * h + 2, :]
            outs.append(_attn_finish(oh, sw, z[:, h * DA:(h + 1) * DA], out_scale))
        o_ref[sq] = jnp.concatenate(outs, axis=1).astype(BF16)


def _attn_decode(page_table, lam, q, kn, cache_k, cache_v, ei, bias, p, sw, out_scale):
    nb, npg = page_table.shape
    page = cache_k.shape[2]
    nseq = math.gcd(nb, DECODE_ATTN_SEQS)
    ck = cache_k.reshape(cache_k.shape[0], cache_k.shape[1], page * H_A, DA)
    cv = cache_v.reshape(cache_v.shape[0], cache_v.shape[1], page * H_A, DA)
    r3 = lambda a: a.reshape(nb, 1, a.shape[-1])
    vec = lambda c: pl.BlockSpec((nseq, 1, W_A), lambda b, pt, c=c: (b, 0, c))
    cache = [pl.BlockSpec((None, None, page * H_A, DA),
                          lambda b, pt, sq=sq, j=j: (ei, pt[(b * nseq + sq) * npg + j], 0, 0))
             for sq in range(nseq) for j in range(npg)]
    out = pl.pallas_call(
        functools.partial(_attn_decode_kernel, page=page, npg=npg, nseq=nseq, out_scale=out_scale),
        out_shape=jax.ShapeDtypeStruct((nb, 1, W_A), BF16),
        grid_spec=pltpu.PrefetchScalarGridSpec(
            num_scalar_prefetch=1,
            grid=(nb // nseq,),
            in_specs=[pl.BlockSpec(memory_space=pltpu.SMEM), vec(0), vec(0), vec(0)] + cache + cache
                     + [pl.BlockSpec(bias.shape, lambda b, pt: (0, 0)),
                        vec(1),
                        pl.BlockSpec((1, DA), lambda b, pt: (0, 0))],
            out_specs=pl.BlockSpec((nseq, 1, W_A), lambda b, pt: (b, 0, 0))),
        compiler_params=_cparams("arbitrary"),
        name="attn_decode",
    )(page_table.reshape(-1), lam, r3(q), r3(kn), r3(p), *([ck] * (nseq * npg)), *([cv] * (nseq * npg)), bias, r3(p), sw)
    return out.reshape(nb, W_A)


def _shifted_conv(x, prev, w):
    r, c = x.shape
    taps = w.shape[0]
    x3 = x.reshape(r // 8, 8, c)
    sub = lax.broadcasted_iota(jnp.int32, x3.shape, 1)
    y = x3 * w[taps - 1:taps, :].reshape(1, 1, c)
    for s in range(1, taps):
        rot = pltpu.roll(x3, s, 1)
        before = jnp.concatenate([pltpu.roll(prev, s, 0)[None], rot[:-1]], axis=0)
        y = y + jnp.where(sub < s, before, rot) * w[taps - 1 - s:taps - s, :].reshape(1, 1, c)
    return y.reshape(r, c)


def _gates(ab, alog, dtb):
    x = ab + dtb
    sp = jnp.maximum(x, 0.0) + jnp.log(1.0 + jnp.exp(-jnp.abs(x)))
    return -jnp.exp(alog) * sp, _sigmoid(ab)


def _l2norm(x):
    return x * lax.rsqrt(jnp.sum(x * x, axis=-1, keepdims=True) + EPS)


def _rms(x, w):
    return x * lax.rsqrt(jnp.mean(x * x, axis=-1, keepdims=True) + EPS) * w


def _level_masks(c):
    i = lax.broadcasted_iota(jnp.int32, (c, c), 0)
    j = lax.broadcasted_iota(jnp.int32, (c, c), 1)
    masks = []
    s = 1
    while s < c:
        sh = s.bit_length() - 1
        masks.append((((i >> (sh + 1)) == (j >> (sh + 1))) & ((i >> sh) != (j >> sh)) & (i > j)).astype(F32))
        s *= 2
    return masks


def _gdn_prompt_kernel(cq_ref, ck_ref, cv_ref, zb_ref, ab_ref, alog_ref, dtb_ref, gnw_ref,
                       ob_ref, s_ref, g_ref, beta_ref, u_ref, wq_ref, ak_ref, gl_ref, *, rows, chunk, nseq):
    t = pl.program_id(1)
    c = chunk

    @pl.when(t == 0)
    def _():
        s_ref[...] = jnp.zeros(s_ref.shape, F32)

    ii = lax.broadcasted_iota(jnp.int32, (c, c), 0)
    jj = lax.broadcasted_iota(jnp.int32, (c, c), 1)
    incl = ii >= jj
    ltri = incl.astype(BF16)
    masks = _level_masks(c)

    def prep(b, ci):
        rs = pl.ds(pl.multiple_of(ci * c, c), c)
        gch = g_ref[b, rs, :]
        g1 = gch.astype(BF16)
        r1 = gch - g1.astype(F32)
        g2 = r1.astype(BF16)
        g3 = (r1 - g2.astype(F32)).astype(BF16)
        d = functools.partial(jnp.dot, preferred_element_type=F32)
        gc = d(ltri, g1) + d(ltri, g2) + d(ltri, g3)
        gct = jnp.concatenate([gc, gc], axis=0).T
        ge = jnp.exp(gc)
        kdec = jnp.exp(gc[c - 1:c, :] - gc)
        gl_ref[b, ci] = ge[c - 1:c, :]
        return rs, gc, gct, ge, kdec, beta_ref[b, rs, :]

    def local(b, ig):
        items = []
        for ci in [LOCAL_GROUP * ig + k for k in range(LOCAL_GROUP)]:
            rs, gc, gct, ge, kdec, bch = prep(b, ci)
            for h in range(H_B):
                hc = slice(h * DK_B, (h + 1) * DK_B)
                qh = _l2norm(cq_ref[b, rs, hc].astype(F32)) * (DK_B ** -0.5)
                kh = _l2norm(ck_ref[b, rs, hc].astype(F32))
                vh = cv_ref[b, rs, hc].astype(F32)
                bcol = bch[:, H_B + h:H_B + h + 1]
                gecol = ge[:, h:h + 1]
                decay = jnp.exp(jnp.where(incl, gc[:, h:h + 1] - gct[h:h + 1, :c], NEG))
                kbeta = kh * bcol
                wq_ref[b, ci, h, c:2 * c] = (qh * gecol).astype(BF16)
                ak_ref[b, ci, h, c:c + DK_B] = (kh * kdec[:, h:h + 1]).T.astype(BF16)
                items.append(dict(ci=ci, h=h, rs=rs, hc=hc, decay=decay, qk=(kbeta, qh, kh),
                                  rhs=jnp.concatenate([vh * bcol, kbeta * gecol], axis=1)))
        for it in items:
            kbeta, qh, kh = it.pop("qk")
            both = _dot_nt(jnp.concatenate([kbeta, qh], axis=0), kh)
            it["mm"] = jnp.where(ii > jj, both[:c] * it["decay"], 0.0)
            ak_ref[b, it["ci"], it["h"], 0:c] = (both[c:] * it["decay"]).astype(BF16)
        for it in items:
            it["pp"] = -(it["mm"] * masks[0])
        for mk in masks[1:]:
            for it in items:
                e = it["mm"] * mk
                it["x"] = e + _dot(it["pp"], e)
            for it in items:
                it["pp"] = it["pp"] - (it["x"] + _dot(it["x"], it["pp"]))
        for it in items:
            uw = it["rhs"] + _dot(it["pp"], it["rhs"])
            u_ref[b, it["rs"], it["hc"]] = uw[:, :DV_B]
            wq_ref[b, it["ci"], it["h"], 0:c] = uw[:, DV_B:].astype(BF16)

    for b in range(nseq):
        g, beta = _gates(ab_ref[b], alog_ref[...], dtb_ref[...])
        g_ref[b] = g
        beta_ref[b] = beta

        def local_b(ig, carry, b=b):
            local(b, ig)
            return carry

        lax.fori_loop(0, rows // (LOCAL_GROUP * c), local_b, 0)
    gnw = gnw_ref[...]

    def scan(ci, carry):
        rs = pl.ds(pl.multiple_of(ci * c, c), c)
        chains = [(b, h) for b in range(nseq) for h in range(H_B)]
        hc = lambda h: slice(h * DK_B, (h + 1) * DK_B)
        s_old = [s_ref[b, h] for b, h in chains]
        r = [jnp.dot(wq_ref[b, ci, h], s_old[k].astype(BF16), preferred_element_type=F32)
             for k, (b, h) in enumerate(chains)]
        v_new = [u_ref[b, rs, hc(h)] - r[k][:c] for k, (b, h) in enumerate(chains)]
        r2 = [jnp.dot(ak_ref[b, ci, h], v_new[k].astype(BF16), preferred_element_type=F32)
              for k, (b, h) in enumerate(chains)]
        for k, (b, h) in enumerate(chains):
            s_ref[b, h] = s_old[k] * gl_ref[b, ci][:, h:h + 1] + r2[k][c:]
            ob_ref[b, rs, hc(h)] = (_rms(r[k][c:] + r2[k][:c], gnw)
                                    * _silu(zb_ref[b, rs, hc(h)].astype(F32))).astype(BF16)
        return carry

    lax.fori_loop(0, rows // c, scan, 0)


def _gdn_prompt(p, ab, alog, dtb, gnw, nb, t, rows):
    m = p.shape[0]
    nt = t // rows
    c = math.gcd(GDN_CHUNK, t)
    nseq = 2 if nb % 2 == 0 else 1
    p3 = p.reshape(nb, t, p.shape[1])
    col = lambda cidx: pl.BlockSpec((nseq, rows, W_B), lambda b, i, cidx=cidx: (b, i, cidx))
    full = lambda a: pl.BlockSpec(a.shape, lambda b, i: (0,) * a.ndim)
    c0 = (2 * W_A) // W_B
    ob, s_new = pl.pallas_call(
        functools.partial(_gdn_prompt_kernel, rows=rows, chunk=c, nseq=nseq),
        out_shape=(jax.ShapeDtypeStruct((nb, t, W_B), BF16), jax.ShapeDtypeStruct((nb, H_B, DK_B, DV_B), F32)),
        grid=(nb // nseq, nt),
        in_specs=[col(c0), col(c0 + 1), col(c0 + 2), col(c0 + 3),
                  pl.BlockSpec((nseq, rows, LANE), lambda b, i: (b, i, 0)),
                  full(alog), full(dtb), full(gnw)],
        out_specs=(pl.BlockSpec((nseq, rows, W_B), lambda b, i: (b, i, 0)),
                   pl.BlockSpec((nseq, H_B, DK_B, DV_B), lambda b, i: (b, 0, 0, 0))),
        scratch_shapes=[pltpu.VMEM((nseq, rows, LANE), F32), pltpu.VMEM((nseq, rows, LANE), F32),
                        pltpu.VMEM((nseq, rows, W_B), F32),
                        pltpu.VMEM((nseq, rows // c, H_B, 2 * c, DK_B), BF16),
                        pltpu.VMEM((nseq, rows // c, H_B, c + DK_B, c), BF16),
                        pltpu.VMEM((nseq, rows // c, 1, LANE), F32)],
        compiler_params=_cparams("arbitrary", "arbitrary"),
        name="gdn_prompt",
    )(p3, p3, p3, p3, ab.reshape(nb, t, LANE), alog, dtb, gnw)
    return ob.reshape(m, W_B), s_new


def _gdn_decode_prep_kernel(pq_ref, pk_ref, pv_ref, ab_ref, c0_ref, cw_ref, alog_ref, dtb_ref,
                            q_ref, k_ref, v_ref, eg_ref, beta_ref, qk_ref, cn_ref):
    cw = cw_ref[...]
    taps = cw.shape[0]
    outs = (q_ref, k_ref, v_ref)
    for seg, ref in enumerate((pq_ref, pk_ref, pv_ref)):
        cols = slice(seg * W_B, (seg + 1) * W_B)
        x = ref[...].astype(F32)
        y = x * cw[taps - 1:taps, cols]
        for j in range(taps - 1):
            y = y + c0_ref[j, :, cols] * cw[j:j + 1, cols]
            if j >= 1:
                cn_ref[j - 1, :, cols] = c0_ref[j, :, cols]
        cn_ref[taps - 2, :, cols] = x
        outs[seg][...] = _silu(y)
    lane = lax.broadcasted_iota(jnp.int32, eg_ref.shape, 1)
    qk = jnp.zeros(eg_ref.shape, F32)
    for h in range(H_B):
        hc = slice(h * DK_B, (h + 1) * DK_B)
        qh = _l2norm(q_ref[:, hc]) * (DK_B ** -0.5)
        kh = _l2norm(k_ref[:, hc])
        q_ref[:, hc] = qh
        k_ref[:, hc] = kh
        qk = jnp.where(lane == h, jnp.sum(qh * kh, axis=-1, keepdims=True), qk)
    g, beta = _gates(ab_ref[...], alog_ref[...], dtb_ref[...])
    eg_ref[...] = jnp.exp(g)
    beta_ref[...] = beta
    qk_ref[...] = qk


def _gdn_decode_prep(p, ab, conv0_t, cw, alog, dtb):
    nb = p.shape[0]
    c0 = (2 * W_A) // W_B
    col = lambda cidx: pl.BlockSpec((nb, W_B), lambda i, cidx=cidx: (0, cidx))
    full = lambda a: pl.BlockSpec(a.shape, lambda i: (0,) * a.ndim)
    wide = jax.ShapeDtypeStruct((nb, W_B), F32)
    narrow = jax.ShapeDtypeStruct((nb, LANE), F32)
    ospec = lambda s: pl.BlockSpec(s.shape, lambda i: (0,) * len(s.shape))
    outs = (wide, wide, wide, narrow, narrow, narrow, jax.ShapeDtypeStruct(conv0_t.shape, F32))
    return pl.pallas_call(
        _gdn_decode_prep_kernel,
        out_shape=outs,
        grid=(1,),
        in_specs=[col(c0), col(c0 + 1), col(c0 + 2),
                  full(ab), full(conv0_t), full(cw), full(alog), full(dtb)],
        out_specs=tuple(ospec(s) for s in outs),
        compiler_params=_cparams("arbitrary"),
        name="gdn_decode_prep",
    )(p, p, p, ab, conv0_t, cw, alog, dtb)


def _gdn_decode_kernel(s_ref, qt_ref, kt_ref, v_ref, eg_ref, beta_ref, qk_ref, zb_ref, gnw_ref, *refs, bb, n_alias):
    so_ref, ob_ref, o_scr = refs[n_alias:]
    v = v_ref[...]
    eg = eg_ref[...]
    beta = beta_ref[...]
    qk = qk_ref[...]
    for h in range(H_B):
        hc = slice(h * DV_B, (h + 1) * DV_B)
        qt = qt_ref[h]
        kt = kt_ref[h]
        for i in range(bb):
            s_old = s_ref[i, h]
            kc = kt[:, i:i + 1]
            qc = qt[:, i:i + 1]
            egs = eg[i:i + 1, h:h + 1]
            ks = jnp.sum(s_old * kc, axis=0, keepdims=True)
            qs = jnp.sum(s_old * qc, axis=0, keepdims=True)
            v_new = beta[i:i + 1, H_B + h:H_B + h + 1] * (v[i:i + 1, hc] - egs * ks)
            o_scr[i:i + 1, hc] = egs * qs + qk[i:i + 1, h:h + 1] * v_new
            so_ref[i, h] = s_old * egs + kc * v_new
    gnw = gnw_ref[...]
    z = zb_ref[...]
    outs = [_rms(o_scr[:, h * DV_B:(h + 1) * DV_B], gnw) * _silu(z[:, h * DV_B:(h + 1) * DV_B].astype(F32))
            for h in range(H_B)]
    ob_ref[...] = jnp.concatenate(outs, axis=1).astype(BF16)


def _gdn_decode(state, ei, qt, kt, v, eg, beta, qk, p, gnw, bb, s_prev):
    nb = v.shape[0]
    ns = nb // bb
    narrow = pl.BlockSpec((bb, LANE), lambda i: (i, 0))
    tr = pl.BlockSpec((None, H_B, DK_B, bb), lambda i: (i, 0, 0, 0))
    zb0 = (2 * W_A + QKV_B) // W_B
    n_alias = 0 if s_prev is None else 1
    return pl.pallas_call(
        functools.partial(_gdn_decode_kernel, bb=bb, n_alias=n_alias),
        out_shape=(jax.ShapeDtypeStruct(state.shape, F32), jax.ShapeDtypeStruct((nb, W_B), BF16)),
        grid=(ns,),
        in_specs=[pl.BlockSpec((None, bb, H_B, DK_B, DV_B), lambda i: (ei, i, 0, 0, 0)),
                  tr, tr,
                  pl.BlockSpec((bb, W_B), lambda i: (i, 0)),
                  narrow, narrow, narrow,
                  pl.BlockSpec((bb, W_B), lambda i: (i, zb0)),
                  pl.BlockSpec((1, DV_B), lambda i: (0, 0))] + [pl.BlockSpec(memory_space=pl.ANY)] * n_alias,
        out_specs=(pl.BlockSpec((None, bb, H_B, DK_B, DV_B), lambda i: (ei, i, 0, 0, 0)),
                   pl.BlockSpec((bb, W_B), lambda i: (i, 0))),
        input_output_aliases={9: 0} if n_alias else {},
        scratch_shapes=[pltpu.VMEM((bb, W_B), F32)],
        compiler_params=_cparams("arbitrary"),
        name="gdn_decode",
    )(state, qt, kt, v, eg, beta, qk, p, gnw, *(() if s_prev is None else (s_prev,)))


def _out_proj_kernel(x_ref, oa_ref, ob_ref, w_ref, y_ref):
    y_ref[...] = (x_ref[...]
                  + jnp.dot(oa_ref[...], w_ref[0:W_A, :], preferred_element_type=F32)
                  + jnp.dot(ob_ref[...], w_ref[W_A:W_A + W_B, :], preferred_element_type=F32))


def _out_proj(x, oa, ob, w3, li, tm):
    m, d = x.shape
    return pl.pallas_call(
        _out_proj_kernel,
        out_shape=jax.ShapeDtypeStruct((m, d), F32),
        grid=(m // tm,),
        in_specs=[pl.BlockSpec((tm, d), lambda i: (i, 0)),
                  pl.BlockSpec((tm, W_A), lambda i: (i, 0)),
                  pl.BlockSpec((tm, W_B), lambda i: (i, 0)),
                  _layer_spec(w3, li)],
        out_specs=pl.BlockSpec((tm, d), lambda i: (i, 0)),
        compiler_params=_cparams("arbitrary"),
        name="out_proj",
    )(x, oa, ob, w3)


def _odd_prompt_kernel(x_ref, oa_ref, ob_ref, wp_ref, nw_ref, wi_ref, cw_ref, wo_ref, y_ref, sc_ref, carry_ref, *, rows):
    t = pl.program_id(1)
    d = x_ref.shape[1]

    @pl.when(t == 0)
    def _():
        carry_ref[...] = jnp.zeros(carry_ref.shape, F32)

    x = (x_ref[...]
         + jnp.dot(oa_ref[...], wp_ref[0:W_A, :], preferred_element_type=F32)
         + jnp.dot(ob_ref[...], wp_ref[W_A:W_A + W_B, :], preferred_element_type=F32))
    h = _rms_rows(x, nw_ref[...]).astype(BF16)
    proj = lambda c: jnp.dot(h, wi_ref[:, c * d:(c + 1) * d], preferred_element_type=F32)
    u = proj(1) * proj(2)
    cv = _shifted_conv(u, carry_ref[...], cw_ref[...])
    carry_ref[...] = u[rows - 8:rows]
    g = proj(0) * cv * _silu(proj(3))
    y_ref[...] = x + jnp.dot(g.astype(BF16), wo_ref[...], preferred_element_type=F32)

    @pl.when(t == pl.num_programs(1) - 1)
    def _():
        sc_ref[...] = u[rows - 8:rows]


def _odd_prompt(x, oa, ob, w_prev3, ei, nw, w_in3, cw, w_out3, oi, nb, t, rows):
    m, d = x.shape
    nt = t // rows
    full = lambda a: pl.BlockSpec(a.shape, lambda b, i: (0,) * a.ndim)
    row = lambda n: pl.BlockSpec((rows, n), lambda b, i: (b * nt + i, 0))
    return pl.pallas_call(
        functools.partial(_odd_prompt_kernel, rows=rows),
        out_shape=(jax.ShapeDtypeStruct((m, d), F32), jax.ShapeDtypeStruct((nb, 8, d), F32)),
        grid=(nb, nt),
        in_specs=[row(d), row(W_A), row(W_B), _layer_spec(w_prev3, ei), full(nw), _layer_spec(w_in3, oi), full(cw),
                  _layer_spec(w_out3, oi)],
        out_specs=(row(d), pl.BlockSpec((None, 8, d), lambda b, i: (b, 0, 0))),
        scratch_shapes=[pltpu.VMEM((8, d), F32)],
        compiler_params=_cparams("arbitrary", "arbitrary"),
        name="odd_prompt",
    )(x, oa, ob, w_prev3, nw, w_in3, cw, w_out3)


def _odd_decode_kernel(x_ref, bg_ref, cg_ref, hh_ref, z_ref, b0_ref, b1_ref, cw_ref, w_ref, y_ref, u_ref):
    cw = cw_ref[...]
    u = cg_ref[...].astype(F32) * hh_ref[...].astype(F32)
    cv = b0_ref[...] * cw[0:1, :] + b1_ref[...] * cw[1:2, :] + u * cw[2:3, :]
    g = bg_ref[...].astype(F32) * cv * _silu(z_ref[...].astype(F32))
    y_ref[...] = x_ref[...] + jnp.dot(g.astype(BF16), w_ref[...], preferred_element_type=F32)
    u_ref[...] = u


def _odd_decode(x, p, b0, b1, cw, w3, li):
    m, d = x.shape
    col = lambda c: pl.BlockSpec((m, d), lambda i, c=c: (0, c))
    full = lambda a: pl.BlockSpec(a.shape, lambda i: (0,) * a.ndim)
    return pl.pallas_call(
        _odd_decode_kernel,
        out_shape=(jax.ShapeDtypeStruct((m, d), F32), jax.ShapeDtypeStruct((m, d), F32)),
        grid=(1,),
        in_specs=[full(x), col(0), col(1), col(2), col(3), full(b0), full(b1), full(cw), _layer_spec(w3, li)],
        out_specs=(pl.BlockSpec((m, d), lambda i: (0, 0)), pl.BlockSpec((m, d), lambda i: (0, 0))),
        compiler_params=_cparams("arbitrary"),
        name="odd_decode",
    )(x, p, p, p, p, b0, b1, cw, w3)


def _tile(n, want):
    t = math.gcd(n, want)
    assert t == n or t % 8 == 0, (n, want)
    return t


def _pad_lanes(v):
    return jnp.pad(v.astype(F32), (0, LANE - v.shape[0])).reshape(1, LANE)


def kernel(x_prompt, x_sample, cache_k, cache_v, page_table, state_gdn, state_gdn_conv, state_shortconv, norm_w, rel_table, w_in_even, w_out_even, qn_w, kn_w, lam_q1, lam_k1, lam_q2, lam_k2, subln_w, gdn_conv_w, gdn_a_log, gdn_dt_bias, gdn_norm_w, w_in_odd, sc_conv_w, w_out_odd):
    nbp, t, d = x_prompt.shape
    nbs = x_sample.shape[0]
    page = cache_k.shape[2]
    assert x_sample.shape[1] == 1 and page >= MAX_DISTANCE and t % 8 == 0 and DEPTH % 2 == 0
    rows = _tile(t, SEQ_ROWS)
    tq = rows
    assert tq >= MAX_DISTANCE

    xp = x_prompt.reshape(nbp * t, d)
    xs = x_sample.reshape(nbs, d)

    g64 = jnp.arange(W_A) // D_HA
    bd = jnp.where(g64[:, None] == g64[None, :], 1.0 / D_HA, 0.0).astype(BF16)
    bias_p = _bias_tiles(rel_table.astype(F32), tq)
    bias_s = _bias_decode(rel_table.astype(F32), page)

    n_even = (DEPTH + 1) // 2
    w_in_e, w_out_e = w_in_even.astype(BF16), w_out_even.astype(BF16)
    w_in_o, w_out_o = w_in_odd.astype(BF16), w_out_odd.astype(BF16)
    kv_p = kv_s = s_s = None
    sp, gcp, scp = [], [], []
    gcs, scs = [], []
    ei = oi = 0
    for li in range(DEPTH):
        nw = norm_w[li].reshape(1, d)
        if li % 2 == 0:
            lambda_init = 0.8 - 0.6 * math.exp(-0.3 * li)
            w_ab = jnp.pad(w_in_even[ei, :, P_MAIN:], ((0, 0), (0, LANE - 2 * H_B))).astype(BF16)
            qw = jnp.tile(qn_w[ei], W_A // D_HA).reshape(1, W_A)
            kw = jnp.tile(kn_w[ei], W_A // D_HA).reshape(1, W_A)
            sw = subln_w[ei].reshape(1, DA)
            lam = (jnp.exp(jnp.sum(lam_q1[ei] * lam_k1[ei]).astype(F32))
                   - jnp.exp(jnp.sum(lam_q2[ei] * lam_k2[ei]).astype(F32)) + lambda_init).reshape(1)
            cw = gdn_conv_w[ei]
            alog = _pad_lanes(gdn_a_log[ei])
            dtb = _pad_lanes(gdn_dt_bias[ei])
            gnw = gdn_norm_w[ei].reshape(1, DV_B)

            qb, kb, *kv_p, p, ab, tail = _even_in(xp, nw, w_in_e, w_ab, qw, kw, bd, rows, ei, n_even, kv_p,
                                                  conv_w=cw, seq_len=t)
            oa = _attn_prompt(lam, qb, kb, bias_p, p, sw, nbp, t, tq, 1.0 - lambda_init)
            ob, s_new = _gdn_prompt(p, ab, alog, dtb, gnw, nbp, t, rows)
            prev_p = (oa, ob, w_out_e, ei)
            sp.append(s_new)
            gcp.append(tail[:, 8 - (GDN_CONV - 1):, :])

            qb, kb, *kv_s, p, ab = _even_in(xs, nw, w_in_e, w_ab, qw, kw, bd, nbs, ei, n_even, kv_s)
            oa = _attn_decode(page_table, lam, qb, kb, cache_k, cache_v, ei, bias_s, p, sw, 1.0 - lambda_init)
            conv0_t = jnp.swapaxes(state_gdn_conv[ei], 0, 1)
            qn, kn, vv, eg, beta, qk, conv_new = _gdn_decode_prep(p, ab, conv0_t, cw, alog, dtb)
            bb = math.gcd(nbs, DECODE_STATE_SEQS)
            to_cols = lambda a: a.reshape(nbs // bb, bb, H_B, DK_B).transpose(0, 2, 3, 1)
            s_s, ob = _gdn_decode(state_gdn, ei, to_cols(qn), to_cols(kn), vv, eg, beta, qk, p, gnw, bb, s_s)
            xs = _out_proj(xs, oa, ob, w_out_e, ei, nbs)
            gcs.append(jnp.swapaxes(conv_new, 0, 1))
            ei += 1
        else:
            cw = sc_conv_w[oi]
            xp, tail = _odd_prompt(xp, *prev_p, nw, w_in_o, cw, w_out_o, oi, nbp, t, rows)
            scp.append(tail[:, 8 - (SC_WIDTH - 1):, :])

            p = _norm_proj(xs, nw, w_in_o, oi, nbs, DECODE_PROJ_COLS)
            buf0 = state_shortconv[oi]
            xs, u = _odd_decode(xs, p, buf0[:, 0, :], buf0[:, 1, :], cw, w_out_o, oi)
            scs.append(jnp.stack([buf0[:, 1, :], u], axis=1))
            oi += 1

    leaf_p = lambda a: a.reshape(n_even, nbp, t, H_A, DA)
    leaf_s = lambda a: a.reshape(n_even, nbs, 1, H_A, DA)
    return (xp.reshape(nbp, t, d), xs.reshape(nbs, 1, d),
            leaf_p(kv_p[0]), leaf_p(kv_p[1]), jnp.stack(sp), jnp.stack(gcp), jnp.stack(scp),
            leaf_s(kv_s[0]), leaf_s(kv_s[1]), s_s, jnp.stack(gcs), jnp.stack(scs))
```

```python
import functools
import math

import jax
import jax.numpy as jnp
from jax import lax
from jax.experimental import pallas as pl
from jax.experimental.pallas import tpu as pltpu

F32, BF16 = jnp.float32, jnp.bfloat16

DEPTH = 4
H_A, D_HA = 4, 64
DA = 2 * D_HA
W_A = H_A * DA
H_B, DK_B, DV_B = 4, 128, 128
W_B = H_B * DV_B
QKV_B = 2 * H_B * DK_B + H_B * DV_B
GDN_CONV, GDN_CHUNK, SC_WIDTH = 4, 64, 3
NUM_BUCKETS, MAX_EXACT, MAX_DISTANCE = 32, 16, 128
EPS, NEG = 1e-6, -1e30
LOG2E = math.log2(math.e)
LANE = 128
P_MAIN = 4 * W_A + QKV_B + W_B
VMEM_LIMIT = 56 * 1024 * 1024
LOCAL_GROUP = 8
SEQ_ROWS = 512
GDN_SEQS = 4
DECODE_STATE_SEQS = 8
DECODE_PROJ_COLS = 1024
DECODE_ATTN_SEQS = 2


def _layer_spec(w3, li, cols=None):
    return pl.BlockSpec((None, w3.shape[1], cols or w3.shape[2]), lambda *_: (li, 0, 0))


def _cparams(*sem):
    return pltpu.CompilerParams(dimension_semantics=sem, vmem_limit_bytes=VMEM_LIMIT)


def _silu(z):
    h = 0.5 * z
    return h + h * jnp.tanh(h)


def _sigmoid(z):
    return 1.0 / (1.0 + jnp.exp(-z))


def _dot(a, b):
    return jnp.dot(a.astype(BF16), b.astype(BF16), preferred_element_type=F32)


def _dot_nt(a, b):
    return lax.dot_general(a.astype(BF16), b.astype(BF16), (((1,), (1,)), ((), ())), preferred_element_type=F32)


def _rms_rows(x, w):
    return x * lax.rsqrt(jnp.mean(x * x, axis=-1, keepdims=True) + EPS) * w


def _norm_proj_kernel(x_ref, nw_ref, w_ref, o_ref, h_ref):
    @pl.when(pl.program_id(1) == 0)
    def _():
        h_ref[...] = _rms_rows(x_ref[...], nw_ref[...]).astype(BF16)

    o_ref[...] = jnp.dot(h_ref[...], w_ref[...], preferred_element_type=F32).astype(BF16)


def _norm_proj(x, nw, w3, li, tm, tn):
    m, d = x.shape
    n = w3.shape[2]
    return pl.pallas_call(
        _norm_proj_kernel,
        out_shape=jax.ShapeDtypeStruct((m, n), BF16),
        grid=(m // tm, n // tn),
        in_specs=[pl.BlockSpec((tm, d), lambda i, j: (i, 0)),
                  pl.BlockSpec((1, d), lambda i, j: (0, 0)),
                  pl.BlockSpec((None, d, tn), lambda i, j: (li, 0, j))],
        out_specs=pl.BlockSpec((tm, tn), lambda i, j: (i, j)),
        scratch_shapes=[pltpu.VMEM((tm, d), BF16)],
        compiler_params=_cparams("arbitrary", "arbitrary"),
        name="norm_proj",
    )(x, nw, w3)


def _even_in_kernel(x_ref, nw_ref, w_ref, wab_ref, qw_ref, kw_ref, bd_ref, *refs, n_alias, tiles_per_seq):
    if tiles_per_seq:
        cw_ref = refs[0]
        qo_ref, kb_ref, kf_ref, vf_ref, p_ref, ab_ref, tail_ref, carry_ref = refs[1 + n_alias:]
    else:
        qo_ref, kb_ref, kf_ref, vf_ref, p_ref, ab_ref = refs[n_alias:]
    tm = x_ref.shape[0]
    h = _rms_rows(x_ref[...], nw_ref[...]).astype(BF16)
    proj = lambda c: jnp.dot(h, w_ref[:, c * W_A:(c + 1) * W_A], preferred_element_type=F32)
    ab_ref[...] = jnp.dot(h, wab_ref[...], preferred_element_type=F32)
    bd = bd_ref[...]

    def group_norm(x, w):
        ms = jnp.dot((x * x).astype(BF16), bd, preferred_element_type=F32)
        return x * lax.rsqrt(ms + EPS) * w

    def leaf(ref, val):
        for hd in range(H_A):
            ref[pl.ds(hd, tm, stride=H_A), :] = val[:, hd * DA:(hd + 1) * DA]

    qn = group_norm(proj(0), qw_ref[...])
    qo_ref[...] = (qn * (D_HA ** -0.5 * LOG2E)).astype(BF16)
    kn = group_norm(proj(1), kw_ref[...])
    kb_ref[...] = kn.astype(BF16)
    leaf(kf_ref, kn)
    v = proj(2)
    leaf(vf_ref, v)
    p_ref[:, 0:W_A] = v.astype(BF16)
    if tiles_per_seq:
        first = pl.program_id(0) % tiles_per_seq == 0

        @pl.when(first)
        def _():
            carry_ref[...] = jnp.zeros(carry_ref.shape, F32)

    c_qkv = (4 * W_A) // W_A
    for c in range(3, P_MAIN // W_A):
        y = proj(c)
        seg = c - c_qkv
        if tiles_per_seq and 0 <= seg < QKV_B // W_B:
            cols = slice(seg * W_B, (seg + 1) * W_B)
            raw = y
            y = _silu(_shifted_conv(raw, carry_ref[:, cols], cw_ref[:, cols]))
            carry_ref[:, cols] = raw[tm - 8:tm]
            tail_ref[:, cols] = raw[tm - 8:tm]
        p_ref[:, (c - 2) * W_A:(c - 1) * W_A] = y.astype(BF16)


def _even_in(x, nw, w3, w_ab, qw, kw, bd, tm, ei, n_layers, kv_prev, conv_w=None, seq_len=None):
    m, d = x.shape
    tiles_per_seq = 0 if conv_w is None else seq_len // tm
    n_rest = P_MAIN - 2 * W_A
    full = lambda a: pl.BlockSpec(a.shape, lambda i: (0,) * a.ndim)
    row = lambda n: pl.BlockSpec((tm, n), lambda i: (i, 0))
    leaf = pl.BlockSpec((None, tm * H_A, DA), lambda i: (ei, i, 0))
    leaf_shape = jax.ShapeDtypeStruct((n_layers, m * H_A, DA), F32)
    n_alias = 0 if kv_prev is None else 2
    out_shape = [jax.ShapeDtypeStruct((m, W_A), BF16), jax.ShapeDtypeStruct((m, W_A), BF16), leaf_shape, leaf_shape,
                 jax.ShapeDtypeStruct((m, n_rest), BF16), jax.ShapeDtypeStruct((m, LANE), F32)]
    out_specs = [row(W_A), row(W_A), leaf, leaf, row(n_rest), row(LANE)]
    in_specs = [row(d), full(nw), _layer_spec(w3, ei, P_MAIN), full(w_ab), full(qw), full(kw), full(bd)]
    args = [x, nw, w3, w_ab, qw, kw, bd]
    scratch = []
    if tiles_per_seq:
        in_specs.append(full(conv_w))
        args.append(conv_w)
        out_shape.append(jax.ShapeDtypeStruct((m // seq_len, 8, QKV_B), F32))
        out_specs.append(pl.BlockSpec((None, 8, QKV_B), lambda i: (i // tiles_per_seq, 0, 0)))
        scratch.append(pltpu.VMEM((8, QKV_B), F32))
    n_in = len(args)
    return pl.pallas_call(
        functools.partial(_even_in_kernel, n_alias=n_alias, tiles_per_seq=tiles_per_seq),
        out_shape=tuple(out_shape),
        grid=(m // tm,),
        in_specs=in_specs + [pl.BlockSpec(memory_space=pl.ANY)] * n_alias,
        out_specs=tuple(out_specs),
        input_output_aliases={} if kv_prev is None else {n_in: 2, n_in + 1: 3},
        scratch_shapes=scratch,
        compiler_params=_cparams("arbitrary"),
        name="even_in",
    )(*args, *(kv_prev or ()))


def _t5_bias(n, tab_ref, h):
    nf = jnp.maximum(n, 1).astype(F32)
    large = MAX_EXACT + (jnp.log(nf / MAX_EXACT) / math.log(MAX_DISTANCE / MAX_EXACT)
                         * (NUM_BUCKETS - MAX_EXACT)).astype(jnp.int32)
    large = jnp.minimum(large, NUM_BUCKETS - 1)
    bkt = jnp.where(n < MAX_EXACT, n, large)
    out = jnp.zeros(n.shape, F32)
    for b in range(NUM_BUCKETS):
        out = jnp.where(bkt == b, tab_ref[b, h], out)
    return (out - tab_ref[NUM_BUCKETS - 1, h]) * LOG2E


def _bias_tiles_kernel(tab_ref, o_ref, *, tq):
    h = pl.program_id(0)
    i = lax.broadcasted_iota(jnp.int32, (tq, tq), 0)
    j = lax.broadcasted_iota(jnp.int32, (tq, tq), 1)
    n0 = i - j
    o_ref[0, 0] = jnp.where(n0 >= 0, _t5_bias(jnp.maximum(n0, 0), tab_ref, h), NEG)
    o_ref[0, 1] = _t5_bias(n0 + tq, tab_ref, h)


def _bias_tiles(rel_table, tq):
    return pl.pallas_call(
        functools.partial(_bias_tiles_kernel, tq=tq),
        out_shape=jax.ShapeDtypeStruct((H_A, 2, tq, tq), F32),
        grid=(H_A,),
        in_specs=[pl.BlockSpec(memory_space=pltpu.SMEM)],
        out_specs=pl.BlockSpec((1, 2, tq, tq), lambda h: (h, 0, 0, 0)),
        compiler_params=_cparams("arbitrary"),
        name="bias_tiles",
    )(rel_table)


def _bias_decode_kernel(tab_ref, o_ref, *, page):
    w = page * H_A
    row = lax.broadcasted_iota(jnp.int32, (2 * H_A, w), 0)
    lane = lax.broadcasted_iota(jnp.int32, (2 * H_A, w), 1)
    valid = (lane & (H_A - 1)) == (row >> 1)
    n = page - (lane >> 2)
    near = jnp.zeros((2 * H_A, w), F32)
    new = jnp.zeros((2 * H_A, LANE), F32)
    for h in range(H_A):
        near = jnp.where((row >> 1) == h, _t5_bias(n, tab_ref, h), near)
        new = jnp.where((row[:, :LANE] >> 1) == h, _t5_bias(jnp.zeros((2 * H_A, LANE), jnp.int32), tab_ref, h), new)
    o_ref[:, 0:w] = jnp.where(valid, 0.0, NEG)
    o_ref[:, w:2 * w] = jnp.where(valid, near, NEG)
    o_ref[:, 2 * w:2 * w + LANE] = new


def _bias_decode(rel_table, page):
    assert H_A == 4
    return pl.pallas_call(
        functools.partial(_bias_decode_kernel, page=page),
        out_shape=jax.ShapeDtypeStruct((2 * H_A, 2 * page * H_A + LANE), F32),
        in_specs=[pl.BlockSpec(memory_space=pltpu.SMEM)],
        out_specs=pl.BlockSpec(memory_space=pltpu.VMEM),
        name="bias_decode",
    )(rel_table)


def _attn_finish(o, sw, z, out_scale):
    ms = jnp.mean(o * o, axis=-1, keepdims=True)
    return o * lax.rsqrt(ms + EPS) * sw * out_scale * _silu(z.astype(F32))


def _attn_prompt_kernel(lam_ref, q_ref, k_ref, v_ref, bias_ref, za_ref, sw_ref, o_ref, m, l, a, *, tq, out_scale):
    reps = tq // LANE
    lane = lax.broadcasted_iota(jnp.int32, (tq, DA), 1)

    def q_block(qi, carry):
        qrows = pl.ds(pl.multiple_of(qi * tq, tq), tq)
        q = q_ref[qrows, :].astype(F32)
        q2 = jnp.concatenate([jnp.where(lane < D_HA, q, 0.0), jnp.where(lane >= D_HA, q, 0.0)], axis=0).astype(BF16)
        m[...] = jnp.full(m.shape, NEG, F32)
        l[...] = jnp.zeros(l.shape, F32)
        a[...] = jnp.zeros(a.shape, F32)

        def step(j, bias):
            rows = pl.ds(pl.multiple_of(j * tq, tq), tq)
            s = lax.dot_general(q2, k_ref[rows, :], (((1,), (1,)), ((), ())), preferred_element_type=F32)
            if bias is not None:
                s = s + jnp.concatenate([bias, bias], axis=0)
            m_prev = m[...]
            m_new = jnp.maximum(m_prev, jnp.max(s, axis=1, keepdims=True))
            p = jnp.exp2(s - jnp.concatenate([m_new] * reps, axis=1))
            alpha = jnp.exp2(m_prev - m_new)
            l[...] = alpha * l[...] + jnp.sum(p, axis=1, keepdims=True)
            a[...] = alpha * a[...] + jnp.dot(p.astype(BF16), v_ref[rows, :], preferred_element_type=F32)
            m[...] = m_new

        def far(j, c):
            step(j, None)
            return c

        lax.fori_loop(0, jnp.maximum(qi - 1, 0), far, 0)

        @pl.when(qi >= 1)
        def _():
            step(qi - 1, bias_ref[0, 1])

        step(qi, bias_ref[0, 0])
        o = a[...] / l[...]
        o = o[:tq] - lam_ref[0] * o[tq:]
        o_ref[qrows, :] = _attn_finish(o, sw_ref[...], za_ref[qrows, :], out_scale).astype(BF16)
        return carry

    lax.fori_loop(0, q_ref.shape[0] // tq, q_block, 0)


def _attn_prompt(lam, q, k, bias, p, sw, nb, t, tq, out_scale):
    m = q.shape[0]
    za0 = W_A // DA
    v0 = 0
    seq = lambda c0: pl.BlockSpec((t, DA), lambda b, h, c0=c0: (b, c0 + h))
    return pl.pallas_call(
        functools.partial(_attn_prompt_kernel, tq=tq, out_scale=out_scale),
        out_shape=jax.ShapeDtypeStruct((m, W_A), BF16),
        grid=(nb, H_A),
        in_specs=[pl.BlockSpec(memory_space=pltpu.SMEM),
                  seq(0), seq(0), seq(v0),
                  pl.BlockSpec((1, 2, tq, tq), lambda b, h: (h, 0, 0, 0)),
                  seq(za0),
                  pl.BlockSpec((1, DA), lambda b, h: (0, 0))],
        out_specs=seq(0),
        scratch_shapes=[pltpu.VMEM((2 * tq, LANE), F32), pltpu.VMEM((2 * tq, LANE), F32),
                        pltpu.VMEM((2 * tq, DA), F32)],
        compiler_params=_cparams("arbitrary", "arbitrary"),
        name="attn_prompt",
    )(lam, q, k, p, bias, p, sw)


def _attn_decode_kernel(pt_ref, lam_ref, q_ref, kn_ref, vn_ref, *refs, page, npg, nseq, out_scale):
    ck_refs, cv_refs = refs[:nseq * npg], refs[nseq * npg:2 * nseq * npg]
    bias_ref, za_ref, sw_ref, o_ref = refs[2 * nseq * npg:]
    rows, w = 2 * H_A, page * H_A
    row = lax.broadcasted_iota(jnp.int32, (rows, DA), 0)
    lane = lax.broadcasted_iota(jnp.int32, (rows, DA), 1)

    def head_rows(x):
        out = jnp.zeros((rows, DA), F32)
        for h in range(H_A):
            out = jnp.where((row >> 1) == h, jnp.broadcast_to(x[:, h * DA:(h + 1) * DA], (rows, DA)), out)
        return out

    sw = sw_ref[...]
    for sq in range(nseq):
        q8 = jnp.where((lane >> 6) == (row & 1), head_rows(q_ref[sq].astype(F32)), 0.0)
        q8b = q8.astype(BF16)
        s_all = []
        for j in range(npg):
            s = lax.dot_general(q8b, ck_refs[sq * npg + j][...].astype(BF16), (((1,), (1,)), ((), ())),
                                preferred_element_type=F32)
            s_all.append(s + (bias_ref[:, w:2 * w] if j == npg - 1 else bias_ref[:, 0:w]))
        s_new = (jnp.sum(q8 * head_rows(kn_ref[sq].astype(F32)), axis=1, keepdims=True)
                 + bias_ref[:, 2 * w:2 * w + 1])
        m = s_all[0]
        for s in s_all[1:]:
            m = jnp.maximum(m, s)
        m = jnp.maximum(jnp.max(m, axis=1, keepdims=True), s_new)
        p_new = jnp.exp2(s_new - m)
        l = p_new
        acc = p_new * head_rows(vn_ref[sq].astype(F32))
        for j in range(npg):
            p = jnp.exp2(s_all[j] - m)
            l = l + jnp.sum(p, axis=1, keepdims=True)
            acc = acc + jnp.dot(p.astype(BF16), cv_refs[sq * npg + j][...].astype(BF16), preferred_element_type=F32)
        o = acc / l
        z = za_ref[sq]
        outs = []
        for h in range(H_A):
            oh = o[2 * h:2 * h + 1, :] - lam_ref[0] * o[2 * h + 1:2 * h + 2, :]
            outs.append(_attn_finish(oh, sw, z[:, h * DA:(h + 1) * DA], out_scale))
        o_ref[sq] = jnp.concatenate(outs, axis=1).astype(BF16)


def _attn_decode(page_table, lam, q, kn, cache_k, cache_v, ei, bias, p, sw, out_scale):
    nb, npg = page_table.shape
    page = cache_k.shape[2]
    nseq = math.gcd(nb, DECODE_ATTN_SEQS)
    ck = cache_k.reshape(cache_k.shape[0], cache_k.shape[1], page * H_A, DA)
    cv = cache_v.reshape(cache_v.shape[0], cache_v.shape[1], page * H_A, DA)
    r3 = lambda a: a.reshape(nb, 1, a.shape[-1])
    vec = lambda c: pl.BlockSpec((nseq, 1, W_A), lambda b, pt, c=c: (b, 0, c))
    cache = [pl.BlockSpec((None, None, page * H_A, DA),
                          lambda b, pt, sq=sq, j=j: (ei, pt[(b * nseq + sq) * npg + j], 0, 0))
             for sq in range(nseq) for j in range(npg)]
    out = pl.pallas_call(
        functools.partial(_attn_decode_kernel, page=page, npg=npg, nseq=nseq, out_scale=out_scale),
        out_shape=jax.ShapeDtypeStruct((nb, 1, W_A), BF16),
        grid_spec=pltpu.PrefetchScalarGridSpec(
            num_scalar_prefetch=1,
            grid=(nb // nseq,),
            in_specs=[pl.BlockSpec(memory_space=pltpu.SMEM), vec(0), vec(0), vec(0)] + cache + cache
                     + [pl.BlockSpec(bias.shape, lambda b, pt: (0, 0)),
                        vec(1),
                        pl.BlockSpec((1, DA), lambda b, pt: (0, 0))],
            out_specs=pl.BlockSpec((nseq, 1, W_A), lambda b, pt: (b, 0, 0))),
        compiler_params=_cparams("arbitrary"),
        name="attn_decode",
    )(page_table.reshape(-1), lam, r3(q), r3(kn), r3(p), *([ck] * (nseq * npg)), *([cv] * (nseq * npg)), bias, r3(p), sw)
    return out.reshape(nb, W_A)


def _shifted_conv(x, prev, w):
    r, c = x.shape
    taps = w.shape[0]
    x3 = x.reshape(r // 8, 8, c)
    sub = lax.broadcasted_iota(jnp.int32, x3.shape, 1)
    y = x3 * w[taps - 1:taps, :].reshape(1, 1, c)
    for s in range(1, taps):
        rot = pltpu.roll(x3, s, 1)
        before = jnp.concatenate([pltpu.roll(prev, s, 0)[None], rot[:-1]], axis=0)
        y = y + jnp.where(sub < s, before, rot) * w[taps - 1 - s:taps - s, :].reshape(1, 1, c)
    return y.reshape(r, c)


def _gates(ab, alog, dtb):
    x = ab + dtb
    sp = jnp.maximum(x, 0.0) + jnp.log(1.0 + jnp.exp(-jnp.abs(x)))
    return -jnp.exp(alog) * sp, _sigmoid(ab)


def _l2norm(x):
    return x * lax.rsqrt(jnp.sum(x * x, axis=-1, keepdims=True) + EPS)


def _rms(x, w):
    return x * lax.rsqrt(jnp.mean(x * x, axis=-1, keepdims=True) + EPS) * w


def _level_masks(c):
    i = lax.broadcasted_iota(jnp.int32, (c, c), 0)
    j = lax.broadcasted_iota(jnp.int32, (c, c), 1)
    masks = []
    s = 1
    while s < c:
        sh = s.bit_length() - 1
        masks.append((((i >> (sh + 1)) == (j >> (sh + 1))) & ((i >> sh) != (j >> sh)) & (i > j)).astype(F32))
        s *= 2
    return masks


def _gdn_prompt_kernel(cq_ref, ck_ref, cv_ref, zb_ref, ab_ref, alog_ref, dtb_ref, gnw_ref,
                       ob_ref, s_ref, g_ref, beta_ref, u_ref, wq_ref, ak_ref, gl_ref, *, rows, chunk, nseq):
    t = pl.program_id(1)
    c = chunk

    @pl.when(t == 0)
    def _():
        s_ref[...] = jnp.zeros(s_ref.shape, F32)

    ii = lax.broadcasted_iota(jnp.int32, (c, c), 0)
    jj = lax.broadcasted_iota(jnp.int32, (c, c), 1)
    incl = ii >= jj
    ltri = incl.astype(BF16)
    masks = _level_masks(c)

    def prep(b, ci):
        rs = pl.ds(pl.multiple_of(ci * c, c), c)
        gch = g_ref[b, rs, :]
        g1 = gch.astype(BF16)
        r1 = gch - g1.astype(F32)
        g2 = r1.astype(BF16)
        g3 = (r1 - g2.astype(F32)).astype(BF16)
        d = functools.partial(jnp.dot, preferred_element_type=F32)
        gc = d(ltri, g1) + d(ltri, g2) + d(ltri, g3)
        gct = jnp.concatenate([gc, gc], axis=0).T
        ge = jnp.exp(gc)
        kdec = jnp.exp(gc[c - 1:c, :] - gc)
        gl_ref[b, ci] = ge[c - 1:c, :]
        return rs, gc, gct, ge, kdec, beta_ref[b, rs, :]

    def local(b, ig):
        items = []
        for ci in [LOCAL_GROUP * ig + k for k in range(LOCAL_GROUP)]:
            rs, gc, gct, ge, kdec, bch = prep(b, ci)
            for h in range(H_B):
                hc = slice(h * DK_B, (h + 1) * DK_B)
                qh = _l2norm(cq_ref[b, rs, hc].astype(F32)) * (DK_B ** -0.5)
                kh = _l2norm(ck_ref[b, rs, hc].astype(F32))
                vh = cv_ref[b, rs, hc].astype(F32)
                bcol = bch[:, H_B + h:H_B + h + 1]
                gecol = ge[:, h:h + 1]
                decay = jnp.exp(jnp.where(incl, gc[:, h:h + 1] - gct[h:h + 1, :c], NEG))
                kbeta = kh * bcol
                wq_ref[b, ci, h, c:2 * c] = (qh * gecol).astype(BF16)
                ak_ref[b, ci, h, c:c + DK_B] = (kh * kdec[:, h:h + 1]).T.astype(BF16)
                items.append(dict(ci=ci, h=h, rs=rs, hc=hc, decay=decay, qk=(kbeta, qh, kh),
                                  rhs=jnp.concatenate([vh * bcol, kbeta * gecol], axis=1)))
        for it in items:
            kbeta, qh, kh = it.pop("qk")
            both = _dot_nt(jnp.concatenate([kbeta, qh], axis=0), kh)
            it["mm"] = jnp.where(ii > jj, both[:c] * it["decay"], 0.0)
            ak_ref[b, it["ci"], it["h"], 0:c] = (both[c:] * it["decay"]).astype(BF16)
        for it in items:
            it["pp"] = -(it["mm"] * masks[0])
        for mk in masks[1:]:
            for it in items:
                e = it["mm"] * mk
                it["x"] = e + _dot(it["pp"], e)
            for it in items:
                it["pp"] = it["pp"] - (it["x"] + _dot(it["x"], it["pp"]))
        for it in items:
            uw = it["rhs"] + _dot(it["pp"], it["rhs"])
            u_ref[b, it["rs"], it["hc"]] = uw[:, :DV_B]
            wq_ref[b, it["ci"], it["h"], 0:c] = uw[:, DV_B:].astype(BF16)

    for b in range(nseq):
        g, beta = _gates(ab_ref[b], alog_ref[...], dtb_ref[...])
        g_ref[b] = g
        beta_ref[b] = beta

        def local_b(ig, carry, b=b):
            local(b, ig)
            return carry

        lax.fori_loop(0, rows // (LOCAL_GROUP * c), local_b, 0)
    gnw = gnw_ref[...]

    def scan(ci, carry):
        rs = pl.ds(pl.multiple_of(ci * c, c), c)
        chains = [(b, h) for b in range(nseq) for h in range(H_B)]
        hc = lambda h: slice(h * DK_B, (h + 1) * DK_B)
        s_old = [s_ref[b, h] for b, h in chains]
        r = [jnp.dot(wq_ref[b, ci, h], s_old[k].astype(BF16), preferred_element_type=F32)
             for k, (b, h) in enumerate(chains)]
        v_new = [u_ref[b, rs, hc(h)] - r[k][:c] for k, (b, h) in enumerate(chains)]
        r2 = [jnp.dot(ak_ref[b, ci, h], v_new[k].astype(BF16), preferred_element_type=F32)
              for k, (b, h) in enumerate(chains)]
        for k, (b, h) in enumerate(chains):
            s_ref[b, h] = s_old[k] * gl_ref[b, ci][:, h:h + 1] + r2[k][c:]
            ob_ref[b, rs, hc(h)] = (_rms(r[k][c:] + r2[k][:c], gnw)
                                    * _silu(zb_ref[b, rs, hc(h)].astype(F32))).astype(BF16)
        return carry

    lax.fori_loop(0, rows // c, scan, 0)


def _gdn_prompt(p, ab, alog, dtb, gnw, nb, t, rows):
    m = p.shape[0]
    nt = t // rows
    c = math.gcd(GDN_CHUNK, t)
    nseq = math.gcd(nb, GDN_SEQS)
    p3 = p.reshape(nb, t, p.shape[1])
    col = lambda cidx: pl.BlockSpec((nseq, rows, W_B), lambda b, i, cidx=cidx: (b, i, cidx))
    full = lambda a: pl.BlockSpec(a.shape, lambda b, i: (0,) * a.ndim)
    c0 = (2 * W_A) // W_B
    ob, s_new = pl.pallas_call(
        functools.partial(_gdn_prompt_kernel, rows=rows, chunk=c, nseq=nseq),
        out_shape=(jax.ShapeDtypeStruct((nb, t, W_B), BF16), jax.ShapeDtypeStruct((nb, H_B, DK_B, DV_B), F32)),
        grid=(nb // nseq, nt),
        in_specs=[col(c0), col(c0 + 1), col(c0 + 2), col(c0 + 3),
                  pl.BlockSpec((nseq, rows, LANE), lambda b, i: (b, i, 0)),
                  full(alog), full(dtb), full(gnw)],
        out_specs=(pl.BlockSpec((nseq, rows, W_B), lambda b, i: (b, i, 0)),
                   pl.BlockSpec((nseq, H_B, DK_B, DV_B), lambda b, i: (b, 0, 0, 0))),
        scratch_shapes=[pltpu.VMEM((nseq, rows, LANE), F32), pltpu.VMEM((nseq, rows, LANE), F32),
                        pltpu.VMEM((nseq, rows, W_B), F32),
                        pltpu.VMEM((nseq, rows // c, H_B, 2 * c, DK_B), BF16),
                        pltpu.VMEM((nseq, rows // c, H_B, c + DK_B, c), BF16),
                        pltpu.VMEM((nseq, rows // c, 1, LANE), F32)],
        compiler_params=_cparams("arbitrary", "arbitrary"),
        name="gdn_prompt",
    )(p3, p3, p3, p3, ab.reshape(nb, t, LANE), alog, dtb, gnw)
    return ob.reshape(m, W_B), s_new


def _gdn_decode_prep_kernel(pq_ref, pk_ref, pv_ref, ab_ref, c0_ref, cw_ref, alog_ref, dtb_ref,
                            q_ref, k_ref, v_ref, eg_ref, beta_ref, qk_ref, cn_ref):
    cw = cw_ref[...]
    taps = cw.shape[0]
    outs = (q_ref, k_ref, v_ref)
    for seg, ref in enumerate((pq_ref, pk_ref, pv_ref)):
        cols = slice(seg * W_B, (seg + 1) * W_B)
        x = ref[...].astype(F32)
        y = x * cw[taps - 1:taps, cols]
        for j in range(taps - 1):
            y = y + c0_ref[j, :, cols] * cw[j:j + 1, cols]
            if j >= 1:
                cn_ref[j - 1, :, cols] = c0_ref[j, :, cols]
        cn_ref[taps - 2, :, cols] = x
        outs[seg][...] = _silu(y)
    lane = lax.broadcasted_iota(jnp.int32, eg_ref.shape, 1)
    qk = jnp.zeros(eg_ref.shape, F32)
    for h in range(H_B):
        hc = slice(h * DK_B, (h + 1) * DK_B)
        qh = _l2norm(q_ref[:, hc]) * (DK_B ** -0.5)
        kh = _l2norm(k_ref[:, hc])
        q_ref[:, hc] = qh
        k_ref[:, hc] = kh
        qk = jnp.where(lane == h, jnp.sum(qh * kh, axis=-1, keepdims=True), qk)
    g, beta = _gates(ab_ref[...], alog_ref[...], dtb_ref[...])
    eg_ref[...] = jnp.exp(g)
    beta_ref[...] = beta
    qk_ref[...] = qk


def _gdn_decode_prep(p, ab, conv0_t, cw, alog, dtb):
    nb = p.shape[0]
    c0 = (2 * W_A) // W_B
    col = lambda cidx: pl.BlockSpec((nb, W_B), lambda i, cidx=cidx: (0, cidx))
    full = lambda a: pl.BlockSpec(a.shape, lambda i: (0,) * a.ndim)
    wide = jax.ShapeDtypeStruct((nb, W_B), F32)
    narrow = jax.ShapeDtypeStruct((nb, LANE), F32)
    ospec = lambda s: pl.BlockSpec(s.shape, lambda i: (0,) * len(s.shape))
    outs = (wide, wide, wide, narrow, narrow, narrow, jax.ShapeDtypeStruct(conv0_t.shape, F32))
    return pl.pallas_call(
        _gdn_decode_prep_kernel,
        out_shape=outs,
        grid=(1,),
        in_specs=[col(c0), col(c0 + 1), col(c0 + 2),
                  full(ab), full(conv0_t), full(cw), full(alog), full(dtb)],
        out_specs=tuple(ospec(s) for s in outs),
        compiler_params=_cparams("arbitrary"),
        name="gdn_decode_prep",
    )(p, p, p, ab, conv0_t, cw, alog, dtb)


def _gdn_decode_kernel(s_ref, qt_ref, kt_ref, v_ref, eg_ref, beta_ref, qk_ref, zb_ref, gnw_ref, *refs, bb, n_alias):
    so_ref, ob_ref, o_scr = refs[n_alias:]
    v = v_ref[...]
    eg = eg_ref[...]
    beta = beta_ref[...]
    qk = qk_ref[...]
    for h in range(H_B):
        hc = slice(h * DV_B, (h + 1) * DV_B)
        qt = qt_ref[h]
        kt = kt_ref[h]
        for i in range(bb):
            s_old = s_ref[i, h]
            kc = kt[:, i:i + 1]
            qc = qt[:, i:i + 1]
            egs = eg[i:i + 1, h:h + 1]
            ks = jnp.sum(s_old * kc, axis=0, keepdims=True)
            qs = jnp.sum(s_old * qc, axis=0, keepdims=True)
            v_new = beta[i:i + 1, H_B + h:H_B + h + 1] * (v[i:i + 1, hc] - egs * ks)
            o_scr[i:i + 1, hc] = egs * qs + qk[i:i + 1, h:h + 1] * v_new
            so_ref[i, h] = s_old * egs + kc * v_new
    gnw = gnw_ref[...]
    z = zb_ref[...]
    outs = [_rms(o_scr[:, h * DV_B:(h + 1) * DV_B], gnw) * _silu(z[:, h * DV_B:(h + 1) * DV_B].astype(F32))
            for h in range(H_B)]
    ob_ref[...] = jnp.concatenate(outs, axis=1).astype(BF16)


def _gdn_decode(state, ei, qt, kt, v, eg, beta, qk, p, gnw, bb, s_prev):
    nb = v.shape[0]
    ns = nb // bb
    narrow = pl.BlockSpec((bb, LANE), lambda i: (i, 0))
    tr = pl.BlockSpec((None, H_B, DK_B, bb), lambda i: (i, 0, 0, 0))
    zb0 = (2 * W_A + QKV_B) // W_B
    n_alias = 0 if s_prev is None else 1
    return pl.pallas_call(
        functools.partial(_gdn_decode_kernel, bb=bb, n_alias=n_alias),
        out_shape=(jax.ShapeDtypeStruct(state.shape, F32), jax.ShapeDtypeStruct((nb, W_B), BF16)),
        grid=(ns,),
        in_specs=[pl.BlockSpec((None, bb, H_B, DK_B, DV_B), lambda i: (ei, i, 0, 0, 0)),
                  tr, tr,
                  pl.BlockSpec((bb, W_B), lambda i: (i, 0)),
                  narrow, narrow, narrow,
                  pl.BlockSpec((bb, W_B), lambda i: (i, zb0)),
                  pl.BlockSpec((1, DV_B), lambda i: (0, 0))] + [pl.BlockSpec(memory_space=pl.ANY)] * n_alias,
        out_specs=(pl.BlockSpec((None, bb, H_B, DK_B, DV_B), lambda i: (ei, i, 0, 0, 0)),
                   pl.BlockSpec((bb, W_B), lambda i: (i, 0))),
        input_output_aliases={9: 0} if n_alias else {},
        scratch_shapes=[pltpu.VMEM((bb, W_B), F32)],
        compiler_params=_cparams("arbitrary"),
        name="gdn_decode",
    )(state, qt, kt, v, eg, beta, qk, p, gnw, *(() if s_prev is None else (s_prev,)))


def _out_proj_kernel(x_ref, oa_ref, ob_ref, w_ref, y_ref):
    y_ref[...] = (x_ref[...]
                  + jnp.dot(oa_ref[...], w_ref[0:W_A, :], preferred_element_type=F32)
                  + jnp.dot(ob_ref[...], w_ref[W_A:W_A + W_B, :], preferred_element_type=F32))


def _out_proj(x, oa, ob, w3, li, tm):
    m, d = x.shape
    return pl.pallas_call(
        _out_proj_kernel,
        out_shape=jax.ShapeDtypeStruct((m, d), F32),
        grid=(m // tm,),
        in_specs=[pl.BlockSpec((tm, d), lambda i: (i, 0)),
                  pl.BlockSpec((tm, W_A), lambda i: (i, 0)),
                  pl.BlockSpec((tm, W_B), lambda i: (i, 0)),
                  _layer_spec(w3, li)],
        out_specs=pl.BlockSpec((tm, d), lambda i: (i, 0)),
        compiler_params=_cparams("arbitrary"),
        name="out_proj",
    )(x, oa, ob, w3)


def _odd_prompt_kernel(x_ref, oa_ref, ob_ref, wp_ref, nw_ref, wi_ref, cw_ref, wo_ref, y_ref, sc_ref, carry_ref, *, rows):
    t = pl.program_id(1)
    d = x_ref.shape[1]

    @pl.when(t == 0)
    def _():
        carry_ref[...] = jnp.zeros(carry_ref.shape, F32)

    x = (x_ref[...]
         + jnp.dot(oa_ref[...], wp_ref[0:W_A, :], preferred_element_type=F32)
         + jnp.dot(ob_ref[...], wp_ref[W_A:W_A + W_B, :], preferred_element_type=F32))
    h = _rms_rows(x, nw_ref[...]).astype(BF16)
    proj = lambda c: jnp.dot(h, wi_ref[:, c * d:(c + 1) * d], preferred_element_type=F32)
    u = proj(1) * proj(2)
    cv = _shifted_conv(u, carry_ref[...], cw_ref[...])
    carry_ref[...] = u[rows - 8:rows]
    g = proj(0) * cv * _silu(proj(3))
    y_ref[...] = x + jnp.dot(g.astype(BF16), wo_ref[...], preferred_element_type=F32)

    @pl.when(t == pl.num_programs(1) - 1)
    def _():
        sc_ref[...] = u[rows - 8:rows]


def _odd_prompt(x, oa, ob, w_prev3, ei, nw, w_in3, cw, w_out3, oi, nb, t, rows):
    m, d = x.shape
    nt = t // rows
    full = lambda a: pl.BlockSpec(a.shape, lambda b, i: (0,) * a.ndim)
    row = lambda n: pl.BlockSpec((rows, n), lambda b, i: (b * nt + i, 0))
    return pl.pallas_call(
        functools.partial(_odd_prompt_kernel, rows=rows),
        out_shape=(jax.ShapeDtypeStruct((m, d), F32), jax.ShapeDtypeStruct((nb, 8, d), F32)),
        grid=(nb, nt),
        in_specs=[row(d), row(W_A), row(W_B), _layer_spec(w_prev3, ei), full(nw), _layer_spec(w_in3, oi), full(cw),
                  _layer_spec(w_out3, oi)],
        out_specs=(row(d), pl.BlockSpec((None, 8, d), lambda b, i: (b, 0, 0))),
        scratch_shapes=[pltpu.VMEM((8, d), F32)],
        compiler_params=_cparams("arbitrary", "arbitrary"),
        name="odd_prompt",
    )(x, oa, ob, w_prev3, nw, w_in3, cw, w_out3)


def _odd_decode_kernel(x_ref, bg_ref, cg_ref, hh_ref, z_ref, b0_ref, b1_ref, cw_ref, w_ref, y_ref, u_ref):
    cw = cw_ref[...]
    u = cg_ref[...].astype(F32) * hh_ref[...].astype(F32)
    cv = b0_ref[...] * cw[0:1, :] + b1_ref[...] * cw[1:2, :] + u * cw[2:3, :]
    g = bg_ref[...].astype(F32) * cv * _silu(z_ref[...].astype(F32))
    y_ref[...] = x_ref[...] + jnp.dot(g.astype(BF16), w_ref[...], preferred_element_type=F32)
    u_ref[...] = u


def _odd_decode(x, p, b0, b1, cw, w3, li):
    m, d = x.shape
    col = lambda c: pl.BlockSpec((m, d), lambda i, c=c: (0, c))
    full = lambda a: pl.BlockSpec(a.shape, lambda i: (0,) * a.ndim)
    return pl.pallas_call(
        _odd_decode_kernel,
        out_shape=(jax.ShapeDtypeStruct((m, d), F32), jax.ShapeDtypeStruct((m, d), F32)),
        grid=(1,),
        in_specs=[full(x), col(0), col(1), col(2), col(3), full(b0), full(b1), full(cw), _layer_spec(w3, li)],
        out_specs=(pl.BlockSpec((m, d), lambda i: (0, 0)), pl.BlockSpec((m, d), lambda i: (0, 0))),
        compiler_params=_cparams("arbitrary"),
        name="odd_decode",
    )(x, p, p, p, p, b0, b1, cw, w3)


def _tile(n, want):
    t = math.gcd(n, want)
    assert t == n or t % 8 == 0, (n, want)
    return t


def _pad_lanes(v):
    return jnp.pad(v.astype(F32), (0, LANE - v.shape[0])).reshape(1, LANE)


def kernel(x_prompt, x_sample, cache_k, cache_v, page_table, state_gdn, state_gdn_conv, state_shortconv, norm_w, rel_table, w_in_even, w_out_even, qn_w, kn_w, lam_q1, lam_k1, lam_q2, lam_k2, subln_w, gdn_conv_w, gdn_a_log, gdn_dt_bias, gdn_norm_w, w_in_odd, sc_conv_w, w_out_odd):
    nbp, t, d = x_prompt.shape
    nbs = x_sample.shape[0]
    page = cache_k.shape[2]
    assert x_sample.shape[1] == 1 and page >= MAX_DISTANCE and t % 8 == 0 and DEPTH % 2 == 0
    rows = _tile(t, SEQ_ROWS)
    tq = rows
    assert tq >= MAX_DISTANCE

    xp = x_prompt.reshape(nbp * t, d)
    xs = x_sample.reshape(nbs, d)

    g64 = jnp.arange(W_A) // D_HA
    bd = jnp.where(g64[:, None] == g64[None, :], 1.0 / D_HA, 0.0).astype(BF16)
    bias_p = _bias_tiles(rel_table.astype(F32), tq)
    bias_s = _bias_decode(rel_table.astype(F32), page)

    n_even = (DEPTH + 1) // 2
    w_in_e, w_out_e = w_in_even.astype(BF16), w_out_even.astype(BF16)
    w_in_o, w_out_o = w_in_odd.astype(BF16), w_out_odd.astype(BF16)
    kv_p = kv_s = s_s = None
    sp, gcp, scp = [], [], []
    gcs, scs = [], []
    ei = oi = 0
    for li in range(DEPTH):
        nw = norm_w[li].reshape(1, d)
        if li % 2 == 0:
            lambda_init = 0.8 - 0.6 * math.exp(-0.3 * li)
            w_ab = jnp.pad(w_in_even[ei, :, P_MAIN:], ((0, 0), (0, LANE - 2 * H_B))).astype(BF16)
            qw = jnp.tile(qn_w[ei], W_A // D_HA).reshape(1, W_A)
            kw = jnp.tile(kn_w[ei], W_A // D_HA).reshape(1, W_A)
            sw = subln_w[ei].reshape(1, DA)
            lam = (jnp.exp(jnp.sum(lam_q1[ei] * lam_k1[ei]).astype(F32))
                   - jnp.exp(jnp.sum(lam_q2[ei] * lam_k2[ei]).astype(F32)) + lambda_init).reshape(1)
            cw = gdn_conv_w[ei]
            alog = _pad_lanes(gdn_a_log[ei])
            dtb = _pad_lanes(gdn_dt_bias[ei])
            gnw = gdn_norm_w[ei].reshape(1, DV_B)

            qb, kb, *kv_p, p, ab, tail = _even_in(xp, nw, w_in_e, w_ab, qw, kw, bd, rows, ei, n_even, kv_p,
                                                  conv_w=cw, seq_len=t)
            oa = _attn_prompt(lam, qb, kb, bias_p, p, sw, nbp, t, tq, 1.0 - lambda_init)
            ob, s_new = _gdn_prompt(p, ab, alog, dtb, gnw, nbp, t, rows)
            prev_p = (oa, ob, w_out_e, ei)
            sp.append(s_new)
            gcp.append(tail[:, 8 - (GDN_CONV - 1):, :])

            qb, kb, *kv_s, p, ab = _even_in(xs, nw, w_in_e, w_ab, qw, kw, bd, nbs, ei, n_even, kv_s)
            oa = _attn_decode(page_table, lam, qb, kb, cache_k, cache_v, ei, bias_s, p, sw, 1.0 - lambda_init)
            conv0_t = jnp.swapaxes(state_gdn_conv[ei], 0, 1)
            qn, kn, vv, eg, beta, qk, conv_new = _gdn_decode_prep(p, ab, conv0_t, cw, alog, dtb)
            bb = math.gcd(nbs, DECODE_STATE_SEQS)
            to_cols = lambda a: a.reshape(nbs // bb, bb, H_B, DK_B).transpose(0, 2, 3, 1)
            s_s, ob = _gdn_decode(state_gdn, ei, to_cols(qn), to_cols(kn), vv, eg, beta, qk, p, gnw, bb, s_s)
            xs = _out_proj(xs, oa, ob, w_out_e, ei, nbs)
            gcs.append(jnp.swapaxes(conv_new, 0, 1))
            ei += 1
        else:
            cw = sc_conv_w[oi]
            xp, tail = _odd_prompt(xp, *prev_p, nw, w_in_o, cw, w_out_o, oi, nbp, t, rows)
            scp.append(tail[:, 8 - (SC_WIDTH - 1):, :])

            p = _norm_proj(xs, nw, w_in_o, oi, nbs, DECODE_PROJ_COLS)
            buf0 = state_shortconv[oi]
            xs, u = _odd_decode(xs, p, buf0[:, 0, :], buf0[:, 1, :], cw, w_out_o, oi)
            scs.append(jnp.stack([buf0[:, 1, :], u], axis=1))
            oi += 1

    leaf_p = lambda a: a.reshape(n_even, nbp, t, H_A, DA)
    leaf_s = lambda a: a.reshape(n_even, nbs, 1, H_A, DA)
    return (xp.reshape(nbp, t, d), xs.reshape(nbs, 1, d),
            leaf_p(kv_p[0]), leaf_p(kv_p[1]), jnp.stack(sp), jnp.stack(gcp), jnp.stack(scp),
            leaf_s(kv_s[0]), leaf_s(kv_s[1]), s_s, jnp.stack(gcs), jnp.stack(scs))
```

```python
import functools
import math

import jax
import jax.numpy as jnp
from jax import lax
from jax.experimental import pallas as pl
from jax.experimental.pallas import tpu as pltpu

F32, BF16 = jnp.float32, jnp.bfloat16

DEPTH = 4
H_A, D_HA = 4, 64
DA = 2 * D_HA
W_A = H_A * DA
H_B, DK_B, DV_B = 4, 128, 128
W_B = H_B * DV_B
QKV_B = 2 * H_B * DK_B + H_B * DV_B
GDN_CONV, GDN_CHUNK, SC_WIDTH = 4, 64, 3
NUM_BUCKETS, MAX_EXACT, MAX_DISTANCE = 32, 16, 128
EPS, NEG = 1e-6, -1e30
LOG2E = math.log2(math.e)
LANE = 128
P_MAIN = 4 * W_A + QKV_B + W_B
VMEM_LIMIT = 48 * 1024 * 1024
LOCAL_GROUP = 8
SEQ_ROWS = 512
DECODE_STATE_SEQS = 8
DECODE_PROJ_COLS = 1024
DECODE_ATTN_SEQS = 2


def _layer_spec(w3, li, cols=None):
    return pl.BlockSpec((None, w3.shape[1], cols or w3.shape[2]), lambda *_: (li, 0, 0))


def _cparams(*sem):
    return pltpu.CompilerParams(dimension_semantics=sem, vmem_limit_bytes=VMEM_LIMIT)


def _silu(z):
    h = 0.5 * z
    return h + h * jnp.tanh(h)


def _sigmoid(z):
    return 1.0 / (1.0 + jnp.exp(-z))


def _dot(a, b):
    return jnp.dot(a.astype(BF16), b.astype(BF16), preferred_element_type=F32)


def _dot_nt(a, b):
    return lax.dot_general(a.astype(BF16), b.astype(BF16), (((1,), (1,)), ((), ())), preferred_element_type=F32)


def _rms_rows(x, w):
    return x * lax.rsqrt(jnp.mean(x * x, axis=-1, keepdims=True) + EPS) * w


def _norm_proj_kernel(x_ref, nw_ref, w_ref, o_ref, h_ref):
    @pl.when(pl.program_id(1) == 0)
    def _():
        h_ref[...] = _rms_rows(x_ref[...], nw_ref[...]).astype(BF16)

    o_ref[...] = jnp.dot(h_ref[...], w_ref[...], preferred_element_type=F32).astype(BF16)


def _norm_proj(x, nw, w3, li, tm, tn):
    m, d = x.shape
    n = w3.shape[2]
    return pl.pallas_call(
        _norm_proj_kernel,
        out_shape=jax.ShapeDtypeStruct((m, n), BF16),
        grid=(m // tm, n // tn),
        in_specs=[pl.BlockSpec((tm, d), lambda i, j: (i, 0)),
                  pl.BlockSpec((1, d), lambda i, j: (0, 0)),
                  pl.BlockSpec((None, d, tn), lambda i, j: (li, 0, j))],
        out_specs=pl.BlockSpec((tm, tn), lambda i, j: (i, j)),
        scratch_shapes=[pltpu.VMEM((tm, d), BF16)],
        compiler_params=_cparams("arbitrary", "arbitrary"),
        name="norm_proj",
    )(x, nw, w3)


def _even_in_kernel(x_ref, nw_ref, w_ref, wab_ref, qw_ref, kw_ref, bd_ref, *refs, n_alias, tiles_per_seq):
    if tiles_per_seq:
        cw_ref = refs[0]
        qo_ref, kb_ref, kf_ref, vf_ref, p_ref, ab_ref, tail_ref, carry_ref = refs[1 + n_alias:]
    else:
        qo_ref, kb_ref, kf_ref, vf_ref, p_ref, ab_ref = refs[n_alias:]
    tm = x_ref.shape[0]
    h = _rms_rows(x_ref[...], nw_ref[...]).astype(BF16)
    proj = lambda c: jnp.dot(h, w_ref[:, c * W_A:(c + 1) * W_A], preferred_element_type=F32)
    ab_ref[...] = jnp.dot(h, wab_ref[...], preferred_element_type=F32)
    bd = bd_ref[...]

    def group_norm(x, w):
        ms = jnp.dot((x * x).astype(BF16), bd, preferred_element_type=F32)
        return x * lax.rsqrt(ms + EPS) * w

    def leaf(ref, val):
        for hd in range(H_A):
            ref[pl.ds(hd, tm, stride=H_A), :] = val[:, hd * DA:(hd + 1) * DA]

    qn = group_norm(proj(0), qw_ref[...])
    qo_ref[...] = (qn * (D_HA ** -0.5 * LOG2E)).astype(BF16)
    kn = group_norm(proj(1), kw_ref[...])
    kb_ref[...] = kn.astype(BF16)
    leaf(kf_ref, kn)
    v = proj(2)
    leaf(vf_ref, v)
    p_ref[:, 0:W_A] = v.astype(BF16)
    if tiles_per_seq:
        first = pl.program_id(0) % tiles_per_seq == 0

        @pl.when(first)
        def _():
            carry_ref[...] = jnp.zeros(carry_ref.shape, F32)

    c_qkv = (4 * W_A) // W_A
    for c in range(3, P_MAIN // W_A):
        y = proj(c)
        seg = c - c_qkv
        if tiles_per_seq and 0 <= seg < QKV_B // W_B:
            cols = slice(seg * W_B, (seg + 1) * W_B)
            raw = y
            y = _silu(_shifted_conv(raw, carry_ref[:, cols], cw_ref[:, cols]))
            carry_ref[:, cols] = raw[tm - 8:tm]
            tail_ref[:, cols] = raw[tm - 8:tm]
        p_ref[:, (c - 2) * W_A:(c - 1) * W_A] = y.astype(BF16)


def _even_in(x, nw, w3, w_ab, qw, kw, bd, tm, ei, n_layers, kv_prev, conv_w=None, seq_len=None):
    m, d = x.shape
    tiles_per_seq = 0 if conv_w is None else seq_len // tm
    n_rest = P_MAIN - 2 * W_A
    full = lambda a: pl.BlockSpec(a.shape, lambda i: (0,) * a.ndim)
    row = lambda n: pl.BlockSpec((tm, n), lambda i: (i, 0))
    leaf = pl.BlockSpec((None, tm * H_A, DA), lambda i: (ei, i, 0))
    leaf_shape = jax.ShapeDtypeStruct((n_layers, m * H_A, DA), F32)
    n_alias = 0 if kv_prev is None else 2
    out_shape = [jax.ShapeDtypeStruct((m, W_A), BF16), jax.ShapeDtypeStruct((m, W_A), BF16), leaf_shape, leaf_shape,
                 jax.ShapeDtypeStruct((m, n_rest), BF16), jax.ShapeDtypeStruct((m, LANE), F32)]
    out_specs = [row(W_A), row(W_A), leaf, leaf, row(n_rest), row(LANE)]
    in_specs = [row(d), full(nw), _layer_spec(w3, ei, P_MAIN), full(w_ab), full(qw), full(kw), full(bd)]
    args = [x, nw, w3, w_ab, qw, kw, bd]
    scratch = []
    if tiles_per_seq:
        in_specs.append(full(conv_w))
        args.append(conv_w)
        out_shape.append(jax.ShapeDtypeStruct((m // seq_len, 8, QKV_B), F32))
        out_specs.append(pl.BlockSpec((None, 8, QKV_B), lambda i: (i // tiles_per_seq, 0, 0)))
        scratch.append(pltpu.VMEM((8, QKV_B), F32))
    n_in = len(args)
    return pl.pallas_call(
        functools.partial(_even_in_kernel, n_alias=n_alias, tiles_per_seq=tiles_per_seq),
        out_shape=tuple(out_shape),
        grid=(m // tm,),
        in_specs=in_specs + [pl.BlockSpec(memory_space=pl.ANY)] * n_alias,
        out_specs=tuple(out_specs),
        input_output_aliases={} if kv_prev is None else {n_in: 2, n_in + 1: 3},
        scratch_shapes=scratch,
        compiler_params=_cparams("arbitrary"),
        name="even_in",
    )(*args, *(kv_prev or ()))


def _t5_bias(n, tab_ref, h):
    nf = jnp.maximum(n, 1).astype(F32)
    large = MAX_EXACT + (jnp.log(nf / MAX_EXACT) / math.log(MAX_DISTANCE / MAX_EXACT)
                         * (NUM_BUCKETS - MAX_EXACT)).astype(jnp.int32)
    large = jnp.minimum(large, NUM_BUCKETS - 1)
    bkt = jnp.where(n < MAX_EXACT, n, large)
    out = jnp.zeros(n.shape, F32)
    for b in range(NUM_BUCKETS):
        out = jnp.where(bkt == b, tab_ref[b, h], out)
    return (out - tab_ref[NUM_BUCKETS - 1, h]) * LOG2E


def _bias_tiles_kernel(tab_ref, o_ref, *, tq):
    h = pl.program_id(0)
    i = lax.broadcasted_iota(jnp.int32, (tq, tq), 0)
    j = lax.broadcasted_iota(jnp.int32, (tq, tq), 1)
    n0 = i - j
    o_ref[0, :, 0:tq] = jnp.where(n0 >= 0, _t5_bias(jnp.maximum(n0, 0), tab_ref, h), NEG)
    o_ref[0, :, tq:tq + MAX_DISTANCE] = _t5_bias(n0[:, 0:MAX_DISTANCE] + MAX_DISTANCE, tab_ref, h)


def _bias_tiles(rel_table, tq):
    assert MAX_DISTANCE == LANE
    return pl.pallas_call(
        functools.partial(_bias_tiles_kernel, tq=tq),
        out_shape=jax.ShapeDtypeStruct((H_A, tq, tq + MAX_DISTANCE), F32),
        grid=(H_A,),
        in_specs=[pl.BlockSpec(memory_space=pltpu.SMEM)],
        out_specs=pl.BlockSpec((1, tq, tq + MAX_DISTANCE), lambda h: (h, 0, 0)),
        compiler_params=_cparams("arbitrary"),
        name="bias_tiles",
    )(rel_table)


def _bias_decode_kernel(tab_ref, o_ref, *, page):
    w = page * H_A
    row = lax.broadcasted_iota(jnp.int32, (2 * H_A, w), 0)
    lane = lax.broadcasted_iota(jnp.int32, (2 * H_A, w), 1)
    valid = (lane & (H_A - 1)) == (row >> 1)
    n = page - (lane >> 2)
    near = jnp.zeros((2 * H_A, w), F32)
    new = jnp.zeros((2 * H_A, LANE), F32)
    for h in range(H_A):
        near = jnp.where((row >> 1) == h, _t5_bias(n, tab_ref, h), near)
        new = jnp.where((row[:, :LANE] >> 1) == h, _t5_bias(jnp.zeros((2 * H_A, LANE), jnp.int32), tab_ref, h), new)
    o_ref[:, 0:w] = jnp.where(valid, 0.0, NEG)
    o_ref[:, w:2 * w] = jnp.where(valid, near, NEG)
    o_ref[:, 2 * w:2 * w + LANE] = new


def _bias_decode(rel_table, page):
    assert H_A == 4
    return pl.pallas_call(
        functools.partial(_bias_decode_kernel, page=page),
        out_shape=jax.ShapeDtypeStruct((2 * H_A, 2 * page * H_A + LANE), F32),
        in_specs=[pl.BlockSpec(memory_space=pltpu.SMEM)],
        out_specs=pl.BlockSpec(memory_space=pltpu.VMEM),
        name="bias_decode",
    )(rel_table)


def _attn_finish(o, sw, z, out_scale):
    ms = jnp.mean(o * o, axis=-1, keepdims=True)
    return o * lax.rsqrt(ms + EPS) * sw * out_scale * _silu(z.astype(F32))


def _attn_prompt_kernel(lam_ref, q_ref, k_ref, v_ref, bias_ref, za_ref, sw_ref, o_ref, m, l, a, *, tq, out_scale):
    reps = tq // LANE
    lane = lax.broadcasted_iota(jnp.int32, (tq, DA), 1)

    def q_block(qi, carry):
        qrows = pl.ds(pl.multiple_of(qi * tq, tq), tq)
        q = q_ref[qrows, :].astype(F32)
        q2 = jnp.concatenate([jnp.where(lane < D_HA, q, 0.0), jnp.where(lane >= D_HA, q, 0.0)], axis=0).astype(BF16)
        m[...] = jnp.full(m.shape, NEG, F32)
        l[...] = jnp.zeros(l.shape, F32)
        a[...] = jnp.zeros(a.shape, F32)

        def step(j, bias):
            rows = pl.ds(pl.multiple_of(j * tq, tq), tq)
            s = lax.dot_general(q2, k_ref[rows, :], (((1,), (1,)), ((), ())), preferred_element_type=F32)
            if bias is not None:
                c0 = tq - bias.shape[1]
                tail = s[:, c0:] + jnp.concatenate([bias, bias], axis=0)
                s = tail if c0 == 0 else jnp.concatenate([s[:, :c0], tail], axis=1)
            m_prev = m[...]
            m_new = jnp.maximum(m_prev, jnp.max(s, axis=1, keepdims=True))
            p = jnp.exp2(s - jnp.concatenate([m_new] * reps, axis=1))
            alpha = jnp.exp2(m_prev - m_new)
            l[...] = alpha * l[...] + jnp.sum(p, axis=1, keepdims=True)
            a[...] = alpha * a[...] + jnp.dot(p.astype(BF16), v_ref[rows, :], preferred_element_type=F32)
            m[...] = m_new

        def far(j, c):
            step(j, None)
            return c

        lax.fori_loop(0, jnp.maximum(qi - 1, 0), far, 0)

        @pl.when(qi >= 1)
        def _():
            step(qi - 1, bias_ref[0, :, tq:tq + MAX_DISTANCE])

        step(qi, bias_ref[0, :, 0:tq])
        o = a[...] / l[...]
        o = o[:tq] - lam_ref[0] * o[tq:]
        o_ref[qrows, :] = _attn_finish(o, sw_ref[...], za_ref[qrows, :], out_scale).astype(BF16)
        return carry

    lax.fori_loop(0, q_ref.shape[0] // tq, q_block, 0)


def _attn_prompt(lam, q, k, bias, p, sw, nb, t, tq, out_scale):
    m = q.shape[0]
    za0 = W_A // DA
    v0 = 0
    seq = lambda c0: pl.BlockSpec((t, DA), lambda b, h, c0=c0: (b, c0 + h))
    return pl.pallas_call(
        functools.partial(_attn_prompt_kernel, tq=tq, out_scale=out_scale),
        out_shape=jax.ShapeDtypeStruct((m, W_A), BF16),
        grid=(nb, H_A),
        in_specs=[pl.BlockSpec(memory_space=pltpu.SMEM),
                  seq(0), seq(0), seq(v0),
                  pl.BlockSpec((1, tq, tq + MAX_DISTANCE), lambda b, h: (h, 0, 0)),
                  seq(za0),
                  pl.BlockSpec((1, DA), lambda b, h: (0, 0))],
        out_specs=seq(0),
        scratch_shapes=[pltpu.VMEM((2 * tq, LANE), F32), pltpu.VMEM((2 * tq, LANE), F32),
                        pltpu.VMEM((2 * tq, DA), F32)],
        compiler_params=_cparams("arbitrary", "arbitrary"),
        name="attn_prompt",
    )(lam, q, k, p, bias, p, sw)


def _attn_decode_kernel(pt_ref, lam_ref, q_ref, kn_ref, vn_ref, *refs, page, npg, nseq, out_scale):
    ck_refs, cv_refs = refs[:nseq * npg], refs[nseq * npg:2 * nseq * npg]
    bias_ref, za_ref, sw_ref, o_ref = refs[2 * nseq * npg:]
    rows, w = 2 * H_A, page * H_A
    row = lax.broadcasted_iota(jnp.int32, (rows, DA), 0)
    lane = lax.broadcasted_iota(jnp.int32, (rows, DA), 1)

    def head_rows(x):
        out = jnp.zeros((rows, DA), F32)
        for h in range(H_A):
            out = jnp.where((row >> 1) == h, jnp.broadcast_to(x[:, h * DA:(h + 1) * DA], (rows, DA)), out)
        return out

    sw = sw_ref[...]
    for sq in range(nseq):
        q8 = jnp.where((lane >> 6) == (row & 1), head_rows(q_ref[sq].astype(F32)), 0.0)
        q8b = q8.astype(BF16)
        s_all = []
        for j in range(npg):
            s = lax.dot_general(q8b, ck_refs[sq * npg + j][...].astype(BF16), (((1,), (1,)), ((), ())),
                                preferred_element_type=F32)
            s_all.append(s + (bias_ref[:, w:2 * w] if j == npg - 1 else bias_ref[:, 0:w]))
        s_new = (jnp.sum(q8 * head_rows(kn_ref[sq].astype(F32)), axis=1, keepdims=True)
                 + bias_ref[:, 2 * w:2 * w + 1])
        m = s_all[0]
        for s in s_all[1:]:
            m = jnp.maximum(m, s)
        m = jnp.maximum(jnp.max(m, axis=1, keepdims=True), s_new)
        p_new = jnp.exp2(s_new - m)
        l = p_new
        acc = p_new * head_rows(vn_ref[sq].astype(F32))
        for j in range(npg):
            p = jnp.exp2(s_all[j] - m)
            l = l + jnp.sum(p, axis=1, keepdims=True)
            acc = acc + jnp.dot(p.astype(BF16), cv_refs[sq * npg + j][...].astype(BF16), preferred_element_type=F32)
        o = acc / l
        z = za_ref[sq]
        outs = []
        for h in range(H_A):
            oh = o[2 * h:2 * h + 1, :] - lam_ref[0] * o[2 * h + 1:2 * h + 2, :]
            outs.append(_attn_finish(oh, sw, z[:, h * DA:(h + 1) * DA], out_scale))
        o_ref[sq] = jnp.concatenate(outs, axis=1).astype(BF16)


def _attn_decode(page_table, lam, q, kn, cache_k, cache_v, ei, bias, p, sw, out_scale):
    nb, npg = page_table.shape
    page = cache_k.shape[2]
    nseq = math.gcd(nb, DECODE_ATTN_SEQS)
    ck = cache_k.reshape(cache_k.shape[0], cache_k.shape[1], page * H_A, DA)
    cv = cache_v.reshape(cache_v.shape[0], cache_v.shape[1], page * H_A, DA)
    r3 = lambda a: a.reshape(nb, 1, a.shape[-1])
    vec = lambda c: pl.BlockSpec((nseq, 1, W_A), lambda b, pt, c=c: (b, 0, c))
    cache = [pl.BlockSpec((None, None, page * H_A, DA),
                          lambda b, pt, sq=sq, j=j: (ei, pt[(b * nseq + sq) * npg + j], 0, 0))
             for sq in range(nseq) for j in range(npg)]
    out = pl.pallas_call(
        functools.partial(_attn_decode_kernel, page=page, npg=npg, nseq=nseq, out_scale=out_scale),
        out_shape=jax.ShapeDtypeStruct((nb, 1, W_A), BF16),
        grid_spec=pltpu.PrefetchScalarGridSpec(
            num_scalar_prefetch=1,
            grid=(nb // nseq,),
            in_specs=[pl.BlockSpec(memory_space=pltpu.SMEM), vec(0), vec(0), vec(0)] + cache + cache
                     + [pl.BlockSpec(bias.shape, lambda b, pt: (0, 0)),
                        vec(1),
                        pl.BlockSpec((1, DA), lambda b, pt: (0, 0))],
            out_specs=pl.BlockSpec((nseq, 1, W_A), lambda b, pt: (b, 0, 0))),
        compiler_params=_cparams("arbitrary"),
        name="attn_decode",
    )(page_table.reshape(-1), lam, r3(q), r3(kn), r3(p), *([ck] * (nseq * npg)), *([cv] * (nseq * npg)), bias, r3(p), sw)
    return out.reshape(nb, W_A)


def _shifted_conv(x, prev, w):
    r, c = x.shape
    taps = w.shape[0]
    x3 = x.reshape(r // 8, 8, c)
    sub = lax.broadcasted_iota(jnp.int32, x3.shape, 1)
    y = x3 * w[taps - 1:taps, :].reshape(1, 1, c)
    for s in range(1, taps):
        rot = pltpu.roll(x3, s, 1)
        before = jnp.concatenate([pltpu.roll(prev, s, 0)[None], rot[:-1]], axis=0)
        y = y + jnp.where(sub < s, before, rot) * w[taps - 1 - s:taps - s, :].reshape(1, 1, c)
    return y.reshape(r, c)


def _gates(ab, alog, dtb):
    x = ab + dtb
    sp = jnp.maximum(x, 0.0) + jnp.log(1.0 + jnp.exp(-jnp.abs(x)))
    return -jnp.exp(alog) * sp, _sigmoid(ab)


def _l2norm(x):
    return x * lax.rsqrt(jnp.sum(x * x, axis=-1, keepdims=True) + EPS)


def _rms(x, w):
    return x * lax.rsqrt(jnp.mean(x * x, axis=-1, keepdims=True) + EPS) * w


def _level_masks(c):
    i = lax.broadcasted_iota(jnp.int32, (c, c), 0)
    j = lax.broadcasted_iota(jnp.int32, (c, c), 1)
    masks = []
    s = 1
    while s < c:
        sh = s.bit_length() - 1
        masks.append((((i >> (sh + 1)) == (j >> (sh + 1))) & ((i >> sh) != (j >> sh)) & (i > j)).astype(F32))
        s *= 2
    return masks


def _gdn_prompt_kernel(cq_ref, ck_ref, cv_ref, zb_ref, ab_ref, alog_ref, dtb_ref, gnw_ref,
                       ob_ref, s_ref, g_ref, beta_ref, u_ref, wq_ref, ak_ref, gl_ref, *, rows, chunk, nseq):
    t = pl.program_id(1)
    c = chunk

    @pl.when(t == 0)
    def _():
        s_ref[...] = jnp.zeros(s_ref.shape, F32)

    ii = lax.broadcasted_iota(jnp.int32, (c, c), 0)
    jj = lax.broadcasted_iota(jnp.int32, (c, c), 1)
    incl = ii >= jj
    ltri = incl.astype(BF16)
    masks = _level_masks(c)

    def prep(b, ci):
        rs = pl.ds(pl.multiple_of(ci * c, c), c)
        gch = g_ref[b, rs, :]
        g1 = gch.astype(BF16)
        r1 = gch - g1.astype(F32)
        g2 = r1.astype(BF16)
        g3 = (r1 - g2.astype(F32)).astype(BF16)
        d = functools.partial(jnp.dot, preferred_element_type=F32)
        gc = d(ltri, g1) + d(ltri, g2) + d(ltri, g3)
        gct = jnp.concatenate([gc, gc], axis=0).T
        ge = jnp.exp(gc)
        kdec = jnp.exp(gc[c - 1:c, :] - gc)
        gl_ref[b, ci] = ge[c - 1:c, :]
        return rs, gc, gct, ge, kdec, beta_ref[b, rs, :]

    def local(b, ig):
        items = []
        for ci in [LOCAL_GROUP * ig + k for k in range(LOCAL_GROUP)]:
            rs, gc, gct, ge, kdec, bch = prep(b, ci)
            for h in range(H_B):
                hc = slice(h * DK_B, (h + 1) * DK_B)
                qh = _l2norm(cq_ref[b, rs, hc].astype(F32)) * (DK_B ** -0.5)
                kh = _l2norm(ck_ref[b, rs, hc].astype(F32))
                vh = cv_ref[b, rs, hc].astype(F32)
                bcol = bch[:, H_B + h:H_B + h + 1]
                gecol = ge[:, h:h + 1]
                decay = jnp.exp(jnp.where(incl, gc[:, h:h + 1] - gct[h:h + 1, :c], NEG))
                kbeta = kh * bcol
                wq_ref[b, ci, h, c:2 * c] = (qh * gecol).astype(BF16)
                ak_ref[b, ci, h, c:c + DK_B] = (kh * kdec[:, h:h + 1]).T.astype(BF16)
                items.append(dict(ci=ci, h=h, rs=rs, hc=hc, decay=decay, qk=(kbeta, qh, kh),
                                  rhs=jnp.concatenate([vh * bcol, kbeta * gecol], axis=1)))
        for it in items:
            kbeta, qh, kh = it.pop("qk")
            both = _dot_nt(jnp.concatenate([kbeta, qh], axis=0), kh)
            it["mm"] = jnp.where(ii > jj, both[:c] * it["decay"], 0.0)
            ak_ref[b, it["ci"], it["h"], 0:c] = (both[c:] * it["decay"]).astype(BF16)
        for it in items:
            it["pp"] = -(it["mm"] * masks[0])
        for mk in masks[1:]:
            for it in items:
                e = it["mm"] * mk
                it["x"] = e + _dot(it["pp"], e)
            for it in items:
                it["pp"] = it["pp"] - (it["x"] + _dot(it["x"], it["pp"]))
        for it in items:
            uw = it["rhs"] + _dot(it["pp"], it["rhs"])
            u_ref[b, it["rs"], it["hc"]] = uw[:, :DV_B]
            wq_ref[b, it["ci"], it["h"], 0:c] = uw[:, DV_B:].astype(BF16)

    for b in range(nseq):
        g, beta = _gates(ab_ref[b], alog_ref[...], dtb_ref[...])
        g_ref[b] = g
        beta_ref[b] = beta

        def local_b(ig, carry, b=b):
            local(b, ig)
            return carry

        lax.fori_loop(0, rows // (LOCAL_GROUP * c), local_b, 0)
    gnw = gnw_ref[...]

    def scan(ci, carry):
        rs = pl.ds(pl.multiple_of(ci * c, c), c)
        chains = [(b, h) for b in range(nseq) for h in range(H_B)]
        hc = lambda h: slice(h * DK_B, (h + 1) * DK_B)
        s_old = [s_ref[b, h] for b, h in chains]
        r = [jnp.dot(wq_ref[b, ci, h], s_old[k].astype(BF16), preferred_element_type=F32)
             for k, (b, h) in enumerate(chains)]
        v_new = [u_ref[b, rs, hc(h)] - r[k][:c] for k, (b, h) in enumerate(chains)]
        r2 = [jnp.dot(ak_ref[b, ci, h], v_new[k].astype(BF16), preferred_element_type=F32)
              for k, (b, h) in enumerate(chains)]
        for k, (b, h) in enumerate(chains):
            s_ref[b, h] = s_old[k] * gl_ref[b, ci][:, h:h + 1] + r2[k][c:]
            ob_ref[b, rs, hc(h)] = (_rms(r[k][c:] + r2[k][:c], gnw)
                                    * _silu(zb_ref[b, rs, hc(h)].astype(F32))).astype(BF16)
        return carry

    lax.fori_loop(0, rows // c, scan, 0)


def _gdn_prompt(p, ab, alog, dtb, gnw, nb, t, rows):
    m = p.shape[0]
    nt = t // rows
    c = math.gcd(GDN_CHUNK, t)
    nseq = 2 if nb % 2 == 0 else 1
    p3 = p.reshape(nb, t, p.shape[1])
    col = lambda cidx: pl.BlockSpec((nseq, rows, W_B), lambda b, i, cidx=cidx: (b, i, cidx))
    full = lambda a: pl.BlockSpec(a.shape, lambda b, i: (0,) * a.ndim)
    c0 = (2 * W_A) // W_B
    ob, s_new = pl.pallas_call(
        functools.partial(_gdn_prompt_kernel, rows=rows, chunk=c, nseq=nseq),
        out_shape=(jax.ShapeDtypeStruct((nb, t, W_B), BF16), jax.ShapeDtypeStruct((nb, H_B, DK_B, DV_B), F32)),
        grid=(nb // nseq, nt),
        in_specs=[col(c0), col(c0 + 1), col(c0 + 2), col(c0 + 3),
                  pl.BlockSpec((nseq, rows, LANE), lambda b, i: (b, i, 0)),
                  full(alog), full(dtb), full(gnw)],
        out_specs=(pl.BlockSpec((nseq, rows, W_B), lambda b, i: (b, i, 0)),
                   pl.BlockSpec((nseq, H_B, DK_B, DV_B), lambda b, i: (b, 0, 0, 0))),
        scratch_shapes=[pltpu.VMEM((nseq, rows, LANE), F32), pltpu.VMEM((nseq, rows, LANE), F32),
                        pltpu.VMEM((nseq, rows, W_B), F32),
                        pltpu.VMEM((nseq, rows // c, H_B, 2 * c, DK_B), BF16),
                        pltpu.VMEM((nseq, rows // c, H_B, c + DK_B, c), BF16),
                        pltpu.VMEM((nseq, rows // c, 1, LANE), F32)],
        compiler_params=_cparams("arbitrary", "arbitrary"),
        name="gdn_prompt",
    )(p3, p3, p3, p3, ab.reshape(nb, t, LANE), alog, dtb, gnw)
    return ob.reshape(m, W_B), s_new


def _gdn_decode_prep_kernel(pq_ref, pk_ref, pv_ref, ab_ref, c0_ref, cw_ref, alog_ref, dtb_ref,
                            q_ref, k_ref, v_ref, eg_ref, beta_ref, qk_ref, cn_ref):
    cw = cw_ref[...]
    taps = cw.shape[0]
    outs = (q_ref, k_ref, v_ref)
    for seg, ref in enumerate((pq_ref, pk_ref, pv_ref)):
        cols = slice(seg * W_B, (seg + 1) * W_B)
        x = ref[...].astype(F32)
        y = x * cw[taps - 1:taps, cols]
        for j in range(taps - 1):
            y = y + c0_ref[j, :, cols] * cw[j:j + 1, cols]
            if j >= 1:
                cn_ref[j - 1, :, cols] = c0_ref[j, :, cols]
        cn_ref[taps - 2, :, cols] = x
        outs[seg][...] = _silu(y)
    lane = lax.broadcasted_iota(jnp.int32, eg_ref.shape, 1)
    qk = jnp.zeros(eg_ref.shape, F32)
    for h in range(H_B):
        hc = slice(h * DK_B, (h + 1) * DK_B)
        qh = _l2norm(q_ref[:, hc]) * (DK_B ** -0.5)
        kh = _l2norm(k_ref[:, hc])
        q_ref[:, hc] = qh
        k_ref[:, hc] = kh
        qk = jnp.where(lane == h, jnp.sum(qh * kh, axis=-1, keepdims=True), qk)
    g, beta = _gates(ab_ref[...], alog_ref[...], dtb_ref[...])
    eg_ref[...] = jnp.exp(g)
    beta_ref[...] = beta
    qk_ref[...] = qk


def _gdn_decode_prep(p, ab, conv0_t, cw, alog, dtb):
    nb = p.shape[0]
    c0 = (2 * W_A) // W_B
    col = lambda cidx: pl.BlockSpec((nb, W_B), lambda i, cidx=cidx: (0, cidx))
    full = lambda a: pl.BlockSpec(a.shape, lambda i: (0,) * a.ndim)
    wide = jax.ShapeDtypeStruct((nb, W_B), F32)
    narrow = jax.ShapeDtypeStruct((nb, LANE), F32)
    ospec = lambda s: pl.BlockSpec(s.shape, lambda i: (0,) * len(s.shape))
    outs = (wide, wide, wide, narrow, narrow, narrow, jax.ShapeDtypeStruct(conv0_t.shape, F32))
    return pl.pallas_call(
        _gdn_decode_prep_kernel,
        out_shape=outs,
        grid=(1,),
        in_specs=[col(c0), col(c0 + 1), col(c0 + 2),
                  full(ab), full(conv0_t), full(cw), full(alog), full(dtb)],
        out_specs=tuple(ospec(s) for s in outs),
        compiler_params=_cparams("arbitrary"),
        name="gdn_decode_prep",
    )(p, p, p, ab, conv0_t, cw, alog, dtb)


def _gdn_decode_kernel(s_ref, qt_ref, kt_ref, v_ref, eg_ref, beta_ref, qk_ref, zb_ref, gnw_ref, *refs, bb, n_alias):
    so_ref, ob_ref, o_scr = refs[n_alias:]
    v = v_ref[...]
    eg = eg_ref[...]
    beta = beta_ref[...]
    qk = qk_ref[...]
    for h in range(H_B):
        hc = slice(h * DV_B, (h + 1) * DV_B)
        qt = qt_ref[h]
        kt = kt_ref[h]
        for i in range(bb):
            s_old = s_ref[i, h]
            kc = kt[:, i:i + 1]
            qc = qt[:, i:i + 1]
            egs = eg[i:i + 1, h:h + 1]
            ks = jnp.sum(s_old * kc, axis=0, keepdims=True)
            qs = jnp.sum(s_old * qc, axis=0, keepdims=True)
            v_new = beta[i:i + 1, H_B + h:H_B + h + 1] * (v[i:i + 1, hc] - egs * ks)
            o_scr[i:i + 1, hc] = egs * qs + qk[i:i + 1, h:h + 1] * v_new
            so_ref[i, h] = s_old * egs + kc * v_new
    gnw = gnw_ref[...]
    z = zb_ref[...]
    outs = [_rms(o_scr[:, h * DV_B:(h + 1) * DV_B], gnw) * _silu(z[:, h * DV_B:(h + 1) * DV_B].astype(F32))
            for h in range(H_B)]
    ob_ref[...] = jnp.concatenate(outs, axis=1).astype(BF16)


def _gdn_decode(state, ei, qt, kt, v, eg, beta, qk, p, gnw, bb, s_prev):
    nb = v.shape[0]
    ns = nb // bb
    narrow = pl.BlockSpec((bb, LANE), lambda i: (i, 0))
    tr = pl.BlockSpec((None, H_B, DK_B, bb), lambda i: (i, 0, 0, 0))
    zb0 = (2 * W_A + QKV_B) // W_B
    n_alias = 0 if s_prev is None else 1
    return pl.pallas_call(
        functools.partial(_gdn_decode_kernel, bb=bb, n_alias=n_alias),
        out_shape=(jax.ShapeDtypeStruct(state.shape, F32), jax.ShapeDtypeStruct((nb, W_B), BF16)),
        grid=(ns,),
        in_specs=[pl.BlockSpec((None, bb, H_B, DK_B, DV_B), lambda i: (ei, i, 0, 0, 0)),
                  tr, tr,
                  pl.BlockSpec((bb, W_B), lambda i: (i, 0)),
                  narrow, narrow, narrow,
                  pl.BlockSpec((bb, W_B), lambda i: (i, zb0)),
                  pl.BlockSpec((1, DV_B), lambda i: (0, 0))] + [pl.BlockSpec(memory_space=pl.ANY)] * n_alias,
        out_specs=(pl.BlockSpec((None, bb, H_B, DK_B, DV_B), lambda i: (ei, i, 0, 0, 0)),
                   pl.BlockSpec((bb, W_B), lambda i: (i, 0))),
        input_output_aliases={9: 0} if n_alias else {},
        scratch_shapes=[pltpu.VMEM((bb, W_B), F32)],
        compiler_params=_cparams("arbitrary"),
        name="gdn_decode",
    )(state, qt, kt, v, eg, beta, qk, p, gnw, *(() if s_prev is None else (s_prev,)))


def _out_proj_kernel(x_ref, oa_ref, ob_ref, w_ref, y_ref):
    y_ref[...] = (x_ref[...]
                  + jnp.dot(oa_ref[...], w_ref[0:W_A, :], preferred_element_type=F32)
                  + jnp.dot(ob_ref[...], w_ref[W_A:W_A + W_B, :], preferred_element_type=F32))


def _out_proj(x, oa, ob, w3, li, tm):
    m, d = x.shape
    return pl.pallas_call(
        _out_proj_kernel,
        out_shape=jax.ShapeDtypeStruct((m, d), F32),
        grid=(m // tm,),
        in_specs=[pl.BlockSpec((tm, d), lambda i: (i, 0)),
                  pl.BlockSpec((tm, W_A), lambda i: (i, 0)),
                  pl.BlockSpec((tm, W_B), lambda i: (i, 0)),
                  _layer_spec(w3, li)],
        out_specs=pl.BlockSpec((tm, d), lambda i: (i, 0)),
        compiler_params=_cparams("arbitrary"),
        name="out_proj",
    )(x, oa, ob, w3)


def _odd_prompt_kernel(x_ref, oa_ref, ob_ref, wp_ref, nw_ref, wi_ref, cw_ref, wo_ref, y_ref, sc_ref, carry_ref, *, rows):
    t = pl.program_id(1)
    d = x_ref.shape[1]

    @pl.when(t == 0)
    def _():
        carry_ref[...] = jnp.zeros(carry_ref.shape, F32)

    x = (x_ref[...]
         + jnp.dot(oa_ref[...], wp_ref[0:W_A, :], preferred_element_type=F32)
         + jnp.dot(ob_ref[...], wp_ref[W_A:W_A + W_B, :], preferred_element_type=F32))
    h = _rms_rows(x, nw_ref[...]).astype(BF16)
    proj = lambda c: jnp.dot(h, wi_ref[:, c * d:(c + 1) * d], preferred_element_type=F32)
    u = proj(1) * proj(2)
    cv = _shifted_conv(u, carry_ref[...], cw_ref[...])
    carry_ref[...] = u[rows - 8:rows]
    g = proj(0) * cv * _silu(proj(3))
    y_ref[...] = x + jnp.dot(g.astype(BF16), wo_ref[...], preferred_element_type=F32)

    @pl.when(t == pl.num_programs(1) - 1)
    def _():
        sc_ref[...] = u[rows - 8:rows]


def _odd_prompt(x, oa, ob, w_prev3, ei, nw, w_in3, cw, w_out3, oi, nb, t, rows):
    m, d = x.shape
    nt = t // rows
    full = lambda a: pl.BlockSpec(a.shape, lambda b, i: (0,) * a.ndim)
    row = lambda n: pl.BlockSpec((rows, n), lambda b, i: (b * nt + i, 0))
    return pl.pallas_call(
        functools.partial(_odd_prompt_kernel, rows=rows),
        out_shape=(jax.ShapeDtypeStruct((m, d), F32), jax.ShapeDtypeStruct((nb, 8, d), F32)),
        grid=(nb, nt),
        in_specs=[row(d), row(W_A), row(W_B), _layer_spec(w_prev3, ei), full(nw), _layer_spec(w_in3, oi), full(cw),
                  _layer_spec(w_out3, oi)],
        out_specs=(row(d), pl.BlockSpec((None, 8, d), lambda b, i: (b, 0, 0))),
        scratch_shapes=[pltpu.VMEM((8, d), F32)],
        compiler_params=_cparams("arbitrary", "arbitrary"),
        name="odd_prompt",
    )(x, oa, ob, w_prev3, nw, w_in3, cw, w_out3)


def _odd_decode_kernel(x_ref, bg_ref, cg_ref, hh_ref, z_ref, b0_ref, b1_ref, cw_ref, w_ref, y_ref, u_ref):
    cw = cw_ref[...]
    u = cg_ref[...].astype(F32) * hh_ref[...].astype(F32)
    cv = b0_ref[...] * cw[0:1, :] + b1_ref[...] * cw[1:2, :] + u * cw[2:3, :]
    g = bg_ref[...].astype(F32) * cv * _silu(z_ref[...].astype(F32))
    y_ref[...] = x_ref[...] + jnp.dot(g.astype(BF16), w_ref[...], preferred_element_type=F32)
    u_ref[...] = u


def _odd_decode(x, p, b0, b1, cw, w3, li):
    m, d = x.shape
    col = lambda c: pl.BlockSpec((m, d), lambda i, c=c: (0, c))
    full = lambda a: pl.BlockSpec(a.shape, lambda i: (0,) * a.ndim)
    return pl.pallas_call(
        _odd_decode_kernel,
        out_shape=(jax.ShapeDtypeStruct((m, d), F32), jax.ShapeDtypeStruct((m, d), F32)),
        grid=(1,),
        in_specs=[full(x), col(0), col(1), col(2), col(3), full(b0), full(b1), full(cw), _layer_spec(w3, li)],
        out_specs=(pl.BlockSpec((m, d), lambda i: (0, 0)), pl.BlockSpec((m, d), lambda i: (0, 0))),
        compiler_params=_cparams("arbitrary"),
        name="odd_decode",
    )(x, p, p, p, p, b0, b1, cw, w3)


def _tile(n, want):
    t = math.gcd(n, want)
    assert t == n or t % 8 == 0, (n, want)
    return t


def _pad_lanes(v):
    return jnp.pad(v.astype(F32), (0, LANE - v.shape[0])).reshape(1, LANE)


def kernel(x_prompt, x_sample, cache_k, cache_v, page_table, state_gdn, state_gdn_conv, state_shortconv, norm_w, rel_table, w_in_even, w_out_even, qn_w, kn_w, lam_q1, lam_k1, lam_q2, lam_k2, subln_w, gdn_conv_w, gdn_a_log, gdn_dt_bias, gdn_norm_w, w_in_odd, sc_conv_w, w_out_odd):
    nbp, t, d = x_prompt.shape
    nbs = x_sample.shape[0]
    page = cache_k.shape[2]
    assert x_sample.shape[1] == 1 and page >= MAX_DISTANCE and t % 8 == 0 and DEPTH % 2 == 0
    rows = _tile(t, SEQ_ROWS)
    tq = rows
    assert tq >= MAX_DISTANCE

    xp = x_prompt.reshape(nbp * t, d)
    xs = x_sample.reshape(nbs, d)

    g64 = jnp.arange(W_A) // D_HA
    bd = jnp.where(g64[:, None] == g64[None, :], 1.0 / D_HA, 0.0).astype(BF16)
    bias_p = _bias_tiles(rel_table.astype(F32), tq)
    bias_s = _bias_decode(rel_table.astype(F32), page)

    n_even = (DEPTH + 1) // 2
    w_in_e, w_out_e = w_in_even.astype(BF16), w_out_even.astype(BF16)
    w_in_o, w_out_o = w_in_odd.astype(BF16), w_out_odd.astype(BF16)
    kv_p = kv_s = s_s = None
    sp, gcp, scp = [], [], []
    gcs, scs = [], []
    ei = oi = 0
    for li in range(DEPTH):
        nw = norm_w[li].reshape(1, d)
        if li % 2 == 0:
            lambda_init = 0.8 - 0.6 * math.exp(-0.3 * li)
            w_ab = jnp.pad(w_in_even[ei, :, P_MAIN:], ((0, 0), (0, LANE - 2 * H_B))).astype(BF16)
            qw = jnp.tile(qn_w[ei], W_A // D_HA).reshape(1, W_A)
            kw = jnp.tile(kn_w[ei], W_A // D_HA).reshape(1, W_A)
            sw = subln_w[ei].reshape(1, DA)
            lam = (jnp.exp(jnp.sum(lam_q1[ei] * lam_k1[ei]).astype(F32))
                   - jnp.exp(jnp.sum(lam_q2[ei] * lam_k2[ei]).astype(F32)) + lambda_init).reshape(1)
            cw = gdn_conv_w[ei]
            alog = _pad_lanes(gdn_a_log[ei])
            dtb = _pad_lanes(gdn_dt_bias[ei])
            gnw = gdn_norm_w[ei].reshape(1, DV_B)

            qb, kb, *kv_p, p, ab, tail = _even_in(xp, nw, w_in_e, w_ab, qw, kw, bd, rows, ei, n_even, kv_p,
                                                  conv_w=cw, seq_len=t)
            oa = _attn_prompt(lam, qb, kb, bias_p, p, sw, nbp, t, tq, 1.0 - lambda_init)
            ob, s_new = _gdn_prompt(p, ab, alog, dtb, gnw, nbp, t, rows)
            prev_p = (oa, ob, w_out_e, ei)
            sp.append(s_new)
            gcp.append(tail[:, 8 - (GDN_CONV - 1):, :])

            qb, kb, *kv_s, p, ab = _even_in(xs, nw, w_in_e, w_ab, qw, kw, bd, nbs, ei, n_even, kv_s)
            oa = _attn_decode(page_table, lam, qb, kb, cache_k, cache_v, ei, bias_s, p, sw, 1.0 - lambda_init)
            conv0_t = jnp.swapaxes(state_gdn_conv[ei], 0, 1)
            qn, kn, vv, eg, beta, qk, conv_new = _gdn_decode_prep(p, ab, conv0_t, cw, alog, dtb)
            bb = math.gcd(nbs, DECODE_STATE_SEQS)
            to_cols = lambda a: a.reshape(nbs // bb, bb, H_B, DK_B).transpose(0, 2, 3, 1)
            s_s, ob = _gdn_decode(state_gdn, ei, to_cols(qn), to_cols(kn), vv, eg, beta, qk, p, gnw, bb, s_s)
            xs = _out_proj(xs, oa, ob, w_out_e, ei, nbs)
            gcs.append(jnp.swapaxes(conv_new, 0, 1))
            ei += 1
        else:
            cw = sc_conv_w[oi]
            xp, tail = _odd_prompt(xp, *prev_p, nw, w_in_o, cw, w_out_o, oi, nbp, t, rows)
            scp.append(tail[:, 8 - (SC_WIDTH - 1):, :])

            p = _norm_proj(xs, nw, w_in_o, oi, nbs, DECODE_PROJ_COLS)
            buf0 = state_shortconv[oi]
            xs, u = _odd_decode(xs, p, buf0[:, 0, :], buf0[:, 1, :], cw, w_out_o, oi)
            scs.append(jnp.stack([buf0[:, 1, :], u], axis=1))
            oi += 1

    leaf_p = lambda a: a.reshape(n_even, nbp, t, H_A, DA)
    leaf_s = lambda a: a.reshape(n_even, nbs, 1, H_A, DA)
    return (xp.reshape(nbp, t, d), xs.reshape(nbs, 1, d),
            leaf_p(kv_p[0]), leaf_p(kv_p[1]), jnp.stack(sp), jnp.stack(gcp), jnp.stack(scp),
            leaf_s(kv_s[0]), leaf_s(kv_s[1]), s_s, jnp.stack(gcs), jnp.stack(scs))
```

```python
import functools
import math

import jax
import jax.numpy as jnp
from jax import lax
from jax.experimental import pallas as pl
from jax.experimental.pallas import tpu as pltpu

F32, BF16 = jnp.float32, jnp.bfloat16

DEPTH = 4
H_A, D_HA = 4, 64
DA = 2 * D_HA
W_A = H_A * DA
H_B, DK_B, DV_B = 4, 128, 128
W_B = H_B * DV_B
QKV_B = 2 * H_B * DK_B + H_B * DV_B
GDN_CONV, GDN_CHUNK, SC_WIDTH = 4, 64, 3
NUM_BUCKETS, MAX_EXACT, MAX_DISTANCE = 32, 16, 128
EPS, NEG = 1e-6, -1e30
LOG2E = math.log2(math.e)
LANE = 128
P_MAIN = 4 * W_A + QKV_B + W_B
VMEM_LIMIT = 48 * 1024 * 1024
LOCAL_GROUP = 8
SEQ_ROWS = 512
DECODE_STATE_SEQS = 8
DECODE_PROJ_COLS = 1024
DECODE_RING = 3


def _layer_spec(w3, li, cols=None):
    return pl.BlockSpec((None, w3.shape[1], cols or w3.shape[2]), lambda *_: (li, 0, 0))


def _cparams(*sem):
    return pltpu.CompilerParams(dimension_semantics=sem, vmem_limit_bytes=VMEM_LIMIT)


def _silu(z):
    h = 0.5 * z
    return h + h * jnp.tanh(h)


def _sigmoid(z):
    return 1.0 / (1.0 + jnp.exp(-z))


def _dot(a, b):
    return jnp.dot(a.astype(BF16), b.astype(BF16), preferred_element_type=F32)


def _dot_nt(a, b):
    return lax.dot_general(a.astype(BF16), b.astype(BF16), (((1,), (1,)), ((), ())), preferred_element_type=F32)


def _rms_rows(x, w):
    return x * lax.rsqrt(jnp.mean(x * x, axis=-1, keepdims=True) + EPS) * w


def _norm_proj_kernel(x_ref, nw_ref, w_ref, o_ref, h_ref):
    @pl.when(pl.program_id(1) == 0)
    def _():
        h_ref[...] = _rms_rows(x_ref[...], nw_ref[...]).astype(BF16)

    o_ref[...] = jnp.dot(h_ref[...], w_ref[...], preferred_element_type=F32).astype(BF16)


def _norm_proj(x, nw, w3, li, tm, tn):
    m, d = x.shape
    n = w3.shape[2]
    return pl.pallas_call(
        _norm_proj_kernel,
        out_shape=jax.ShapeDtypeStruct((m, n), BF16),
        grid=(m // tm, n // tn),
        in_specs=[pl.BlockSpec((tm, d), lambda i, j: (i, 0)),
                  pl.BlockSpec((1, d), lambda i, j: (0, 0)),
                  pl.BlockSpec((None, d, tn), lambda i, j: (li, 0, j))],
        out_specs=pl.BlockSpec((tm, tn), lambda i, j: (i, j)),
        scratch_shapes=[pltpu.VMEM((tm, d), BF16)],
        compiler_params=_cparams("arbitrary", "arbitrary"),
        name="norm_proj",
    )(x, nw, w3)


def _even_in_kernel(x_ref, nw_ref, w_ref, wab_ref, qw_ref, kw_ref, bd_ref, *refs, n_alias, tiles_per_seq):
    if tiles_per_seq:
        cw_ref = refs[0]
        qo_ref, kb_ref, kf_ref, vf_ref, p_ref, ab_ref, tail_ref, carry_ref = refs[1 + n_alias:]
    else:
        qo_ref, kb_ref, kf_ref, vf_ref, p_ref, ab_ref = refs[n_alias:]
    tm = x_ref.shape[0]
    h = _rms_rows(x_ref[...], nw_ref[...]).astype(BF16)
    proj = lambda c: jnp.dot(h, w_ref[:, c * W_A:(c + 1) * W_A], preferred_element_type=F32)
    ab_ref[...] = jnp.dot(h, wab_ref[...], preferred_element_type=F32)
    bd = bd_ref[...]

    def group_norm(x, w):
        ms = jnp.dot((x * x).astype(BF16), bd, preferred_element_type=F32)
        return x * lax.rsqrt(ms + EPS) * w

    def leaf(ref, val):
        for hd in range(H_A):
            ref[pl.ds(hd, tm, stride=H_A), :] = val[:, hd * DA:(hd + 1) * DA]

    qn = group_norm(proj(0), qw_ref[...])
    qo_ref[...] = (qn * (D_HA ** -0.5 * LOG2E)).astype(BF16)
    kn = group_norm(proj(1), kw_ref[...])
    kb_ref[...] = kn.astype(BF16)
    leaf(kf_ref, kn)
    v = proj(2)
    leaf(vf_ref, v)
    p_ref[:, 0:W_A] = v.astype(BF16)
    if tiles_per_seq:
        first = pl.program_id(0) % tiles_per_seq == 0

        @pl.when(first)
        def _():
            carry_ref[...] = jnp.zeros(carry_ref.shape, F32)

    c_qkv = (4 * W_A) // W_A
    for c in range(3, P_MAIN // W_A):
        y = proj(c)
        seg = c - c_qkv
        if tiles_per_seq and 0 <= seg < QKV_B // W_B:
            cols = slice(seg * W_B, (seg + 1) * W_B)
            raw = y
            y = _silu(_shifted_conv(raw, carry_ref[:, cols], cw_ref[:, cols]))
            carry_ref[:, cols] = raw[tm - 8:tm]
            tail_ref[:, cols] = raw[tm - 8:tm]
        p_ref[:, (c - 2) * W_A:(c - 1) * W_A] = y.astype(BF16)


def _even_in(x, nw, w3, w_ab, qw, kw, bd, tm, ei, n_layers, kv_prev, conv_w=None, seq_len=None):
    m, d = x.shape
    tiles_per_seq = 0 if conv_w is None else seq_len // tm
    n_rest = P_MAIN - 2 * W_A
    full = lambda a: pl.BlockSpec(a.shape, lambda i: (0,) * a.ndim)
    row = lambda n: pl.BlockSpec((tm, n), lambda i: (i, 0))
    leaf = pl.BlockSpec((None, tm * H_A, DA), lambda i: (ei, i, 0))
    leaf_shape = jax.ShapeDtypeStruct((n_layers, m * H_A, DA), F32)
    n_alias = 0 if kv_prev is None else 2
    out_shape = [jax.ShapeDtypeStruct((m, W_A), BF16), jax.ShapeDtypeStruct((m, W_A), BF16), leaf_shape, leaf_shape,
                 jax.ShapeDtypeStruct((m, n_rest), BF16), jax.ShapeDtypeStruct((m, LANE), F32)]
    out_specs = [row(W_A), row(W_A), leaf, leaf, row(n_rest), row(LANE)]
    in_specs = [row(d), full(nw), _layer_spec(w3, ei, P_MAIN), full(w_ab), full(qw), full(kw), full(bd)]
    args = [x, nw, w3, w_ab, qw, kw, bd]
    scratch = []
    if tiles_per_seq:
        in_specs.append(full(conv_w))
        args.append(conv_w)
        out_shape.append(jax.ShapeDtypeStruct((m // seq_len, 8, QKV_B), F32))
        out_specs.append(pl.BlockSpec((None, 8, QKV_B), lambda i: (i // tiles_per_seq, 0, 0)))
        scratch.append(pltpu.VMEM((8, QKV_B), F32))
    n_in = len(args)
    return pl.pallas_call(
        functools.partial(_even_in_kernel, n_alias=n_alias, tiles_per_seq=tiles_per_seq),
        out_shape=tuple(out_shape),
        grid=(m // tm,),
        in_specs=in_specs + [pl.BlockSpec(memory_space=pl.ANY)] * n_alias,
        out_specs=tuple(out_specs),
        input_output_aliases={} if kv_prev is None else {n_in: 2, n_in + 1: 3},
        scratch_shapes=scratch,
        compiler_params=_cparams("arbitrary"),
        name="even_in",
    )(*args, *(kv_prev or ()))


def _t5_bias(n, tab_ref, h):
    nf = jnp.maximum(n, 1).astype(F32)
    large = MAX_EXACT + (jnp.log(nf / MAX_EXACT) / math.log(MAX_DISTANCE / MAX_EXACT)
                         * (NUM_BUCKETS - MAX_EXACT)).astype(jnp.int32)
    large = jnp.minimum(large, NUM_BUCKETS - 1)
    bkt = jnp.where(n < MAX_EXACT, n, large)
    out = jnp.zeros(n.shape, F32)
    for b in range(NUM_BUCKETS):
        out = jnp.where(bkt == b, tab_ref[b, h], out)
    return (out - tab_ref[NUM_BUCKETS - 1, h]) * LOG2E


def _bias_tiles_kernel(tab_ref, o_ref, *, tq):
    h = pl.program_id(0)
    i = lax.broadcasted_iota(jnp.int32, (tq, tq), 0)
    j = lax.broadcasted_iota(jnp.int32, (tq, tq), 1)
    n0 = i - j
    o_ref[0, 0] = jnp.where(n0 >= 0, _t5_bias(jnp.maximum(n0, 0), tab_ref, h), NEG)
    o_ref[0, 1] = _t5_bias(n0 + tq, tab_ref, h)


def _bias_tiles(rel_table, tq):
    return pl.pallas_call(
        functools.partial(_bias_tiles_kernel, tq=tq),
        out_shape=jax.ShapeDtypeStruct((H_A, 2, tq, tq), F32),
        grid=(H_A,),
        in_specs=[pl.BlockSpec(memory_space=pltpu.SMEM)],
        out_specs=pl.BlockSpec((1, 2, tq, tq), lambda h: (h, 0, 0, 0)),
        compiler_params=_cparams("arbitrary"),
        name="bias_tiles",
    )(rel_table)


def _bias_decode_kernel(tab_ref, o_ref, *, page):
    w = page * H_A
    row = lax.broadcasted_iota(jnp.int32, (2 * H_A, w), 0)
    lane = lax.broadcasted_iota(jnp.int32, (2 * H_A, w), 1)
    valid = (lane & (H_A - 1)) == (row >> 1)
    n = page - (lane >> 2)
    near = jnp.zeros((2 * H_A, w), F32)
    new = jnp.zeros((2 * H_A, LANE), F32)
    for h in range(H_A):
        near = jnp.where((row >> 1) == h, _t5_bias(n, tab_ref, h), near)
        new = jnp.where((row[:, :LANE] >> 1) == h, _t5_bias(jnp.zeros((2 * H_A, LANE), jnp.int32), tab_ref, h), new)
    o_ref[:, 0:w] = jnp.where(valid, 0.0, NEG)
    o_ref[:, w:2 * w] = jnp.where(valid, near, NEG)
    o_ref[:, 2 * w:2 * w + LANE] = new


def _bias_decode(rel_table, page):
    assert H_A == 4
    return pl.pallas_call(
        functools.partial(_bias_decode_kernel, page=page),
        out_shape=jax.ShapeDtypeStruct((2 * H_A, 2 * page * H_A + LANE), F32),
        in_specs=[pl.BlockSpec(memory_space=pltpu.SMEM)],
        out_specs=pl.BlockSpec(memory_space=pltpu.VMEM),
        name="bias_decode",
    )(rel_table)


def _attn_finish(o, sw, z, out_scale):
    ms = jnp.mean(o * o, axis=-1, keepdims=True)
    return o * lax.rsqrt(ms + EPS) * sw * out_scale * _silu(z.astype(F32))


def _attn_prompt_kernel(lam_ref, q_ref, k_ref, v_ref, bias_ref, za_ref, sw_ref, o_ref, m, l, a, *, tq, out_scale):
    reps = tq // LANE
    lane = lax.broadcasted_iota(jnp.int32, (tq, DA), 1)

    def q_block(qi, carry):
        qrows = pl.ds(pl.multiple_of(qi * tq, tq), tq)
        q = q_ref[qrows, :].astype(F32)
        q2 = jnp.concatenate([jnp.where(lane < D_HA, q, 0.0), jnp.where(lane >= D_HA, q, 0.0)], axis=0).astype(BF16)
        m[...] = jnp.full(m.shape, NEG, F32)
        l[...] = jnp.zeros(l.shape, F32)
        a[...] = jnp.zeros(a.shape, F32)

        def step(j, bias):
            rows = pl.ds(pl.multiple_of(j * tq, tq), tq)
            s = lax.dot_general(q2, k_ref[rows, :], (((1,), (1,)), ((), ())), preferred_element_type=F32)
            if bias is not None:
                s = s + jnp.concatenate([bias, bias], axis=0)
            m_prev = m[...]
            m_new = jnp.maximum(m_prev, jnp.max(s, axis=1, keepdims=True))
            p = jnp.exp2(s - jnp.concatenate([m_new] * reps, axis=1))
            alpha = jnp.exp2(m_prev - m_new)
            l[...] = alpha * l[...] + jnp.sum(p, axis=1, keepdims=True)
            a[...] = alpha * a[...] + jnp.dot(p.astype(BF16), v_ref[rows, :], preferred_element_type=F32)
            m[...] = m_new

        def far(j, c):
            step(j, None)
            return c

        lax.fori_loop(0, jnp.maximum(qi - 1, 0), far, 0)

        @pl.when(qi >= 1)
        def _():
            step(qi - 1, bias_ref[0, 1])

        step(qi, bias_ref[0, 0])
        o = a[...] / l[...]
        o = o[:tq] - lam_ref[0] * o[tq:]
        o_ref[qrows, :] = _attn_finish(o, sw_ref[...], za_ref[qrows, :], out_scale).astype(BF16)
        return carry

    lax.fori_loop(0, q_ref.shape[0] // tq, q_block, 0)


def _attn_prompt(lam, q, k, bias, p, sw, nb, t, tq, out_scale):
    m = q.shape[0]
    za0 = W_A // DA
    v0 = 0
    seq = lambda c0: pl.BlockSpec((t, DA), lambda b, h, c0=c0: (b, c0 + h))
    return pl.pallas_call(
        functools.partial(_attn_prompt_kernel, tq=tq, out_scale=out_scale),
        out_shape=jax.ShapeDtypeStruct((m, W_A), BF16),
        grid=(nb, H_A),
        in_specs=[pl.BlockSpec(memory_space=pltpu.SMEM),
                  seq(0), seq(0), seq(v0),
                  pl.BlockSpec((1, 2, tq, tq), lambda b, h: (h, 0, 0, 0)),
                  seq(za0),
                  pl.BlockSpec((1, DA), lambda b, h: (0, 0))],
        out_specs=seq(0),
        scratch_shapes=[pltpu.VMEM((2 * tq, LANE), F32), pltpu.VMEM((2 * tq, LANE), F32),
                        pltpu.VMEM((2 * tq, DA), F32)],
        compiler_params=_cparams("arbitrary", "arbitrary"),
        name="attn_prompt",
    )(lam, q, k, p, bias, p, sw)


def _attn_decode_kernel(pt_ref, lam_ref, q_ref, kn_ref, vn_ref, ck_hbm, cv_hbm, bias_ref, za_ref, sw_ref, o_ref,
                        kbuf, vbuf, sem, *, page, npg, ei, out_scale):
    step = pl.program_id(0)
    nstep = pl.num_programs(0)
    rows, w = 2 * H_A, page * H_A
    row = lax.broadcasted_iota(jnp.int32, (rows, DA), 0)
    lane = lax.broadcasted_iota(jnp.int32, (rows, DA), 1)

    def page_copies(sq, slot):
        out = []
        for j in range(npg):
            pg = pt_ref[sq * npg + j]
            out.append(pltpu.make_async_copy(ck_hbm.at[ei, pg], kbuf.at[slot, j], sem.at[slot, 0]))
            out.append(pltpu.make_async_copy(cv_hbm.at[ei, pg], vbuf.at[slot, j], sem.at[slot, 1]))
        return out

    @pl.when(step == 0)
    def _():
        for k in range(DECODE_RING - 1):
            for cp in page_copies(k, k):
                cp.start()

    ahead = step + (DECODE_RING - 1)

    @pl.when(ahead < nstep)
    def _():
        for cp in page_copies(ahead, ahead % DECODE_RING):
            cp.start()

    slot = step % DECODE_RING
    for cp in page_copies(step, slot):
        cp.wait()

    def head_rows(x):
        out = jnp.zeros((rows, DA), F32)
        for h in range(H_A):
            out = jnp.where((row >> 1) == h, jnp.broadcast_to(x[:, h * DA:(h + 1) * DA], (rows, DA)), out)
        return out

    q8 = jnp.where((lane >> 6) == (row & 1), head_rows(q_ref[...].astype(F32)), 0.0)
    q8b = q8.astype(BF16)
    s_all = []
    for j in range(npg):
        s = lax.dot_general(q8b, kbuf[slot, j].astype(BF16), (((1,), (1,)), ((), ())), preferred_element_type=F32)
        s_all.append(s + (bias_ref[:, w:2 * w] if j == npg - 1 else bias_ref[:, 0:w]))
    s_new = (jnp.sum(q8 * head_rows(kn_ref[...].astype(F32)), axis=1, keepdims=True)
             + bias_ref[:, 2 * w:2 * w + 1])
    m = s_all[0]
    for s in s_all[1:]:
        m = jnp.maximum(m, s)
    m = jnp.maximum(jnp.max(m, axis=1, keepdims=True), s_new)
    p_new = jnp.exp2(s_new - m)
    l = p_new
    acc = p_new * head_rows(vn_ref[...].astype(F32))
    for j in range(npg):
        p = jnp.exp2(s_all[j] - m)
        l = l + jnp.sum(p, axis=1, keepdims=True)
        acc = acc + jnp.dot(p.astype(BF16), vbuf[slot, j].astype(BF16), preferred_element_type=F32)
    o = acc / l
    sw = sw_ref[...]
    z = za_ref[...]
    outs = []
    for h in range(H_A):
        oh = o[2 * h:2 * h + 1, :] - lam_ref[0] * o[2 * h + 1:2 * h + 2, :]
        outs.append(_attn_finish(oh, sw, z[:, h * DA:(h + 1) * DA], out_scale))
    o_ref[...] = jnp.concatenate(outs, axis=1).astype(BF16)


def _attn_decode(page_table, lam, q, kn, cache_k, cache_v, ei, bias, p, sw, out_scale):
    nb, npg = page_table.shape
    page = cache_k.shape[2]
    assert nb >= DECODE_RING
    ck = cache_k.reshape(cache_k.shape[0], cache_k.shape[1], page * H_A, DA)
    cv = cache_v.reshape(cache_v.shape[0], cache_v.shape[1], page * H_A, DA)
    r3 = lambda a: a.reshape(nb, 1, a.shape[-1])
    vec = lambda c: pl.BlockSpec((None, 1, W_A), lambda b, pt, c=c: (b, 0, c))
    hbm = pl.BlockSpec(memory_space=pl.ANY)
    ring = pltpu.VMEM((DECODE_RING, npg, page * H_A, DA), F32)
    out = pl.pallas_call(
        functools.partial(_attn_decode_kernel, page=page, npg=npg, ei=ei, out_scale=out_scale),
        out_shape=jax.ShapeDtypeStruct((nb, 1, W_A), BF16),
        grid_spec=pltpu.PrefetchScalarGridSpec(
            num_scalar_prefetch=1,
            grid=(nb,),
            in_specs=[pl.BlockSpec(memory_space=pltpu.SMEM), vec(0), vec(0), vec(0), hbm, hbm,
                      pl.BlockSpec(bias.shape, lambda b, pt: (0, 0)),
                      vec(1),
                      pl.BlockSpec((1, DA), lambda b, pt: (0, 0))],
            out_specs=pl.BlockSpec((None, 1, W_A), lambda b, pt: (b, 0, 0)),
            scratch_shapes=[ring, ring, pltpu.SemaphoreType.DMA((DECODE_RING, 2))]),
        compiler_params=_cparams("arbitrary"),
        name="attn_decode",
    )(page_table.reshape(-1), lam, r3(q), r3(kn), r3(p), ck, cv, bias, r3(p), sw)
    return out.reshape(nb, W_A)


def _shifted_conv(x, prev, w):
    r, c = x.shape
    taps = w.shape[0]
    x3 = x.reshape(r // 8, 8, c)
    sub = lax.broadcasted_iota(jnp.int32, x3.shape, 1)
    y = x3 * w[taps - 1:taps, :].reshape(1, 1, c)
    for s in range(1, taps):
        rot = pltpu.roll(x3, s, 1)
        before = jnp.concatenate([pltpu.roll(prev, s, 0)[None], rot[:-1]], axis=0)
        y = y + jnp.where(sub < s, before, rot) * w[taps - 1 - s:taps - s, :].reshape(1, 1, c)
    return y.reshape(r, c)


def _gates(ab, alog, dtb):
    x = ab + dtb
    sp = jnp.maximum(x, 0.0) + jnp.log(1.0 + jnp.exp(-jnp.abs(x)))
    return -jnp.exp(alog) * sp, _sigmoid(ab)


def _l2norm(x):
    return x * lax.rsqrt(jnp.sum(x * x, axis=-1, keepdims=True) + EPS)


def _rms(x, w):
    return x * lax.rsqrt(jnp.mean(x * x, axis=-1, keepdims=True) + EPS) * w


def _level_masks(c):
    i = lax.broadcasted_iota(jnp.int32, (c, c), 0)
    j = lax.broadcasted_iota(jnp.int32, (c, c), 1)
    masks = []
    s = 1
    while s < c:
        sh = s.bit_length() - 1
        masks.append((((i >> (sh + 1)) == (j >> (sh + 1))) & ((i >> sh) != (j >> sh)) & (i > j)).astype(F32))
        s *= 2
    return masks


def _gdn_prompt_kernel(cq_ref, ck_ref, cv_ref, zb_ref, ab_ref, alog_ref, dtb_ref, gnw_ref,
                       ob_ref, s_ref, g_ref, beta_ref, u_ref, wq_ref, ak_ref, gl_ref, *, rows, chunk, nseq):
    t = pl.program_id(1)
    c = chunk

    @pl.when(t == 0)
    def _():
        s_ref[...] = jnp.zeros(s_ref.shape, F32)

    ii = lax.broadcasted_iota(jnp.int32, (c, c), 0)
    jj = lax.broadcasted_iota(jnp.int32, (c, c), 1)
    incl = ii >= jj
    ltri = incl.astype(BF16)
    masks = _level_masks(c)

    def prep(b, ci):
        rs = pl.ds(pl.multiple_of(ci * c, c), c)
        gch = g_ref[b, rs, :]
        g1 = gch.astype(BF16)
        r1 = gch - g1.astype(F32)
        g2 = r1.astype(BF16)
        g3 = (r1 - g2.astype(F32)).astype(BF16)
        d = functools.partial(jnp.dot, preferred_element_type=F32)
        gc = d(ltri, g1) + d(ltri, g2) + d(ltri, g3)
        gct = jnp.concatenate([gc, gc], axis=0).T
        ge = jnp.exp(gc)
        kdec = jnp.exp(gc[c - 1:c, :] - gc)
        gl_ref[b, ci] = ge[c - 1:c, :]
        return rs, gc, gct, ge, kdec, beta_ref[b, rs, :]

    def local(b, ig):
        items = []
        for ci in [LOCAL_GROUP * ig + k for k in range(LOCAL_GROUP)]:
            rs, gc, gct, ge, kdec, bch = prep(b, ci)
            for h in range(H_B):
                hc = slice(h * DK_B, (h + 1) * DK_B)
                qh = _l2norm(cq_ref[b, rs, hc].astype(F32)) * (DK_B ** -0.5)
                kh = _l2norm(ck_ref[b, rs, hc].astype(F32))
                vh = cv_ref[b, rs, hc].astype(F32)
                bcol = bch[:, H_B + h:H_B + h + 1]
                gecol = ge[:, h:h + 1]
                decay = jnp.exp(jnp.where(incl, gc[:, h:h + 1] - gct[h:h + 1, :c], NEG))
                kbeta = kh * bcol
                wq_ref[b, ci, h, c:2 * c] = (qh * gecol).astype(BF16)
                ak_ref[b, ci, h, c:c + DK_B] = (kh * kdec[:, h:h + 1]).T.astype(BF16)
                items.append(dict(ci=ci, h=h, rs=rs, hc=hc, decay=decay, qk=(kbeta, qh, kh),
                                  rhs=jnp.concatenate([vh * bcol, kbeta * gecol], axis=1)))
        for it in items:
            kbeta, qh, kh = it.pop("qk")
            both = _dot_nt(jnp.concatenate([kbeta, qh], axis=0), kh)
            it["mm"] = jnp.where(ii > jj, both[:c] * it["decay"], 0.0)
            ak_ref[b, it["ci"], it["h"], 0:c] = (both[c:] * it["decay"]).astype(BF16)
        for it in items:
            it["pp"] = -(it["mm"] * masks[0])
        for mk in masks[1:]:
            for it in items:
                e = it["mm"] * mk
                it["x"] = e + _dot(it["pp"], e)
            for it in items:
                it["pp"] = it["pp"] - (it["x"] + _dot(it["x"], it["pp"]))
        for it in items:
            uw = it["rhs"] + _dot(it["pp"], it["rhs"])
            u_ref[b, it["rs"], it["hc"]] = uw[:, :DV_B]
            wq_ref[b, it["ci"], it["h"], 0:c] = uw[:, DV_B:].astype(BF16)

    for b in range(nseq):
        g, beta = _gates(ab_ref[b], alog_ref[...], dtb_ref[...])
        g_ref[b] = g
        beta_ref[b] = beta

        def local_b(ig, carry, b=b):
            local(b, ig)
            return carry

        lax.fori_loop(0, rows // (LOCAL_GROUP * c), local_b, 0)
    gnw = gnw_ref[...]

    def scan(ci, carry):
        rs = pl.ds(pl.multiple_of(ci * c, c), c)
        chains = [(b, h) for b in range(nseq) for h in range(H_B)]
        hc = lambda h: slice(h * DK_B, (h + 1) * DK_B)
        s_old = [s_ref[b, h] for b, h in chains]
        r = [jnp.dot(wq_ref[b, ci, h], s_old[k].astype(BF16), preferred_element_type=F32)
             for k, (b, h) in enumerate(chains)]
        v_new = [u_ref[b, rs, hc(h)] - r[k][:c] for k, (b, h) in enumerate(chains)]
        r2 = [jnp.dot(ak_ref[b, ci, h], v_new[k].astype(BF16), preferred_element_type=F32)
              for k, (b, h) in enumerate(chains)]
        for k, (b, h) in enumerate(chains):
            s_ref[b, h] = s_old[k] * gl_ref[b, ci][:, h:h + 1] + r2[k][c:]
            ob_ref[b, rs, hc(h)] = (_rms(r[k][c:] + r2[k][:c], gnw)
                                    * _silu(zb_ref[b, rs, hc(h)].astype(F32))).astype(BF16)
        return carry

    lax.fori_loop(0, rows // c, scan, 0)


def _gdn_prompt(p, ab, alog, dtb, gnw, nb, t, rows):
    m = p.shape[0]
    nt = t // rows
    c = math.gcd(GDN_CHUNK, t)
    nseq = 2 if nb % 2 == 0 else 1
    p3 = p.reshape(nb, t, p.shape[1])
    col = lambda cidx: pl.BlockSpec((nseq, rows, W_B), lambda b, i, cidx=cidx: (b, i, cidx))
    full = lambda a: pl.BlockSpec(a.shape, lambda b, i: (0,) * a.ndim)
    c0 = (2 * W_A) // W_B
    ob, s_new = pl.pallas_call(
        functools.partial(_gdn_prompt_kernel, rows=rows, chunk=c, nseq=nseq),
        out_shape=(jax.ShapeDtypeStruct((nb, t, W_B), BF16), jax.ShapeDtypeStruct((nb, H_B, DK_B, DV_B), F32)),
        grid=(nb // nseq, nt),
        in_specs=[col(c0), col(c0 + 1), col(c0 + 2), col(c0 + 3),
                  pl.BlockSpec((nseq, rows, LANE), lambda b, i: (b, i, 0)),
                  full(alog), full(dtb), full(gnw)],
        out_specs=(pl.BlockSpec((nseq, rows, W_B), lambda b, i: (b, i, 0)),
                   pl.BlockSpec((nseq, H_B, DK_B, DV_B), lambda b, i: (b, 0, 0, 0))),
        scratch_shapes=[pltpu.VMEM((nseq, rows, LANE), F32), pltpu.VMEM((nseq, rows, LANE), F32),
                        pltpu.VMEM((nseq, rows, W_B), F32),
                        pltpu.VMEM((nseq, rows // c, H_B, 2 * c, DK_B), BF16),
                        pltpu.VMEM((nseq, rows // c, H_B, c + DK_B, c), BF16),
                        pltpu.VMEM((nseq, rows // c, 1, LANE), F32)],
        compiler_params=_cparams("arbitrary", "arbitrary"),
        name="gdn_prompt",
    )(p3, p3, p3, p3, ab.reshape(nb, t, LANE), alog, dtb, gnw)
    return ob.reshape(m, W_B), s_new


def _gdn_decode_prep_kernel(pq_ref, pk_ref, pv_ref, ab_ref, c0_ref, cw_ref, alog_ref, dtb_ref,
                            q_ref, k_ref, v_ref, eg_ref, beta_ref, qk_ref, cn_ref):
    cw = cw_ref[...]
    taps = cw.shape[0]
    outs = (q_ref, k_ref, v_ref)
    for seg, ref in enumerate((pq_ref, pk_ref, pv_ref)):
        cols = slice(seg * W_B, (seg + 1) * W_B)
        x = ref[...].astype(F32)
        y = x * cw[taps - 1:taps, cols]
        for j in range(taps - 1):
            y = y + c0_ref[j, :, cols] * cw[j:j + 1, cols]
            if j >= 1:
                cn_ref[j - 1, :, cols] = c0_ref[j, :, cols]
        cn_ref[taps - 2, :, cols] = x
        outs[seg][...] = _silu(y)
    lane = lax.broadcasted_iota(jnp.int32, eg_ref.shape, 1)
    qk = jnp.zeros(eg_ref.shape, F32)
    for h in range(H_B):
        hc = slice(h * DK_B, (h + 1) * DK_B)
        qh = _l2norm(q_ref[:, hc]) * (DK_B ** -0.5)
        kh = _l2norm(k_ref[:, hc])
        q_ref[:, hc] = qh
        k_ref[:, hc] = kh
        qk = jnp.where(lane == h, jnp.sum(qh * kh, axis=-1, keepdims=True), qk)
    g, beta = _gates(ab_ref[...], alog_ref[...], dtb_ref[...])
    eg_ref[...] = jnp.exp(g)
    beta_ref[...] = beta
    qk_ref[...] = qk


def _gdn_decode_prep(p, ab, conv0_t, cw, alog, dtb):
    nb = p.shape[0]
    c0 = (2 * W_A) // W_B
    col = lambda cidx: pl.BlockSpec((nb, W_B), lambda i, cidx=cidx: (0, cidx))
    full = lambda a: pl.BlockSpec(a.shape, lambda i: (0,) * a.ndim)
    wide = jax.ShapeDtypeStruct((nb, W_B), F32)
    narrow = jax.ShapeDtypeStruct((nb, LANE), F32)
    ospec = lambda s: pl.BlockSpec(s.shape, lambda i: (0,) * len(s.shape))
    outs = (wide, wide, wide, narrow, narrow, narrow, jax.ShapeDtypeStruct(conv0_t.shape, F32))
    return pl.pallas_call(
        _gdn_decode_prep_kernel,
        out_shape=outs,
        grid=(1,),
        in_specs=[col(c0), col(c0 + 1), col(c0 + 2),
                  full(ab), full(conv0_t), full(cw), full(alog), full(dtb)],
        out_specs=tuple(ospec(s) for s in outs),
        compiler_params=_cparams("arbitrary"),
        name="gdn_decode_prep",
    )(p, p, p, ab, conv0_t, cw, alog, dtb)


def _gdn_decode_kernel(s_ref, qt_ref, kt_ref, v_ref, eg_ref, beta_ref, qk_ref, zb_ref, gnw_ref, *refs, bb, n_alias):
    so_ref, ob_ref, o_scr = refs[n_alias:]
    v = v_ref[...]
    eg = eg_ref[...]
    beta = beta_ref[...]
    qk = qk_ref[...]
    for h in range(H_B):
        hc = slice(h * DV_B, (h + 1) * DV_B)
        qt = qt_ref[h]
        kt = kt_ref[h]
        for i in range(bb):
            s_old = s_ref[i, h]
            kc = kt[:, i:i + 1]
            qc = qt[:, i:i + 1]
            egs = eg[i:i + 1, h:h + 1]
            ks = jnp.sum(s_old * kc, axis=0, keepdims=True)
            qs = jnp.sum(s_old * qc, axis=0, keepdims=True)
            v_new = beta[i:i + 1, H_B + h:H_B + h + 1] * (v[i:i + 1, hc] - egs * ks)
            o_scr[i:i + 1, hc] = egs * qs + qk[i:i + 1, h:h + 1] * v_new
            so_ref[i, h] = s_old * egs + kc * v_new
    gnw = gnw_ref[...]
    z = zb_ref[...]
    outs = [_rms(o_scr[:, h * DV_B:(h + 1) * DV_B], gnw) * _silu(z[:, h * DV_B:(h + 1) * DV_B].astype(F32))
            for h in range(H_B)]
    ob_ref[...] = jnp.concatenate(outs, axis=1).astype(BF16)


def _gdn_decode(state, ei, qt, kt, v, eg, beta, qk, p, gnw, bb, s_prev):
    nb = v.shape[0]
    ns = nb // bb
    narrow = pl.BlockSpec((bb, LANE), lambda i: (i, 0))
    tr = pl.BlockSpec((None, H_B, DK_B, bb), lambda i: (i, 0, 0, 0))
    zb0 = (2 * W_A + QKV_B) // W_B
    n_alias = 0 if s_prev is None else 1
    return pl.pallas_call(
        functools.partial(_gdn_decode_kernel, bb=bb, n_alias=n_alias),
        out_shape=(jax.ShapeDtypeStruct(state.shape, F32), jax.ShapeDtypeStruct((nb, W_B), BF16)),
        grid=(ns,),
        in_specs=[pl.BlockSpec((None, bb, H_B, DK_B, DV_B), lambda i: (ei, i, 0, 0, 0)),
                  tr, tr,
                  pl.BlockSpec((bb, W_B), lambda i: (i, 0)),
                  narrow, narrow, narrow,
                  pl.BlockSpec((bb, W_B), lambda i: (i, zb0)),
                  pl.BlockSpec((1, DV_B), lambda i: (0, 0))] + [pl.BlockSpec(memory_space=pl.ANY)] * n_alias,
        out_specs=(pl.BlockSpec((None, bb, H_B, DK_B, DV_B), lambda i: (ei, i, 0, 0, 0)),
                   pl.BlockSpec((bb, W_B), lambda i: (i, 0))),
        input_output_aliases={9: 0} if n_alias else {},
        scratch_shapes=[pltpu.VMEM((bb, W_B), F32)],
        compiler_params=_cparams("arbitrary"),
        name="gdn_decode",
    )(state, qt, kt, v, eg, beta, qk, p, gnw, *(() if s_prev is None else (s_prev,)))


def _out_proj_kernel(x_ref, oa_ref, ob_ref, w_ref, y_ref):
    y_ref[...] = (x_ref[...]
                  + jnp.dot(oa_ref[...], w_ref[0:W_A, :], preferred_element_type=F32)
                  + jnp.dot(ob_ref[...], w_ref[W_A:W_A + W_B, :], preferred_element_type=F32))


def _out_proj(x, oa, ob, w3, li, tm):
    m, d = x.shape
    return pl.pallas_call(
        _out_proj_kernel,
        out_shape=jax.ShapeDtypeStruct((m, d), F32),
        grid=(m // tm,),
        in_specs=[pl.BlockSpec((tm, d), lambda i: (i, 0)),
                  pl.BlockSpec((tm, W_A), lambda i: (i, 0)),
                  pl.BlockSpec((tm, W_B), lambda i: (i, 0)),
                  _layer_spec(w3, li)],
        out_specs=pl.BlockSpec((tm, d), lambda i: (i, 0)),
        compiler_params=_cparams("arbitrary"),
        name="out_proj",
    )(x, oa, ob, w3)


def _odd_prompt_kernel(x_ref, oa_ref, ob_ref, wp_ref, nw_ref, wi_ref, cw_ref, wo_ref, y_ref, sc_ref, carry_ref, *, rows):
    t = pl.program_id(1)
    d = x_ref.shape[1]

    @pl.when(t == 0)
    def _():
        carry_ref[...] = jnp.zeros(carry_ref.shape, F32)

    x = (x_ref[...]
         + jnp.dot(oa_ref[...], wp_ref[0:W_A, :], preferred_element_type=F32)
         + jnp.dot(ob_ref[...], wp_ref[W_A:W_A + W_B, :], preferred_element_type=F32))
    h = _rms_rows(x, nw_ref[...]).astype(BF16)
    proj = lambda c: jnp.dot(h, wi_ref[:, c * d:(c + 1) * d], preferred_element_type=F32)
    u = proj(1) * proj(2)
    cv = _shifted_conv(u, carry_ref[...], cw_ref[...])
    carry_ref[...] = u[rows - 8:rows]
    g = proj(0) * cv * _silu(proj(3))
    y_ref[...] = x + jnp.dot(g.astype(BF16), wo_ref[...], preferred_element_type=F32)

    @pl.when(t == pl.num_programs(1) - 1)
    def _():
        sc_ref[...] = u[rows - 8:rows]


def _odd_prompt(x, oa, ob, w_prev3, ei, nw, w_in3, cw, w_out3, oi, nb, t, rows):
    m, d = x.shape
    nt = t // rows
    full = lambda a: pl.BlockSpec(a.shape, lambda b, i: (0,) * a.ndim)
    row = lambda n: pl.BlockSpec((rows, n), lambda b, i: (b * nt + i, 0))
    return pl.pallas_call(
        functools.partial(_odd_prompt_kernel, rows=rows),
        out_shape=(jax.ShapeDtypeStruct((m, d), F32), jax.ShapeDtypeStruct((nb, 8, d), F32)),
        grid=(nb, nt),
        in_specs=[row(d), row(W_A), row(W_B), _layer_spec(w_prev3, ei), full(nw), _layer_spec(w_in3, oi), full(cw),
                  _layer_spec(w_out3, oi)],
        out_specs=(row(d), pl.BlockSpec((None, 8, d), lambda b, i: (b, 0, 0))),
        scratch_shapes=[pltpu.VMEM((8, d), F32)],
        compiler_params=_cparams("arbitrary", "arbitrary"),
        name="odd_prompt",
    )(x, oa, ob, w_prev3, nw, w_in3, cw, w_out3)


def _odd_decode_kernel(x_ref, bg_ref, cg_ref, hh_ref, z_ref, b0_ref, b1_ref, cw_ref, w_ref, y_ref, u_ref):
    cw = cw_ref[...]
    u = cg_ref[...].astype(F32) * hh_ref[...].astype(F32)
    cv = b0_ref[...] * cw[0:1, :] + b1_ref[...] * cw[1:2, :] + u * cw[2:3, :]
    g = bg_ref[...].astype(F32) * cv * _silu(z_ref[...].astype(F32))
    y_ref[...] = x_ref[...] + jnp.dot(g.astype(BF16), w_ref[...], preferred_element_type=F32)
    u_ref[...] = u


def _odd_decode(x, p, b0, b1, cw, w3, li):
    m, d = x.shape
    col = lambda c: pl.BlockSpec((m, d), lambda i, c=c: (0, c))
    full = lambda a: pl.BlockSpec(a.shape, lambda i: (0,) * a.ndim)
    return pl.pallas_call(
        _odd_decode_kernel,
        out_shape=(jax.ShapeDtypeStruct((m, d), F32), jax.ShapeDtypeStruct((m, d), F32)),
        grid=(1,),
        in_specs=[full(x), col(0), col(1), col(2), col(3), full(b0), full(b1), full(cw), _layer_spec(w3, li)],
        out_specs=(pl.BlockSpec((m, d), lambda i: (0, 0)), pl.BlockSpec((m, d), lambda i: (0, 0))),
        compiler_params=_cparams("arbitrary"),
        name="odd_decode",
    )(x, p, p, p, p, b0, b1, cw, w3)


def _tile(n, want):
    t = math.gcd(n, want)
    assert t == n or t % 8 == 0, (n, want)
    return t


def _pad_lanes(v):
    return jnp.pad(v.astype(F32), (0, LANE - v.shape[0])).reshape(1, LANE)


def kernel(x_prompt, x_sample, cache_k, cache_v, page_table, state_gdn, state_gdn_conv, state_shortconv, norm_w, rel_table, w_in_even, w_out_even, qn_w, kn_w, lam_q1, lam_k1, lam_q2, lam_k2, subln_w, gdn_conv_w, gdn_a_log, gdn_dt_bias, gdn_norm_w, w_in_odd, sc_conv_w, w_out_odd):
    nbp, t, d = x_prompt.shape
    nbs = x_sample.shape[0]
    page = cache_k.shape[2]
    assert x_sample.shape[1] == 1 and page >= MAX_DISTANCE and t % 8 == 0 and DEPTH % 2 == 0
    rows = _tile(t, SEQ_ROWS)
    tq = rows
    assert tq >= MAX_DISTANCE

    xp = x_prompt.reshape(nbp * t, d)
    xs = x_sample.reshape(nbs, d)

    g64 = jnp.arange(W_A) // D_HA
    bd = jnp.where(g64[:, None] == g64[None, :], 1.0 / D_HA, 0.0).astype(BF16)
    bias_p = _bias_tiles(rel_table.astype(F32), tq)
    bias_s = _bias_decode(rel_table.astype(F32), page)

    n_even = (DEPTH + 1) // 2
    w_in_e, w_out_e = w_in_even.astype(BF16), w_out_even.astype(BF16)
    w_in_o, w_out_o = w_in_odd.astype(BF16), w_out_odd.astype(BF16)
    kv_p = kv_s = s_s = None
    sp, gcp, scp = [], [], []
    gcs, scs = [], []
    ei = oi = 0
    for li in range(DEPTH):
        nw = norm_w[li].reshape(1, d)
        if li % 2 == 0:
            lambda_init = 0.8 - 0.6 * math.exp(-0.3 * li)
            w_ab = jnp.pad(w_in_even[ei, :, P_MAIN:], ((0, 0), (0, LANE - 2 * H_B))).astype(BF16)
            qw = jnp.tile(qn_w[ei], W_A // D_HA).reshape(1, W_A)
            kw = jnp.tile(kn_w[ei], W_A // D_HA).reshape(1, W_A)
            sw = subln_w[ei].reshape(1, DA)
            lam = (jnp.exp(jnp.sum(lam_q1[ei] * lam_k1[ei]).astype(F32))
                   - jnp.exp(jnp.sum(lam_q2[ei] * lam_k2[ei]).astype(F32)) + lambda_init).reshape(1)
            cw = gdn_conv_w[ei]
            alog = _pad_lanes(gdn_a_log[ei])
            dtb = _pad_lanes(gdn_dt_bias[ei])
            gnw = gdn_norm_w[ei].reshape(1, DV_B)

            qb, kb, *kv_p, p, ab, tail = _even_in(xp, nw, w_in_e, w_ab, qw, kw, bd, rows, ei, n_even, kv_p,
                                                  conv_w=cw, seq_len=t)
            oa = _attn_prompt(lam, qb, kb, bias_p, p, sw, nbp, t, tq, 1.0 - lambda_init)
            ob, s_new = _gdn_prompt(p, ab, alog, dtb, gnw, nbp, t, rows)
            prev_p = (oa, ob, w_out_e, ei)
            sp.append(s_new)
            gcp.append(tail[:, 8 - (GDN_CONV - 1):, :])

            qb, kb, *kv_s, p, ab = _even_in(xs, nw, w_in_e, w_ab, qw, kw, bd, nbs, ei, n_even, kv_s)
            oa = _attn_decode(page_table, lam, qb, kb, cache_k, cache_v, ei, bias_s, p, sw, 1.0 - lambda_init)
            conv0_t = jnp.swapaxes(state_gdn_conv[ei], 0, 1)
            qn, kn, vv, eg, beta, qk, conv_new = _gdn_decode_prep(p, ab, conv0_t, cw, alog, dtb)
            bb = math.gcd(nbs, DECODE_STATE_SEQS)
            to_cols = lambda a: a.reshape(nbs // bb, bb, H_B, DK_B).transpose(0, 2, 3, 1)
            s_s, ob = _gdn_decode(state_gdn, ei, to_cols(qn), to_cols(kn), vv, eg, beta, qk, p, gnw, bb, s_s)
            xs = _out_proj(xs, oa, ob, w_out_e, ei, nbs)
            gcs.append(jnp.swapaxes(conv_new, 0, 1))
            ei += 1
        else:
            cw = sc_conv_w[oi]
            xp, tail = _odd_prompt(xp, *prev_p, nw, w_in_o, cw, w_out_o, oi, nbp, t, rows)
            scp.append(tail[:, 8 - (SC_WIDTH - 1):, :])

            p = _norm_proj(xs, nw, w_in_o, oi, nbs, DECODE_PROJ_COLS)
            buf0 = state_shortconv[oi]
            xs, u = _odd_decode(xs, p, buf0[:, 0, :], buf0[:, 1, :], cw, w_out_o, oi)
            scs.append(jnp.stack([buf0[:, 1, :], u], axis=1))
            oi += 1

    leaf_p = lambda a: a.reshape(n_even, nbp, t, H_A, DA)
    leaf_s = lambda a: a.reshape(n_even, nbs, 1, H_A, DA)
    return (xp.reshape(nbp, t, d), xs.reshape(nbs, 1, d),
            leaf_p(kv_p[0]), leaf_p(kv_p[1]), jnp.stack(sp), jnp.stack(gcp), jnp.stack(scp),
            leaf_s(kv_s[0]), leaf_s(kv_s[1]), s_s, jnp.stack(gcs), jnp.stack(scs))
```

```python
import functools
import math

import jax
import jax.numpy as jnp
from jax import lax
from jax.experimental import pallas as pl
from jax.experimental.pallas import tpu as pltpu

F32, BF16 = jnp.float32, jnp.bfloat16

DEPTH = 4
H_A, D_HA = 4, 64
DA = 2 * D_HA
W_A = H_A * DA
H_B, DK_B, DV_B = 4, 128, 128
W_B = H_B * DV_B
QKV_B = 2 * H_B * DK_B + H_B * DV_B
GDN_CONV, GDN_CHUNK, SC_WIDTH = 4, 64, 3
NUM_BUCKETS, MAX_EXACT, MAX_DISTANCE = 32, 16, 128
EPS, NEG = 1e-6, -1e30
LOG2E = math.log2(math.e)
LANE = 128
P_MAIN = 4 * W_A + QKV_B + W_B
VMEM_LIMIT = 48 * 1024 * 1024
LOCAL_GROUP = 8
SEQ_ROWS = 512
DECODE_STATE_SEQS = 16
DECODE_PROJ_COLS = 1024
DECODE_ATTN_SEQS = 2


def _layer_spec(w3, li, cols=None):
    return pl.BlockSpec((None, w3.shape[1], cols or w3.shape[2]), lambda *_: (li, 0, 0))


def _cparams(*sem):
    return pltpu.CompilerParams(dimension_semantics=sem, vmem_limit_bytes=VMEM_LIMIT)


def _silu(z):
    h = 0.5 * z
    return h + h * jnp.tanh(h)


def _sigmoid(z):
    return 1.0 / (1.0 + jnp.exp(-z))


def _dot(a, b):
    return jnp.dot(a.astype(BF16), b.astype(BF16), preferred_element_type=F32)


def _dot_nt(a, b):
    return lax.dot_general(a.astype(BF16), b.astype(BF16), (((1,), (1,)), ((), ())), preferred_element_type=F32)


def _rms_rows(x, w):
    return x * lax.rsqrt(jnp.mean(x * x, axis=-1, keepdims=True) + EPS) * w


def _norm_proj_kernel(x_ref, nw_ref, w_ref, o_ref, h_ref):
    @pl.when(pl.program_id(1) == 0)
    def _():
        h_ref[...] = _rms_rows(x_ref[...], nw_ref[...]).astype(BF16)

    o_ref[...] = jnp.dot(h_ref[...], w_ref[...], preferred_element_type=F32).astype(BF16)


def _norm_proj(x, nw, w3, li, tm, tn):
    m, d = x.shape
    n = w3.shape[2]
    return pl.pallas_call(
        _norm_proj_kernel,
        out_shape=jax.ShapeDtypeStruct((m, n), BF16),
        grid=(m // tm, n // tn),
        in_specs=[pl.BlockSpec((tm, d), lambda i, j: (i, 0)),
                  pl.BlockSpec((1, d), lambda i, j: (0, 0)),
                  pl.BlockSpec((None, d, tn), lambda i, j: (li, 0, j))],
        out_specs=pl.BlockSpec((tm, tn), lambda i, j: (i, j)),
        scratch_shapes=[pltpu.VMEM((tm, d), BF16)],
        compiler_params=_cparams("arbitrary", "arbitrary"),
        name="norm_proj",
    )(x, nw, w3)


def _even_in_kernel(x_ref, nw_ref, w_ref, wab_ref, qw_ref, kw_ref, bd_ref, *refs, n_alias, tiles_per_seq):
    if tiles_per_seq:
        cw_ref = refs[0]
        qo_ref, kb_ref, kf_ref, vf_ref, p_ref, ab_ref, tail_ref, carry_ref = refs[1 + n_alias:]
    else:
        qo_ref, kb_ref, kf_ref, vf_ref, p_ref, ab_ref = refs[n_alias:]
    tm = x_ref.shape[0]
    h = _rms_rows(x_ref[...], nw_ref[...]).astype(BF16)
    proj = lambda c: jnp.dot(h, w_ref[:, c * W_A:(c + 1) * W_A], preferred_element_type=F32)
    ab_ref[...] = jnp.dot(h, wab_ref[...], preferred_element_type=F32)
    bd = bd_ref[...]

    def group_norm(x, w):
        ms = jnp.dot((x * x).astype(BF16), bd, preferred_element_type=F32)
        return x * lax.rsqrt(ms + EPS) * w

    def leaf(ref, val):
        for hd in range(H_A):
            ref[pl.ds(hd, tm, stride=H_A), :] = val[:, hd * DA:(hd + 1) * DA]

    qn = group_norm(proj(0), qw_ref[...])
    qo_ref[...] = (qn * (D_HA ** -0.5 * LOG2E)).astype(BF16)
    kn = group_norm(proj(1), kw_ref[...])
    kb_ref[...] = kn.astype(BF16)
    leaf(kf_ref, kn)
    v = proj(2)
    leaf(vf_ref, v)
    p_ref[:, 0:W_A] = v.astype(BF16)
    if tiles_per_seq:
        first = pl.program_id(0) % tiles_per_seq == 0

        @pl.when(first)
        def _():
            carry_ref[...] = jnp.zeros(carry_ref.shape, F32)

    c_qkv = (4 * W_A) // W_A
    for c in range(3, P_MAIN // W_A):
        y = proj(c)
        seg = c - c_qkv
        if tiles_per_seq and 0 <= seg < QKV_B // W_B:
            cols = slice(seg * W_B, (seg + 1) * W_B)
            raw = y
            y = _silu(_shifted_conv(raw, carry_ref[:, cols], cw_ref[:, cols]))
            carry_ref[:, cols] = raw[tm - 8:tm]
            tail_ref[:, cols] = raw[tm - 8:tm]
        p_ref[:, (c - 2) * W_A:(c - 1) * W_A] = y.astype(BF16)


def _even_in(x, nw, w3, w_ab, qw, kw, bd, tm, ei, n_layers, kv_prev, conv_w=None, seq_len=None):
    m, d = x.shape
    tiles_per_seq = 0 if conv_w is None else seq_len // tm
    n_rest = P_MAIN - 2 * W_A
    full = lambda a: pl.BlockSpec(a.shape, lambda i: (0,) * a.ndim)
    row = lambda n: pl.BlockSpec((tm, n), lambda i: (i, 0))
    leaf = pl.BlockSpec((None, tm * H_A, DA), lambda i: (ei, i, 0))
    leaf_shape = jax.ShapeDtypeStruct((n_layers, m * H_A, DA), F32)
    n_alias = 0 if kv_prev is None else 2
    out_shape = [jax.ShapeDtypeStruct((m, W_A), BF16), jax.ShapeDtypeStruct((m, W_A), BF16), leaf_shape, leaf_shape,
                 jax.ShapeDtypeStruct((m, n_rest), BF16), jax.ShapeDtypeStruct((m, LANE), F32)]
    out_specs = [row(W_A), row(W_A), leaf, leaf, row(n_rest), row(LANE)]
    in_specs = [row(d), full(nw), _layer_spec(w3, ei, P_MAIN), full(w_ab), full(qw), full(kw), full(bd)]
    args = [x, nw, w3, w_ab, qw, kw, bd]
    scratch = []
    if tiles_per_seq:
        in_specs.append(full(conv_w))
        args.append(conv_w)
        out_shape.append(jax.ShapeDtypeStruct((m // seq_len, 8, QKV_B), F32))
        out_specs.append(pl.BlockSpec((None, 8, QKV_B), lambda i: (i // tiles_per_seq, 0, 0)))
        scratch.append(pltpu.VMEM((8, QKV_B), F32))
    n_in = len(args)
    return pl.pallas_call(
        functools.partial(_even_in_kernel, n_alias=n_alias, tiles_per_seq=tiles_per_seq),
        out_shape=tuple(out_shape),
        grid=(m // tm,),
        in_specs=in_specs + [pl.BlockSpec(memory_space=pl.ANY)] * n_alias,
        out_specs=tuple(out_specs),
        input_output_aliases={} if kv_prev is None else {n_in: 2, n_in + 1: 3},
        scratch_shapes=scratch,
        compiler_params=_cparams("arbitrary"),
        name="even_in",
    )(*args, *(kv_prev or ()))


def _t5_bias(n, tab_ref, h):
    nf = jnp.maximum(n, 1).astype(F32)
    large = MAX_EXACT + (jnp.log(nf / MAX_EXACT) / math.log(MAX_DISTANCE / MAX_EXACT)
                         * (NUM_BUCKETS - MAX_EXACT)).astype(jnp.int32)
    large = jnp.minimum(large, NUM_BUCKETS - 1)
    bkt = jnp.where(n < MAX_EXACT, n, large)
    out = jnp.zeros(n.shape, F32)
    for b in range(NUM_BUCKETS):
        out = jnp.where(bkt == b, tab_ref[b, h], out)
    return (out - tab_ref[NUM_BUCKETS - 1, h]) * LOG2E


def _bias_tiles_kernel(tab_ref, o_ref, *, tq):
    h = pl.program_id(0)
    i = lax.broadcasted_iota(jnp.int32, (tq, tq), 0)
    j = lax.broadcasted_iota(jnp.int32, (tq, tq), 1)
    n0 = i - j
    o_ref[0, 0] = jnp.where(n0 >= 0, _t5_bias(jnp.maximum(n0, 0), tab_ref, h), NEG)
    o_ref[0, 1] = _t5_bias(n0 + tq, tab_ref, h)


def _bias_tiles(rel_table, tq):
    return pl.pallas_call(
        functools.partial(_bias_tiles_kernel, tq=tq),
        out_shape=jax.ShapeDtypeStruct((H_A, 2, tq, tq), F32),
        grid=(H_A,),
        in_specs=[pl.BlockSpec(memory_space=pltpu.SMEM)],
        out_specs=pl.BlockSpec((1, 2, tq, tq), lambda h: (h, 0, 0, 0)),
        compiler_params=_cparams("arbitrary"),
        name="bias_tiles",
    )(rel_table)


def _bias_decode_kernel(tab_ref, o_ref, *, page):
    w = page * H_A
    row = lax.broadcasted_iota(jnp.int32, (2 * H_A, w), 0)
    lane = lax.broadcasted_iota(jnp.int32, (2 * H_A, w), 1)
    valid = (lane & (H_A - 1)) == (row >> 1)
    n = page - (lane >> 2)
    near = jnp.zeros((2 * H_A, w), F32)
    new = jnp.zeros((2 * H_A, LANE), F32)
    for h in range(H_A):
        near = jnp.where((row >> 1) == h, _t5_bias(n, tab_ref, h), near)
        new = jnp.where((row[:, :LANE] >> 1) == h, _t5_bias(jnp.zeros((2 * H_A, LANE), jnp.int32), tab_ref, h), new)
    o_ref[:, 0:w] = jnp.where(valid, 0.0, NEG)
    o_ref[:, w:2 * w] = jnp.where(valid, near, NEG)
    o_ref[:, 2 * w:2 * w + LANE] = new


def _bias_decode(rel_table, page):
    assert H_A == 4
    return pl.pallas_call(
        functools.partial(_bias_decode_kernel, page=page),
        out_shape=jax.ShapeDtypeStruct((2 * H_A, 2 * page * H_A + LANE), F32),
        in_specs=[pl.BlockSpec(memory_space=pltpu.SMEM)],
        out_specs=pl.BlockSpec(memory_space=pltpu.VMEM),
        name="bias_decode",
    )(rel_table)


def _attn_finish(o, sw, z, out_scale):
    ms = jnp.mean(o * o, axis=-1, keepdims=True)
    return o * lax.rsqrt(ms + EPS) * sw * out_scale * _silu(z.astype(F32))


def _attn_prompt_kernel(lam_ref, q_ref, k_ref, v_ref, bias_ref, za_ref, sw_ref, o_ref, m, l, a, *, tq, out_scale):
    reps = tq // LANE
    lane = lax.broadcasted_iota(jnp.int32, (tq, DA), 1)

    def q_block(qi, carry):
        qrows = pl.ds(pl.multiple_of(qi * tq, tq), tq)
        q = q_ref[qrows, :].astype(F32)
        q2 = jnp.concatenate([jnp.where(lane < D_HA, q, 0.0), jnp.where(lane >= D_HA, q, 0.0)], axis=0).astype(BF16)
        m[...] = jnp.full(m.shape, NEG, F32)
        l[...] = jnp.zeros(l.shape, F32)
        a[...] = jnp.zeros(a.shape, F32)

        def step(j, bias):
            rows = pl.ds(pl.multiple_of(j * tq, tq), tq)
            s = lax.dot_general(q2, k_ref[rows, :], (((1,), (1,)), ((), ())), preferred_element_type=F32)
            if bias is not None:
                s = s + jnp.concatenate([bias, bias], axis=0)
            m_prev = m[...]
            m_new = jnp.maximum(m_prev, jnp.max(s, axis=1, keepdims=True))
            p = jnp.exp2(s - jnp.concatenate([m_new] * reps, axis=1))
            alpha = jnp.exp2(m_prev - m_new)
            l[...] = alpha * l[...] + jnp.sum(p, axis=1, keepdims=True)
            a[...] = alpha * a[...] + jnp.dot(p.astype(BF16), v_ref[rows, :], preferred_element_type=F32)
            m[...] = m_new

        def far(j, c):
            step(j, None)
            return c

        lax.fori_loop(0, jnp.maximum(qi - 1, 0), far, 0)

        @pl.when(qi >= 1)
        def _():
            step(qi - 1, bias_ref[0, 1])

        step(qi, bias_ref[0, 0])
        o = a[...] / l[...]
        o = o[:tq] - lam_ref[0] * o[tq:]
        o_ref[qrows, :] = _attn_finish(o, sw_ref[...], za_ref[qrows, :], out_scale).astype(BF16)
        return carry

    lax.fori_loop(0, q_ref.shape[0] // tq, q_block, 0)


def _attn_prompt(lam, q, k, bias, p, sw, nb, t, tq, out_scale):
    m = q.shape[0]
    za0 = W_A // DA
    v0 = 0
    seq = lambda c0: pl.BlockSpec((t, DA), lambda b, h, c0=c0: (b, c0 + h))
    return pl.pallas_call(
        functools.partial(_attn_prompt_kernel, tq=tq, out_scale=out_scale),
        out_shape=jax.ShapeDtypeStruct((m, W_A), BF16),
        grid=(nb, H_A),
        in_specs=[pl.BlockSpec(memory_space=pltpu.SMEM),
                  seq(0), seq(0), seq(v0),
                  pl.BlockSpec((1, 2, tq, tq), lambda b, h: (h, 0, 0, 0)),
                  seq(za0),
                  pl.BlockSpec((1, DA), lambda b, h: (0, 0))],
        out_specs=seq(0),
        scratch_shapes=[pltpu.VMEM((2 * tq, LANE), F32), pltpu.VMEM((2 * tq, LANE), F32),
                        pltpu.VMEM((2 * tq, DA), F32)],
        compiler_params=_cparams("arbitrary", "arbitrary"),
        name="attn_prompt",
    )(lam, q, k, p, bias, p, sw)


def _attn_decode_kernel(pt_ref, lam_ref, q_ref, kn_ref, vn_ref, *refs, page, npg, nseq, out_scale):
    ck_refs, cv_refs = refs[:nseq * npg], refs[nseq * npg:2 * nseq * npg]
    bias_ref, za_ref, sw_ref, o_ref = refs[2 * nseq * npg:]
    rows, w = 2 * H_A, page * H_A
    row = lax.broadcasted_iota(jnp.int32, (rows, DA), 0)
    lane = lax.broadcasted_iota(jnp.int32, (rows, DA), 1)

    def head_rows(x):
        out = jnp.zeros((rows, DA), F32)
        for h in range(H_A):
            out = jnp.where((row >> 1) == h, jnp.broadcast_to(x[:, h * DA:(h + 1) * DA], (rows, DA)), out)
        return out

    sw = sw_ref[...]
    for sq in range(nseq):
        q8 = jnp.where((lane >> 6) == (row & 1), head_rows(q_ref[sq].astype(F32)), 0.0)
        q8b = q8.astype(BF16)
        s_all = []
        for j in range(npg):
            s = lax.dot_general(q8b, ck_refs[sq * npg + j][...].astype(BF16), (((1,), (1,)), ((), ())),
                                preferred_element_type=F32)
            s_all.append(s + (bias_ref[:, w:2 * w] if j == npg - 1 else bias_ref[:, 0:w]))
        s_new = (jnp.sum(q8 * head_rows(kn_ref[sq].astype(F32)), axis=1, keepdims=True)
                 + bias_ref[:, 2 * w:2 * w + 1])
        m = s_all[0]
        for s in s_all[1:]:
            m = jnp.maximum(m, s)
        m = jnp.maximum(jnp.max(m, axis=1, keepdims=True), s_new)
        p_new = jnp.exp2(s_new - m)
        l = p_new
        acc = p_new * head_rows(vn_ref[sq].astype(F32))
        for j in range(npg):
            p = jnp.exp2(s_all[j] - m)
            l = l + jnp.sum(p, axis=1, keepdims=True)
            acc = acc + jnp.dot(p.astype(BF16), cv_refs[sq * npg + j][...].astype(BF16), preferred_element_type=F32)
        o = acc / l
        z = za_ref[sq]
        outs = []
        for h in range(H_A):
            oh = o[2 * h:2 * h + 1, :] - lam_ref[0] * o[2 * h + 1:2 * h + 2, :]
            outs.append(_attn_finish(oh, sw, z[:, h * DA:(h + 1) * DA], out_scale))
        o_ref[sq] = jnp.concatenate(outs, axis=1).astype(BF16)


def _attn_decode(page_table, lam, q, kn, cache_k, cache_v, ei, bias, p, sw, out_scale):
    nb, npg = page_table.shape
    page = cache_k.shape[2]
    nseq = math.gcd(nb, DECODE_ATTN_SEQS)
    ck = cache_k.reshape(cache_k.shape[0], cache_k.shape[1], page * H_A, DA)
    cv = cache_v.reshape(cache_v.shape[0], cache_v.shape[1], page * H_A, DA)
    r3 = lambda a: a.reshape(nb, 1, a.shape[-1])
    vec = lambda c: pl.BlockSpec((nseq, 1, W_A), lambda b, pt, c=c: (b, 0, c))
    cache = [pl.BlockSpec((None, None, page * H_A, DA),
                          lambda b, pt, sq=sq, j=j: (ei, pt[(b * nseq + sq) * npg + j], 0, 0))
             for sq in range(nseq) for j in range(npg)]
    out = pl.pallas_call(
        functools.partial(_attn_decode_kernel, page=page, npg=npg, nseq=nseq, out_scale=out_scale),
        out_shape=jax.ShapeDtypeStruct((nb, 1, W_A), BF16),
        grid_spec=pltpu.PrefetchScalarGridSpec(
            num_scalar_prefetch=1,
            grid=(nb // nseq,),
            in_specs=[pl.BlockSpec(memory_space=pltpu.SMEM), vec(0), vec(0), vec(0)] + cache + cache
                     + [pl.BlockSpec(bias.shape, lambda b, pt: (0, 0)),
                        vec(1),
                        pl.BlockSpec((1, DA), lambda b, pt: (0, 0))],
            out_specs=pl.BlockSpec((nseq, 1, W_A), lambda b, pt: (b, 0, 0))),
        compiler_params=_cparams("arbitrary"),
        name="attn_decode",
    )(page_table.reshape(-1), lam, r3(q), r3(kn), r3(p), *([ck] * (nseq * npg)), *([cv] * (nseq * npg)), bias, r3(p), sw)
    return out.reshape(nb, W_A)


def _shifted_conv(x, prev, w):
    r, c = x.shape
    taps = w.shape[0]
    x3 = x.reshape(r // 8, 8, c)
    sub = lax.broadcasted_iota(jnp.int32, x3.shape, 1)
    y = x3 * w[taps - 1:taps, :].reshape(1, 1, c)
    for s in range(1, taps):
        rot = pltpu.roll(x3, s, 1)
        before = jnp.concatenate([pltpu.roll(prev, s, 0)[None], rot[:-1]], axis=0)
        y = y + jnp.where(sub < s, before, rot) * w[taps - 1 - s:taps - s, :].reshape(1, 1, c)
    return y.reshape(r, c)


def _gates(ab, alog, dtb):
    x = ab + dtb
    sp = jnp.maximum(x, 0.0) + jnp.log(1.0 + jnp.exp(-jnp.abs(x)))
    return -jnp.exp(alog) * sp, _sigmoid(ab)


def _l2norm(x):
    return x * lax.rsqrt(jnp.sum(x * x, axis=-1, keepdims=True) + EPS)


def _rms(x, w):
    return x * lax.rsqrt(jnp.mean(x * x, axis=-1, keepdims=True) + EPS) * w


def _level_masks(c):
    i = lax.broadcasted_iota(jnp.int32, (c, c), 0)
    j = lax.broadcasted_iota(jnp.int32, (c, c), 1)
    masks = []
    s = 1
    while s < c:
        sh = s.bit_length() - 1
        masks.append((((i >> (sh + 1)) == (j >> (sh + 1))) & ((i >> sh) != (j >> sh)) & (i > j)).astype(F32))
        s *= 2
    return masks


def _gdn_prompt_kernel(cq_ref, ck_ref, cv_ref, zb_ref, ab_ref, alog_ref, dtb_ref, gnw_ref,
                       ob_ref, s_ref, g_ref, beta_ref, u_ref, wq_ref, ak_ref, gl_ref, *, rows, chunk, nseq):
    t = pl.program_id(1)
    c = chunk

    @pl.when(t == 0)
    def _():
        s_ref[...] = jnp.zeros(s_ref.shape, F32)

    ii = lax.broadcasted_iota(jnp.int32, (c, c), 0)
    jj = lax.broadcasted_iota(jnp.int32, (c, c), 1)
    incl = ii >= jj
    ltri = incl.astype(BF16)
    masks = _level_masks(c)

    def prep(b, ci):
        rs = pl.ds(pl.multiple_of(ci * c, c), c)
        gch = g_ref[b, rs, :]
        g1 = gch.astype(BF16)
        r1 = gch - g1.astype(F32)
        g2 = r1.astype(BF16)
        g3 = (r1 - g2.astype(F32)).astype(BF16)
        d = functools.partial(jnp.dot, preferred_element_type=F32)
        gc = d(ltri, g1) + d(ltri, g2) + d(ltri, g3)
        gct = jnp.concatenate([gc, gc], axis=0).T
        ge = jnp.exp(gc)
        kdec = jnp.exp(gc[c - 1:c, :] - gc)
        gl_ref[b, ci] = ge[c - 1:c, :]
        return rs, gc, gct, ge, kdec, beta_ref[b, rs, :]

    def local(b, ig):
        items = []
        for ci in [LOCAL_GROUP * ig + k for k in range(LOCAL_GROUP)]:
            rs, gc, gct, ge, kdec, bch = prep(b, ci)
            for h in range(H_B):
                hc = slice(h * DK_B, (h + 1) * DK_B)
                qh = _l2norm(cq_ref[b, rs, hc].astype(F32)) * (DK_B ** -0.5)
                kh = _l2norm(ck_ref[b, rs, hc].astype(F32))
                vh = cv_ref[b, rs, hc].astype(F32)
                bcol = bch[:, H_B + h:H_B + h + 1]
                gecol = ge[:, h:h + 1]
                decay = jnp.exp(jnp.where(incl, gc[:, h:h + 1] - gct[h:h + 1, :c], NEG))
                kbeta = kh * bcol
                wq_ref[b, ci, h, c:2 * c] = (qh * gecol).astype(BF16)
                ak_ref[b, ci, h, c:c + DK_B] = (kh * kdec[:, h:h + 1]).T.astype(BF16)
                items.append(dict(ci=ci, h=h, rs=rs, hc=hc, decay=decay, qk=(kbeta, qh, kh),
                                  rhs=jnp.concatenate([vh * bcol, kbeta * gecol], axis=1)))
        for it in items:
            kbeta, qh, kh = it.pop("qk")
            both = _dot_nt(jnp.concatenate([kbeta, qh], axis=0), kh)
            it["mm"] = jnp.where(ii > jj, both[:c] * it["decay"], 0.0)
            ak_ref[b, it["ci"], it["h"], 0:c] = (both[c:] * it["decay"]).astype(BF16)
        for it in items:
            it["pp"] = -(it["mm"] * masks[0])
        for mk in masks[1:]:
            for it in items:
                e = it["mm"] * mk
                it["x"] = e + _dot(it["pp"], e)
            for it in items:
                it["pp"] = it["pp"] - (it["x"] + _dot(it["x"], it["pp"]))
        for it in items:
            uw = it["rhs"] + _dot(it["pp"], it["rhs"])
            u_ref[b, it["rs"], it["hc"]] = uw[:, :DV_B]
            wq_ref[b, it["ci"], it["h"], 0:c] = uw[:, DV_B:].astype(BF16)

    for b in range(nseq):
        g, beta = _gates(ab_ref[b], alog_ref[...], dtb_ref[...])
        g_ref[b] = g
        beta_ref[b] = beta

        def local_b(ig, carry, b=b):
            local(b, ig)
            return carry

        lax.fori_loop(0, rows // (LOCAL_GROUP * c), local_b, 0)
    gnw = gnw_ref[...]

    def scan(ci, carry):
        rs = pl.ds(pl.multiple_of(ci * c, c), c)
        chains = [(b, h) for b in range(nseq) for h in range(H_B)]
        hc = lambda h: slice(h * DK_B, (h + 1) * DK_B)
        s_old = [s_ref[b, h] for b, h in chains]
        r = [jnp.dot(wq_ref[b, ci, h], s_old[k].astype(BF16), preferred_element_type=F32)
             for k, (b, h) in enumerate(chains)]
        v_new = [u_ref[b, rs, hc(h)] - r[k][:c] for k, (b, h) in enumerate(chains)]
        r2 = [jnp.dot(ak_ref[b, ci, h], v_new[k].astype(BF16), preferred_element_type=F32)
              for k, (b, h) in enumerate(chains)]
        for k, (b, h) in enumerate(chains):
            s_ref[b, h] = s_old[k] * gl_ref[b, ci][:, h:h + 1] + r2[k][c:]
            ob_ref[b, rs, hc(h)] = (_rms(r[k][c:] + r2[k][:c], gnw)
                                    * _silu(zb_ref[b, rs, hc(h)].astype(F32))).astype(BF16)
        return carry

    lax.fori_loop(0, rows // c, scan, 0)


def _gdn_prompt(p, ab, alog, dtb, gnw, nb, t, rows):
    m = p.shape[0]
    nt = t // rows
    c = math.gcd(GDN_CHUNK, t)
    nseq = 2 if nb % 2 == 0 else 1
    p3 = p.reshape(nb, t, p.shape[1])
    col = lambda cidx: pl.BlockSpec((nseq, rows, W_B), lambda b, i, cidx=cidx: (b, i, cidx))
    full = lambda a: pl.BlockSpec(a.shape, lambda b, i: (0,) * a.ndim)
    c0 = (2 * W_A) // W_B
    ob, s_new = pl.pallas_call(
        functools.partial(_gdn_prompt_kernel, rows=rows, chunk=c, nseq=nseq),
        out_shape=(jax.ShapeDtypeStruct((nb, t, W_B), BF16), jax.ShapeDtypeStruct((nb, H_B, DK_B, DV_B), F32)),
        grid=(nb // nseq, nt),
        in_specs=[col(c0), col(c0 + 1), col(c0 + 2), col(c0 + 3),
                  pl.BlockSpec((nseq, rows, LANE), lambda b, i: (b, i, 0)),
                  full(alog), full(dtb), full(gnw)],
        out_specs=(pl.BlockSpec((nseq, rows, W_B), lambda b, i: (b, i, 0)),
                   pl.BlockSpec((nseq, H_B, DK_B, DV_B), lambda b, i: (b, 0, 0, 0))),
        scratch_shapes=[pltpu.VMEM((nseq, rows, LANE), F32), pltpu.VMEM((nseq, rows, LANE), F32),
                        pltpu.VMEM((nseq, rows, W_B), F32),
                        pltpu.VMEM((nseq, rows // c, H_B, 2 * c, DK_B), BF16),
                        pltpu.VMEM((nseq, rows // c, H_B, c + DK_B, c), BF16),
                        pltpu.VMEM((nseq, rows // c, 1, LANE), F32)],
        compiler_params=_cparams("arbitrary", "arbitrary"),
        name="gdn_prompt",
    )(p3, p3, p3, p3, ab.reshape(nb, t, LANE), alog, dtb, gnw)
    return ob.reshape(m, W_B), s_new


def _gdn_decode_prep_kernel(pq_ref, pk_ref, pv_ref, ab_ref, c0_ref, cw_ref, alog_ref, dtb_ref,
                            q_ref, k_ref, v_ref, eg_ref, beta_ref, qk_ref, cn_ref):
    cw = cw_ref[...]
    taps = cw.shape[0]
    outs = (q_ref, k_ref, v_ref)
    for seg, ref in enumerate((pq_ref, pk_ref, pv_ref)):
        cols = slice(seg * W_B, (seg + 1) * W_B)
        x = ref[...].astype(F32)
        y = x * cw[taps - 1:taps, cols]
        for j in range(taps - 1):
            y = y + c0_ref[j, :, cols] * cw[j:j + 1, cols]
            if j >= 1:
                cn_ref[j - 1, :, cols] = c0_ref[j, :, cols]
        cn_ref[taps - 2, :, cols] = x
        outs[seg][...] = _silu(y)
    lane = lax.broadcasted_iota(jnp.int32, eg_ref.shape, 1)
    qk = jnp.zeros(eg_ref.shape, F32)
    for h in range(H_B):
        hc = slice(h * DK_B, (h + 1) * DK_B)
        qh = _l2norm(q_ref[:, hc]) * (DK_B ** -0.5)
        kh = _l2norm(k_ref[:, hc])
        q_ref[:, hc] = qh
        k_ref[:, hc] = kh
        qk = jnp.where(lane == h, jnp.sum(qh * kh, axis=-1, keepdims=True), qk)
    g, beta = _gates(ab_ref[...], alog_ref[...], dtb_ref[...])
    eg_ref[...] = jnp.exp(g)
    beta_ref[...] = beta
    qk_ref[...] = qk


def _gdn_decode_prep(p, ab, conv0_t, cw, alog, dtb):
    nb = p.shape[0]
    c0 = (2 * W_A) // W_B
    col = lambda cidx: pl.BlockSpec((nb, W_B), lambda i, cidx=cidx: (0, cidx))
    full = lambda a: pl.BlockSpec(a.shape, lambda i: (0,) * a.ndim)
    wide = jax.ShapeDtypeStruct((nb, W_B), F32)
    narrow = jax.ShapeDtypeStruct((nb, LANE), F32)
    ospec = lambda s: pl.BlockSpec(s.shape, lambda i: (0,) * len(s.shape))
    outs = (wide, wide, wide, narrow, narrow, narrow, jax.ShapeDtypeStruct(conv0_t.shape, F32))
    return pl.pallas_call(
        _gdn_decode_prep_kernel,
        out_shape=outs,
        grid=(1,),
        in_specs=[col(c0), col(c0 + 1), col(c0 + 2),
                  full(ab), full(conv0_t), full(cw), full(alog), full(dtb)],
        out_specs=tuple(ospec(s) for s in outs),
        compiler_params=_cparams("arbitrary"),
        name="gdn_decode_prep",
    )(p, p, p, ab, conv0_t, cw, alog, dtb)


def _gdn_decode_kernel(s_ref, qt_ref, kt_ref, v_ref, eg_ref, beta_ref, qk_ref, zb_ref, gnw_ref, *refs, bb, n_alias):
    so_ref, ob_ref, o_scr = refs[n_alias:]
    v = v_ref[...]
    eg = eg_ref[...]
    beta = beta_ref[...]
    qk = qk_ref[...]
    for h in range(H_B):
        hc = slice(h * DV_B, (h + 1) * DV_B)
        qt = qt_ref[h]
        kt = kt_ref[h]
        for i in range(bb):
            s_old = s_ref[i, h]
            kc = kt[:, i:i + 1]
            qc = qt[:, i:i + 1]
            egs = eg[i:i + 1, h:h + 1]
            ks = jnp.sum(s_old * kc, axis=0, keepdims=True)
            qs = jnp.sum(s_old * qc, axis=0, keepdims=True)
            v_new = beta[i:i + 1, H_B + h:H_B + h + 1] * (v[i:i + 1, hc] - egs * ks)
            o_scr[i:i + 1, hc] = egs * qs + qk[i:i + 1, h:h + 1] * v_new
            so_ref[i, h] = s_old * egs + kc * v_new
    gnw = gnw_ref[...]
    z = zb_ref[...]
    outs = [_rms(o_scr[:, h * DV_B:(h + 1) * DV_B], gnw) * _silu(z[:, h * DV_B:(h + 1) * DV_B].astype(F32))
            for h in range(H_B)]
    ob_ref[...] = jnp.concatenate(outs, axis=1).astype(BF16)


def _gdn_decode(state, ei, qt, kt, v, eg, beta, qk, p, gnw, bb, s_prev):
    nb = v.shape[0]
    ns = nb // bb
    narrow = pl.BlockSpec((bb, LANE), lambda i: (i, 0))
    tr = pl.BlockSpec((None, H_B, DK_B, bb), lambda i: (i, 0, 0, 0))
    zb0 = (2 * W_A + QKV_B) // W_B
    n_alias = 0 if s_prev is None else 1
    return pl.pallas_call(
        functools.partial(_gdn_decode_kernel, bb=bb, n_alias=n_alias),
        out_shape=(jax.ShapeDtypeStruct(state.shape, F32), jax.ShapeDtypeStruct((nb, W_B), BF16)),
        grid=(ns,),
        in_specs=[pl.BlockSpec((None, bb, H_B, DK_B, DV_B), lambda i: (ei, i, 0, 0, 0)),
                  tr, tr,
                  pl.BlockSpec((bb, W_B), lambda i: (i, 0)),
                  narrow, narrow, narrow,
                  pl.BlockSpec((bb, W_B), lambda i: (i, zb0)),
                  pl.BlockSpec((1, DV_B), lambda i: (0, 0))] + [pl.BlockSpec(memory_space=pl.ANY)] * n_alias,
        out_specs=(pl.BlockSpec((None, bb, H_B, DK_B, DV_B), lambda i: (ei, i, 0, 0, 0)),
                   pl.BlockSpec((bb, W_B), lambda i: (i, 0))),
        input_output_aliases={9: 0} if n_alias else {},
        scratch_shapes=[pltpu.VMEM((bb, W_B), F32)],
        compiler_params=_cparams("arbitrary"),
        name="gdn_decode",
    )(state, qt, kt, v, eg, beta, qk, p, gnw, *(() if s_prev is None else (s_prev,)))


def _out_proj_kernel(x_ref, oa_ref, ob_ref, w_ref, y_ref):
    y_ref[...] = (x_ref[...]
                  + jnp.dot(oa_ref[...], w_ref[0:W_A, :], preferred_element_type=F32)
                  + jnp.dot(ob_ref[...], w_ref[W_A:W_A + W_B, :], preferred_element_type=F32))


def _out_proj(x, oa, ob, w3, li, tm):
    m, d = x.shape
    return pl.pallas_call(
        _out_proj_kernel,
        out_shape=jax.ShapeDtypeStruct((m, d), F32),
        grid=(m // tm,),
        in_specs=[pl.BlockSpec((tm, d), lambda i: (i, 0)),
                  pl.BlockSpec((tm, W_A), lambda i: (i, 0)),
                  pl.BlockSpec((tm, W_B), lambda i: (i, 0)),
                  _layer_spec(w3, li)],
        out_specs=pl.BlockSpec((tm, d), lambda i: (i, 0)),
        compiler_params=_cparams("arbitrary"),
        name="out_proj",
    )(x, oa, ob, w3)


def _odd_prompt_kernel(x_ref, oa_ref, ob_ref, wp_ref, nw_ref, wi_ref, cw_ref, wo_ref, y_ref, sc_ref, carry_ref, *, rows):
    t = pl.program_id(1)
    d = x_ref.shape[1]

    @pl.when(t == 0)
    def _():
        carry_ref[...] = jnp.zeros(carry_ref.shape, F32)

    x = (x_ref[...]
         + jnp.dot(oa_ref[...], wp_ref[0:W_A, :], preferred_element_type=F32)
         + jnp.dot(ob_ref[...], wp_ref[W_A:W_A + W_B, :], preferred_element_type=F32))
    h = _rms_rows(x, nw_ref[...]).astype(BF16)
    proj = lambda c: jnp.dot(h, wi_ref[:, c * d:(c + 1) * d], preferred_element_type=F32)
    u = proj(1) * proj(2)
    cv = _shifted_conv(u, carry_ref[...], cw_ref[...])
    carry_ref[...] = u[rows - 8:rows]
    g = proj(0) * cv * _silu(proj(3))
    y_ref[...] = x + jnp.dot(g.astype(BF16), wo_ref[...], preferred_element_type=F32)

    @pl.when(t == pl.num_programs(1) - 1)
    def _():
        sc_ref[...] = u[rows - 8:rows]


def _odd_prompt(x, oa, ob, w_prev3, ei, nw, w_in3, cw, w_out3, oi, nb, t, rows):
    m, d = x.shape
    nt = t // rows
    full = lambda a: pl.BlockSpec(a.shape, lambda b, i: (0,) * a.ndim)
    row = lambda n: pl.BlockSpec((rows, n), lambda b, i: (b * nt + i, 0))
    return pl.pallas_call(
        functools.partial(_odd_prompt_kernel, rows=rows),
        out_shape=(jax.ShapeDtypeStruct((m, d), F32), jax.ShapeDtypeStruct((nb, 8, d), F32)),
        grid=(nb, nt),
        in_specs=[row(d), row(W_A), row(W_B), _layer_spec(w_prev3, ei), full(nw), _layer_spec(w_in3, oi), full(cw),
                  _layer_spec(w_out3, oi)],
        out_specs=(row(d), pl.BlockSpec((None, 8, d), lambda b, i: (b, 0, 0))),
        scratch_shapes=[pltpu.VMEM((8, d), F32)],
        compiler_params=_cparams("arbitrary", "arbitrary"),
        name="odd_prompt",
    )(x, oa, ob, w_prev3, nw, w_in3, cw, w_out3)


def _odd_decode_kernel(x_ref, bg_ref, cg_ref, hh_ref, z_ref, b0_ref, b1_ref, cw_ref, w_ref, y_ref, u_ref):
    cw = cw_ref[...]
    u = cg_ref[...].astype(F32) * hh_ref[...].astype(F32)
    cv = b0_ref[...] * cw[0:1, :] + b1_ref[...] * cw[1:2, :] + u * cw[2:3, :]
    g = bg_ref[...].astype(F32) * cv * _silu(z_ref[...].astype(F32))
    y_ref[...] = x_ref[...] + jnp.dot(g.astype(BF16), w_ref[...], preferred_element_type=F32)
    u_ref[...] = u


def _odd_decode(x, p, b0, b1, cw, w3, li):
    m, d = x.shape
    col = lambda c: pl.BlockSpec((m, d), lambda i, c=c: (0, c))
    full = lambda a: pl.BlockSpec(a.shape, lambda i: (0,) * a.ndim)
    return pl.pallas_call(
        _odd_decode_kernel,
        out_shape=(jax.ShapeDtypeStruct((m, d), F32), jax.ShapeDtypeStruct((m, d), F32)),
        grid=(1,),
        in_specs=[full(x), col(0), col(1), col(2), col(3), full(b0), full(b1), full(cw), _layer_spec(w3, li)],
        out_specs=(pl.BlockSpec((m, d), lambda i: (0, 0)), pl.BlockSpec((m, d), lambda i: (0, 0))),
        compiler_params=_cparams("arbitrary"),
        name="odd_decode",
    )(x, p, p, p, p, b0, b1, cw, w3)


def _tile(n, want):
    t = math.gcd(n, want)
    assert t == n or t % 8 == 0, (n, want)
    return t


def _pad_lanes(v):
    return jnp.pad(v.astype(F32), (0, LANE - v.shape[0])).reshape(1, LANE)


def kernel(x_prompt, x_sample, cache_k, cache_v, page_table, state_gdn, state_gdn_conv, state_shortconv, norm_w, rel_table, w_in_even, w_out_even, qn_w, kn_w, lam_q1, lam_k1, lam_q2, lam_k2, subln_w, gdn_conv_w, gdn_a_log, gdn_dt_bias, gdn_norm_w, w_in_odd, sc_conv_w, w_out_odd):
    nbp, t, d = x_prompt.shape
    nbs = x_sample.shape[0]
    page = cache_k.shape[2]
    assert x_sample.shape[1] == 1 and page >= MAX_DISTANCE and t % 8 == 0 and DEPTH % 2 == 0
    rows = _tile(t, SEQ_ROWS)
    tq = rows
    assert tq >= MAX_DISTANCE

    xp = x_prompt.reshape(nbp * t, d)
    xs = x_sample.reshape(nbs, d)

    g64 = jnp.arange(W_A) // D_HA
    bd = jnp.where(g64[:, None] == g64[None, :], 1.0 / D_HA, 0.0).astype(BF16)
    bias_p = _bias_tiles(rel_table.astype(F32), tq)
    bias_s = _bias_decode(rel_table.astype(F32), page)

    n_even = (DEPTH + 1) // 2
    w_in_e, w_out_e = w_in_even.astype(BF16), w_out_even.astype(BF16)
    w_in_o, w_out_o = w_in_odd.astype(BF16), w_out_odd.astype(BF16)
    kv_p = kv_s = s_s = None
    sp, gcp, scp = [], [], []
    gcs, scs = [], []
    ei = oi = 0
    for li in range(DEPTH):
        nw = norm_w[li].reshape(1, d)
        if li % 2 == 0:
            lambda_init = 0.8 - 0.6 * math.exp(-0.3 * li)
            w_ab = jnp.pad(w_in_even[ei, :, P_MAIN:], ((0, 0), (0, LANE - 2 * H_B))).astype(BF16)
            qw = jnp.tile(qn_w[ei], W_A // D_HA).reshape(1, W_A)
            kw = jnp.tile(kn_w[ei], W_A // D_HA).reshape(1, W_A)
            sw = subln_w[ei].reshape(1, DA)
            lam = (jnp.exp(jnp.sum(lam_q1[ei] * lam_k1[ei]).astype(F32))
                   - jnp.exp(jnp.sum(lam_q2[ei] * lam_k2[ei]).astype(F32)) + lambda_init).reshape(1)
            cw = gdn_conv_w[ei]
            alog = _pad_lanes(gdn_a_log[ei])
            dtb = _pad_lanes(gdn_dt_bias[ei])
            gnw = gdn_norm_w[ei].reshape(1, DV_B)

            qb, kb, *kv_p, p, ab, tail = _even_in(xp, nw, w_in_e, w_ab, qw, kw, bd, rows, ei, n_even, kv_p,
                                                  conv_w=cw, seq_len=t)
            oa = _attn_prompt(lam, qb, kb, bias_p, p, sw, nbp, t, tq, 1.0 - lambda_init)
            ob, s_new = _gdn_prompt(p, ab, alog, dtb, gnw, nbp, t, rows)
            prev_p = (oa, ob, w_out_e, ei)
            sp.append(s_new)
            gcp.append(tail[:, 8 - (GDN_CONV - 1):, :])

            qb, kb, *kv_s, p, ab = _even_in(xs, nw, w_in_e, w_ab, qw, kw, bd, nbs, ei, n_even, kv_s)
            oa = _attn_decode(page_table, lam, qb, kb, cache_k, cache_v, ei, bias_s, p, sw, 1.0 - lambda_init)
            conv0_t = jnp.swapaxes(state_gdn_conv[ei], 0, 1)
            qn, kn, vv, eg, beta, qk, conv_new = _gdn_decode_prep(p, ab, conv0_t, cw, alog, dtb)
            bb = math.gcd(nbs, DECODE_STATE_SEQS)
            to_cols = lambda a: a.reshape(nbs // bb, bb, H_B, DK_B).transpose(0, 2, 3, 1)
            s_s, ob = _gdn_decode(state_gdn, ei, to_cols(qn), to_cols(kn), vv, eg, beta, qk, p, gnw, bb, s_s)
            xs = _out_proj(xs, oa, ob, w_out_e, ei, nbs)
            gcs.append(jnp.swapaxes(conv_new, 0, 1))
            ei += 1
        else:
            cw = sc_conv_w[oi]
            xp, tail = _odd_prompt(xp, *prev_p, nw, w_in_o, cw, w_out_o, oi, nbp, t, rows)
            scp.append(tail[:, 8 - (SC_WIDTH - 1):, :])

            p = _norm_proj(xs, nw, w_in_o, oi, nbs, DECODE_PROJ_COLS)
            buf0 = state_shortconv[oi]
            xs, u = _odd_decode(xs, p, buf0[:, 0, :], buf0[:, 1, :], cw, w_out_o, oi)
            scs.append(jnp.stack([buf0[:, 1, :], u], axis=1))
            oi += 1

    leaf_p = lambda a: a.reshape(n_even, nbp, t, H_A, DA)
    leaf_s = lambda a: a.reshape(n_even, nbs, 1, H_A, DA)
    return (xp.reshape(nbp, t, d), xs.reshape(nbs, 1, d),
            leaf_p(kv_p[0]), leaf_p(kv_p[1]), jnp.stack(sp), jnp.stack(gcp), jnp.stack(scp),
            leaf_s(kv_s[0]), leaf_s(kv_s[1]), s_s, jnp.stack(gcs), jnp.stack(scs))
```
